```python
import math
import jax, jax.numpy as jnp
from jax import lax
import numpy as np

D_MODEL = 2048
BATCH = 1
SEQ = 8192
DEPTH = 4

CHUNK = 64
N_MIXERS = 4
N_CONV_LAYERS = len(range(0, DEPTH, N_MIXERS))
N_POOL_LAYERS = len(range(1, DEPTH, N_MIXERS))
N_ATT_LAYERS = len(range(2, DEPTH, N_MIXERS))
N_SSM_LAYERS = len(range(3, DEPTH, N_MIXERS))
D_FF = 4 * D_MODEL
CONV_WIDTH = 3
POOL_WINDOWS = (2, 4, 8, 16)
N_POOL_GROUPS = len(POOL_WINDOWS)
POOL_GROUP = D_MODEL // N_POOL_GROUPS
ATT_HEAD_DIM = 128
ATT_HEADS = D_MODEL // ATT_HEAD_DIM
ATT_LEFT_CHUNKS = 8
ATT_PAD = ATT_LEFT_CHUNKS * CHUNK
ATT_BAND = ATT_PAD + CHUNK
REL_CLIP = 256
MASK_VALUE = -1e30
SSM_GROUP = 16
SSM_GROUPS = D_MODEL // SSM_GROUP
SSM_STATE = 64
SSM_BLOCK = 16
SSM_N_BLOCKS = SSM_GROUPS // SSM_BLOCK
DT_MIN = 1e-3
DT_MAX = 1e-1
RMS_EPS = 1e-6

kernel_name = 'interleaved_hybrid_streaming_encoder'


def rms_norm(x, gain):
    xf = x.astype(jnp.float32)
    y = xf * lax.rsqrt(jnp.mean(xf * xf, axis=-1, keepdims=True) + RMS_EPS)
    return (y * gain.astype(jnp.float32)).astype(x.dtype)


def squared_relu_mlp(h, w1, w2):
    a = jax.nn.relu(h @ w1)
    return (a * a) @ w2


def short_conv_mixer(h, w_in, conv_w, w_out):
    b_gate, c_gate, v = jnp.split(h @ w_in, 3, axis=-1)
    u = c_gate * v
    conv = lax.conv_general_dilated(
        u, conv_w.reshape(CONV_WIDTH, 1, D_MODEL).astype(u.dtype),
        window_strides=(1,), padding=[(CONV_WIDTH - 1, 0)],
        dimension_numbers=('NWC', 'WIO', 'NWC'), feature_group_count=D_MODEL)
    return (b_gate * conv) @ w_out


def pool_mixer(h, w_in, w_group, scale):
    b, s, _ = h.shape
    u = (h @ w_in).astype(jnp.float32).reshape(b, s, N_POOL_GROUPS, POOL_GROUP)
    csum = jnp.cumsum(u, axis=1)
    pos = jnp.arange(1, s + 1, dtype=jnp.float32)
    outs = []
    for gi, w in enumerate(POOL_WINDOWS):
        c = csum[:, :, gi]
        lagged = jnp.pad(c, ((0, 0), (w, 0), (0, 0)))[:, :s]
        count = jnp.minimum(pos, float(w))[None, :, None]
        outs.append((c - lagged) / count - u[:, :, gi])
    pooled = jnp.stack(outs, axis=2).astype(h.dtype)
    y = jnp.einsum('bsgc,gcd->bsgd', pooled, w_group)
    return y.reshape(b, s, D_MODEL) * scale


def chunk_attention_mixer(h, w_qkv, q_gain, k_gain, rel_bias, w_out):
    b, s, _ = h.shape
    nc = s // CHUNK
    qkv = (h @ w_qkv).reshape(b, s, 3, ATT_HEADS, ATT_HEAD_DIM)
    q = rms_norm(qkv[:, :, 0], q_gain)
    k = rms_norm(qkv[:, :, 1], k_gain)
    v = qkv[:, :, 2]
    k_pad = jnp.pad(k, ((0, 0), (ATT_PAD, 0), (0, 0), (0, 0)))
    v_pad = jnp.pad(v, ((0, 0), (ATT_PAD, 0), (0, 0), (0, 0)))
    q_idx = jnp.arange(CHUNK)[:, None] + ATT_PAD
    k_idx = jnp.arange(ATT_BAND)[None, :]
    rel = jnp.clip(q_idx - k_idx, -REL_CLIP, REL_CLIP) + REL_CLIP
    bias = rel_bias[:, rel].astype(jnp.float32)
    q_chunks = q.reshape(b, nc, CHUNK, ATT_HEADS, ATT_HEAD_DIM).transpose(1, 0, 2, 3, 4)
    scale = ATT_HEAD_DIM ** -0.5

    def one_chunk(args):
        c, q_c = args
        start = c * CHUNK
        k_band = lax.dynamic_slice_in_dim(k_pad, start, ATT_BAND, axis=1)
        v_band = lax.dynamic_slice_in_dim(v_pad, start, ATT_BAND, axis=1)
        scores = jnp.einsum('bqhd,bkhd->bhqk', q_c, k_band).astype(jnp.float32) * scale + bias
        key_pos = start - ATT_PAD + jnp.arange(ATT_BAND)
        scores = jnp.where((key_pos >= 0)[None, None, None, :], scores, MASK_VALUE)
        probs = jax.nn.softmax(scores, axis=-1).astype(v_band.dtype)
        return jnp.einsum('bhqk,bkhd->bqhd', probs, v_band)

    out = lax.map(one_chunk, (jnp.arange(nc), q_chunks))
    out = out.transpose(1, 0, 2, 3, 4).reshape(b, s, D_MODEL)
    return out @ w_out


def _ssm_combine(left, right):
    a1, b1 = left
    a2, b2 = right
    return a1 * a2, a2 * b1 + b2


def s5_mixer(h, a_re, a_im, log_dt, b_re, b_im, c_re, c_im, d_skip, w_glu):
    b, s, _ = h.shape
    f32 = jnp.float32
    u_flat = h.astype(f32)
    lam = lax.complex(a_re.astype(f32), a_im.astype(f32))
    dt = jnp.exp(log_dt.astype(f32))[:, None]
    a_bar = jnp.exp(lam * dt)
    b_mat = lax.complex(b_re.astype(f32), b_im.astype(f32))
    b_bar = ((a_bar - 1.0) / lam)[..., None] * b_mat
    c_mat = lax.complex(c_re.astype(f32), c_im.astype(f32))

    def to_blocks(t):
        return t.reshape(SSM_N_BLOCKS, SSM_BLOCK, *t.shape[1:])

    u_blk = u_flat.reshape(b, s, SSM_N_BLOCKS, SSM_BLOCK, SSM_GROUP).transpose(2, 0, 1, 3, 4)

    def scan_block(args):
        u_b, a_b, bb_b, c_b = args
        bu = jnp.einsum('bsgc,gnc->bsgn', u_b.astype(jnp.complex64), bb_b)
        a_t = jnp.broadcast_to(a_b, bu.shape)
        _, states = lax.associative_scan(_ssm_combine, (a_t, bu), axis=1)
        return jnp.real(jnp.einsum('bsgn,gcn->bsgc', states, c_b))

    y = lax.map(scan_block, (u_blk, to_blocks(a_bar), to_blocks(b_bar), to_blocks(c_mat)))
    y = y.transpose(1, 2, 0, 3, 4).reshape(b, s, D_MODEL) + d_skip.astype(f32) * u_flat
    z = jax.nn.gelu(y).astype(h.dtype)
    val, gate = jnp.split(z @ w_glu, 2, axis=-1)
    return val * jax.nn.sigmoid(gate)


def setup_inputs(seed: int = 0) -> dict:
    key = jax.random.key(seed)
    ks = jax.random.split(key, 32)
    f32 = jnp.float32

    def nrm(k, shape, scale):
        return jax.random.normal(k, shape, f32) * scale

    nA, nB, nC, nD = N_CONV_LAYERS, N_POOL_LAYERS, N_ATT_LAYERS, N_SSM_LAYERS
    G, N = SSM_GROUPS, SSM_STATE
    inv_d = D_MODEL ** -0.5
    return {
        'x': nrm(ks[0], (BATCH, SEQ, D_MODEL), 1.0),
        'norm_mix': 1.0 + nrm(ks[1], (DEPTH, D_MODEL), 0.02),
        'norm_mlp': 1.0 + nrm(ks[2], (DEPTH, D_MODEL), 0.02),
        'mlp_w1': nrm(ks[3], (DEPTH, D_MODEL, D_FF), inv_d),
        'mlp_w2': nrm(ks[4], (DEPTH, D_FF, D_MODEL), D_FF ** -0.5),
        'conv_w_in': nrm(ks[5], (nA, D_MODEL, 3 * D_MODEL), inv_d),
        'conv_w': nrm(ks[6], (nA, CONV_WIDTH, D_MODEL), CONV_WIDTH ** -0.5),
        'conv_w_out': nrm(ks[7], (nA, D_MODEL, D_MODEL), inv_d),
        'pool_w_in': nrm(ks[8], (nB, D_MODEL, D_MODEL), inv_d),
        'pool_w_group': nrm(ks[9], (nB, N_POOL_GROUPS, POOL_GROUP, POOL_GROUP), POOL_GROUP ** -0.5),
        'pool_scale': 1.0 + nrm(ks[10], (nB, D_MODEL), 0.1),
        'att_w_qkv': nrm(ks[11], (nC, D_MODEL, 3 * D_MODEL), inv_d),
        'att_q_norm': 1.0 + nrm(ks[12], (nC, ATT_HEAD_DIM), 0.02),
        'att_k_norm': 1.0 + nrm(ks[13], (nC, ATT_HEAD_DIM), 0.02),
        'att_rel_bias': nrm(ks[14], (nC, ATT_HEADS, 2 * REL_CLIP + 1), 0.5),
        'att_w_out': nrm(ks[15], (nC, D_MODEL, D_MODEL), inv_d),
        'ssm_a_re': -0.5 + nrm(ks[16], (nD, G, N), 0.01),
        'ssm_a_im': math.pi * jnp.arange(N, dtype=f32) + nrm(ks[17], (nD, G, N), 0.01),
        'ssm_log_dt': jax.random.uniform(ks[18], (nD, G), f32, math.log(DT_MIN), math.log(DT_MAX)),
        'ssm_b_re': nrm(ks[19], (nD, G, N, SSM_GROUP), (2 * SSM_GROUP) ** -0.5),
        'ssm_b_im': nrm(ks[20], (nD, G, N, SSM_GROUP), (2 * SSM_GROUP) ** -0.5),
        'ssm_c_re': nrm(ks[21], (nD, G, SSM_GROUP, N), (2 * N) ** -0.5 * 4.0),
        'ssm_c_im': nrm(ks[22], (nD, G, SSM_GROUP, N), (2 * N) ** -0.5 * 4.0),
        'ssm_d': nrm(ks[23], (nD, D_MODEL), 1.0),
        'ssm_w_glu': nrm(ks[24], (nD, D_MODEL, 2 * D_MODEL), inv_d),
    }


def reference(x, norm_mix, norm_mlp, mlp_w1, mlp_w2, conv_w_in, conv_w, conv_w_out,
              pool_w_in, pool_w_group, pool_scale, att_w_qkv, att_q_norm, att_k_norm,
              att_rel_bias, att_w_out, ssm_a_re, ssm_a_im, ssm_log_dt, ssm_b_re, ssm_b_im,
              ssm_c_re, ssm_c_im, ssm_d, ssm_w_glu):
    for i in range(DEPTH):
        kind = i % N_MIXERS
        j = i // N_MIXERS
        h = rms_norm(x, norm_mix[i])
        if kind == 0:
            m = short_conv_mixer(h, conv_w_in[j], conv_w[j], conv_w_out[j])
        elif kind == 1:
            m = pool_mixer(h, pool_w_in[j], pool_w_group[j], pool_scale[j])
        elif kind == 2:
            m = chunk_attention_mixer(h, att_w_qkv[j], att_q_norm[j], att_k_norm[j],
                                      att_rel_bias[j], att_w_out[j])
        else:
            m = s5_mixer(h, ssm_a_re[j], ssm_a_im[j], ssm_log_dt[j], ssm_b_re[j], ssm_b_im[j],
                         ssm_c_re[j], ssm_c_im[j], ssm_d[j], ssm_w_glu[j])
        x = x + m.astype(x.dtype)
        h = rms_norm(x, norm_mlp[i])
        x = x + squared_relu_mlp(h, mlp_w1[i], mlp_w2[i]).astype(x.dtype)
    return x
```

```python
import functools
import math

import jax
import jax.numpy as jnp
from jax import lax
from jax.experimental import pallas as pl
from jax.experimental.pallas import tpu as pltpu

F32 = jnp.float32
BF16 = jnp.bfloat16

RMS_EPS = 1e-6
CHUNK = 64
ATT_HEAD_DIM = 128
ATT_LEFT_CHUNKS = 8
REL_CLIP = 256
MASK_VALUE = -1e30
POOL_WINDOWS = (2, 4, 8, 16)
CONV_WIDTH = 3
SSM_GROUP = 16
SSM_STATE = 64

SUBLANES = 8
LANES = 128
VMEM_LIMIT_BYTES = 56 * 1024 * 1024

ROW_TILE = 512
COL_TILE = 512
FF_TILE = 512
ATT_QBLOCK = 256
SSM_ROW_TILE = 256
SSM_GROUPS_PER_BLOCK = 8
HALO = 16


def _params(*semantics):
    return pltpu.CompilerParams(dimension_semantics=semantics, vmem_limit_bytes=VMEM_LIMIT_BYTES)


def _rms_norm(x, gain):
    return x * lax.rsqrt(jnp.mean(x * x, axis=-1, keepdims=True) + RMS_EPS) * gain


def _dot(a, b):
    return jnp.dot(a, b, preferred_element_type=F32)


def _mlp_kernel(x_ref, g_ref, w1_ref, w2_ref, o_ref, h_ref):
    f = pl.program_id(1)

    @pl.when(f == 0)
    def _():
        x = x_ref[...]
        h_ref[...] = _rms_norm(x, g_ref[...]).astype(BF16)
        o_ref[...] = x

    a = jnp.maximum(_dot(h_ref[...], w1_ref[...]), 0.0)
    o_ref[...] += _dot((a * a).astype(BF16), w2_ref[...])


def _mlp(x, gain, w1, w2):
    s, d = x.shape
    dff = w1.shape[1]
    tm, tf = min(ROW_TILE, s), FF_TILE
    return pl.pallas_call(
        _mlp_kernel,
        grid=(s // tm, dff // tf),
        in_specs=[
            pl.BlockSpec((tm, d), lambda i, f: (i, 0)),
            pl.BlockSpec((1, d), lambda i, f: (0, 0)),
            pl.BlockSpec((d, tf), lambda i, f: (0, f)),
            pl.BlockSpec((tf, d), lambda i, f: (f, 0)),
        ],
        out_specs=pl.BlockSpec((tm, d), lambda i, f: (i, 0)),
        out_shape=jax.ShapeDtypeStruct((s, d), F32),
        scratch_shapes=[pltpu.VMEM((tm, d), BF16)],
        compiler_params=_params("parallel", "arbitrary"),
        name="mlp",
    )(x, gain, w1, w2)


def _proj_res_kernel(a_ref, w_ref, x_ref, o_ref):
    o_ref[...] = x_ref[...] + _dot(a_ref[...], w_ref[...])


def _proj_res(a, w, x):
    s, k = a.shape
    n = w.shape[1]
    tm, tn = min(ROW_TILE, s), COL_TILE
    return pl.pallas_call(
        _proj_res_kernel,
        grid=(s // tm, n // tn),
        in_specs=[
            pl.BlockSpec((tm, k), lambda i, j: (i, 0)),
            pl.BlockSpec((k, tn), lambda i, j: (0, j)),
            pl.BlockSpec((tm, tn), lambda i, j: (i, j)),
        ],
        out_specs=pl.BlockSpec((tm, tn), lambda i, j: (i, j)),
        out_shape=jax.ShapeDtypeStruct((s, n), F32),
        compiler_params=_params("parallel", "arbitrary"),
        name="proj_res",
    )(a, w, x)


def _glu_res_kernel(a_ref, wv_ref, wg_ref, x_ref, o_ref):
    a = a_ref[...]
    val = _dot(a, wv_ref[...])
    gate = _dot(a, wg_ref[...])
    o_ref[...] = x_ref[...] + val * jax.nn.sigmoid(gate)


def _glu_res(a, w_glu, x):
    s, k = a.shape
    n = w_glu.shape[1] // 2
    tm, tn = min(ROW_TILE, s), COL_TILE
    nj = n // tn
    return pl.pallas_call(
        _glu_res_kernel,
        grid=(s // tm, nj),
        in_specs=[
            pl.BlockSpec((tm, k), lambda i, j: (i, 0)),
            pl.BlockSpec((k, tn), lambda i, j: (0, j)),
            pl.BlockSpec((k, tn), lambda i, j: (0, j + nj)),
            pl.BlockSpec((tm, tn), lambda i, j: (i, j)),
        ],
        out_specs=pl.BlockSpec((tm, tn), lambda i, j: (i, j)),
        out_shape=jax.ShapeDtypeStruct((s, n), F32),
        compiler_params=_params("parallel", "arbitrary"),
        name="glu_res",
    )(a, w_glu, w_glu, x)


def _conv_front_kernel(x_ref, g_ref, wb_ref, wc_ref, wv_ref, cw_ref, o_ref, h_ref, ext_ref, carry_ref):
    i, j = pl.program_id(0), pl.program_id(1)
    tm = o_ref.shape[0]

    @pl.when(j == 0)
    def _():
        h_ref[...] = _rms_norm(x_ref[...], g_ref[...]).astype(BF16)

    @pl.when(i == 0)
    def _():
        carry_ref[j] = jnp.zeros(carry_ref.shape[1:], F32)

    h = h_ref[...]
    u = _dot(h, wc_ref[...]) * _dot(h, wv_ref[...])
    ext_ref[0:HALO, :] = carry_ref[j]
    ext_ref[HALO:, :] = u
    carry_ref[j] = u[tm - HALO:, :]
    conv = cw_ref[CONV_WIDTH - 1:CONV_WIDTH, :] * u
    for lag in range(1, CONV_WIDTH):
        tap = CONV_WIDTH - 1 - lag
        conv += cw_ref[tap:tap + 1, :] * ext_ref[pl.ds(HALO - lag, tm), :]
    o_ref[...] = (_dot(h, wb_ref[...]) * conv).astype(BF16)


def _conv_front(x, gain, w_in, conv_w):
    s, d = x.shape
    tm, tn = min(ROW_TILE, s), COL_TILE
    nj = d // tn
    return pl.pallas_call(
        _conv_front_kernel,
        grid=(s // tm, nj),
        in_specs=[
            pl.BlockSpec((tm, d), lambda i, j: (i, 0)),
            pl.BlockSpec((1, d), lambda i, j: (0, 0)),
            pl.BlockSpec((d, tn), lambda i, j: (0, j)),
            pl.BlockSpec((d, tn), lambda i, j: (0, j + nj)),
            pl.BlockSpec((d, tn), lambda i, j: (0, j + 2 * nj)),
            pl.BlockSpec((CONV_WIDTH, tn), lambda i, j: (0, j)),
        ],
        out_specs=pl.BlockSpec((tm, tn), lambda i, j: (i, j)),
        out_shape=jax.ShapeDtypeStruct((s, d), BF16),
        scratch_shapes=[
            pltpu.VMEM((tm, d), BF16),
            pltpu.VMEM((tm + HALO, tn), F32),
            pltpu.VMEM((nj, HALO, tn), F32),
        ],
        compiler_params=_params("arbitrary", "arbitrary"),
        name="conv_front",
    )(x, gain, w_in, w_in, w_in, conv_w)


def _pool_kernel(x_ref, g_ref, win_ref, wg_ref, sc_ref, xr_ref, o_ref, h_ref, ext_ref, carry_ref):
    i, grp = pl.program_id(0), pl.program_id(1)
    tm = o_ref.shape[0]

    @pl.when(grp == 0)
    def _():
        h_ref[...] = _rms_norm(x_ref[...], g_ref[...]).astype(BF16)

    @pl.when(i == 0)
    def _():
        carry_ref[grp] = jnp.zeros(carry_ref.shape[1:], F32)

    u = _dot(h_ref[...], win_ref[...])
    ext_ref[0:HALO, :] = carry_ref[grp]
    ext_ref[HALO:, :] = u
    carry_ref[grp] = u[tm - HALO:, :]
    pos = (i * tm + 1 + lax.broadcasted_iota(jnp.int32, (tm, 1), 0)).astype(F32)

    for gi, w in enumerate(POOL_WINDOWS):
        @pl.when(grp == gi)
        def _(w=w):
            acc = u
            for lag in range(1, w):
                acc += ext_ref[pl.ds(HALO - lag, tm), :]
            inv_count = 1.0 / jnp.minimum(pos, float(w))
            pooled = (acc * inv_count - u).astype(BF16)
            o_ref[...] = xr_ref[...] + _dot(pooled, wg_ref[0]) * sc_ref[...]


def _pool_mixer(x, gain, w_in, w_group, scale):
    s, d = x.shape
    ng, pg = w_group.shape[0], w_group.shape[1]
    tm = min(ROW_TILE, s)
    return pl.pallas_call(
        _pool_kernel,
        grid=(s // tm, ng),
        in_specs=[
            pl.BlockSpec((tm, d), lambda i, g: (i, 0)),
            pl.BlockSpec((1, d), lambda i, g: (0, 0)),
            pl.BlockSpec((d, pg), lambda i, g: (0, g)),
            pl.BlockSpec((1, pg, pg), lambda i, g: (g, 0, 0)),
            pl.BlockSpec((1, pg), lambda i, g: (0, g)),
            pl.BlockSpec((tm, pg), lambda i, g: (i, g)),
        ],
        out_specs=pl.BlockSpec((tm, pg), lambda i, g: (i, g)),
        out_shape=jax.ShapeDtypeStruct((s, d), F32),
        scratch_shapes=[
            pltpu.VMEM((tm, d), BF16),
            pltpu.VMEM((tm + HALO, pg), F32),
            pltpu.VMEM((ng, HALO, pg), F32),
        ],
        compiler_params=_params("arbitrary", "arbitrary"),
        name="pool_mixer",
    )(x, gain, w_in, w_group, scale, x)


def _qkv_kernel(x_ref, g_ref, w_ref, hg_ref, o_ref, h_ref, *, n_norm_tiles):
    j = pl.program_id(1)

    @pl.when(j == 0)
    def _():
        h_ref[...] = _rms_norm(x_ref[...], g_ref[...]).astype(BF16)

    y = _dot(h_ref[...], w_ref[...])

    @pl.when(j < n_norm_tiles)
    def _():
        for hd in range(y.shape[1] // ATT_HEAD_DIM):
            sl = slice(hd * ATT_HEAD_DIM, (hd + 1) * ATT_HEAD_DIM)
            o_ref[:, sl] = _rms_norm(y[:, sl], hg_ref[:, sl]).astype(BF16)

    @pl.when(j >= n_norm_tiles)
    def _():
        o_ref[...] = y.astype(BF16)


def _qkv_proj(x, gain, w_qkv, head_gains):
    s, d = x.shape
    n = w_qkv.shape[1]
    tm, tn = min(ROW_TILE, s), COL_TILE
    n_norm_tiles = head_gains.shape[1] // tn
    return pl.pallas_call(
        functools.partial(_qkv_kernel, n_norm_tiles=n_norm_tiles),
        grid=(s // tm, n // tn),
        in_specs=[
            pl.BlockSpec((tm, d), lambda i, j: (i, 0)),
            pl.BlockSpec((1, d), lambda i, j: (0, 0)),
            pl.BlockSpec((d, tn), lambda i, j: (0, j)),
            pl.BlockSpec((1, tn), lambda i, j: (0, jnp.minimum(j, n_norm_tiles - 1))),
        ],
        out_specs=pl.BlockSpec((tm, tn), lambda i, j: (i, j)),
        out_shape=jax.ShapeDtypeStruct((s, n), BF16),
        scratch_shapes=[pltpu.VMEM((tm, d), BF16)],
        compiler_params=_params("parallel", "arbitrary"),
        name="qkv_proj",
    )(x, gain, w_qkv, head_gains)


def _attn_kernel(q_ref, k0_ref, k1_ref, k2_ref, v0_ref, v1_ref, v2_ref, bias_ref, o_ref):
    b = pl.program_id(1)
    qb = q_ref.shape[0]
    q = q_ref[...]
    scale = ATT_HEAD_DIM ** -0.5
    scores = []
    for c, k_ref in enumerate((k0_ref, k1_ref, k2_ref)):
        s_c = lax.dot_general(q, k_ref[...], (((1,), (1,)), ((), ())), preferred_element_type=F32)
        s_c = s_c * scale + bias_ref[0, :, c * qb:(c + 1) * qb]
        if c < 2:
            s_c = jnp.where(b - 2 + c >= 0, s_c, MASK_VALUE)
        scores.append(s_c)
    m = jnp.maximum(jnp.maximum(jnp.max(scores[0], axis=-1, keepdims=True),
                                jnp.max(scores[1], axis=-1, keepdims=True)),
                    jnp.max(scores[2], axis=-1, keepdims=True))
    acc = jnp.zeros((qb, ATT_HEAD_DIM), F32)
    denom = jnp.zeros((qb, 1), F32)
    for s_c, v_ref in zip(scores, (v0_ref, v1_ref, v2_ref)):
        p = jnp.exp(s_c - m)
        denom += jnp.sum(p, axis=-1, keepdims=True)
        acc += _dot(p.astype(BF16), v_ref[...])
    o_ref[...] = (acc / denom).astype(BF16)


def _attention(qkv, bias, n_heads):
    s = qkv.shape[0]
    qb = ATT_QBLOCK
    nb = s // qb
    hd = ATT_HEAD_DIM

    def kv_spec(offset, back):
        return pl.BlockSpec((qb, hd), lambda h, b: (jnp.maximum(b - back, 0), offset + h))

    return pl.pallas_call(
        _attn_kernel,
        grid=(n_heads, nb),
        in_specs=[
            pl.BlockSpec((qb, hd), lambda h, b: (b, h)),
            kv_spec(n_heads, 2), kv_spec(n_heads, 1), kv_spec(n_heads, 0),
            kv_spec(2 * n_heads, 2), kv_spec(2 * n_heads, 1), kv_spec(2 * n_heads, 0),
            pl.BlockSpec((1, qb, 3 * qb), lambda h, b: (h, 0, 0)),
        ],
        out_specs=pl.BlockSpec((qb, hd), lambda h, b: (b, h)),
        out_shape=jax.ShapeDtypeStruct((s, n_heads * hd), BF16),
        compiler_params=_params("parallel", "arbitrary"),
        name="chunk_attention",
    )(qkv, qkv, qkv, qkv, qkv, qkv, qkv, bias)


def _attention_bias(rel_bias):
    qb = ATT_QBLOCK
    q_idx = jnp.arange(qb)[:, None] + 2 * qb
    k_idx = jnp.arange(3 * qb)[None, :]
    rel = jnp.clip(q_idx - k_idx, -REL_CLIP, REL_CLIP) + REL_CLIP
    chunk_start = (q_idx // CHUNK) * CHUNK
    in_band = (k_idx >= chunk_start - ATT_LEFT_CHUNKS * CHUNK) & (k_idx < chunk_start + CHUNK)
    return jnp.where(in_band[None], rel_bias[:, rel].astype(F32), MASK_VALUE)


def _ssm_prep_kernel(are_ref, aim_ref, ldt_ref, btr_ref, bti_ref,
                     abr_ref, abi_ref, apr_ref, api_ref, bbr_ref, bbi_ref, *, n_squarings):
    lam_r, lam_i = are_ref[...], aim_ref[...]
    dt = jnp.exp(ldt_ref[...])
    mag = jnp.exp(lam_r * dt)
    ab_r, ab_i = mag * jnp.cos(lam_i * dt), mag * jnp.sin(lam_i * dt)
    abr_ref[...] = ab_r
    abi_ref[...] = ab_i
    p_r, p_i = ab_r, ab_i
    for _ in range(n_squarings):
        p_r, p_i = p_r * p_r - p_i * p_i, 2.0 * p_r * p_i
    apr_ref[...] = p_r
    api_ref[...] = p_i
    num_r, num_i = ab_r - 1.0, ab_i
    den = lam_r * lam_r + lam_i * lam_i
    co_r = (num_r * lam_r + num_i * lam_i) / den
    co_i = (num_i * lam_r - num_r * lam_i) / den
    for c in range(btr_ref.shape[0]):
        b_r, b_i = btr_ref[c], bti_ref[c]
        bbr_ref[c] = co_r * b_r - co_i * b_i
        bbi_ref[c] = co_r * b_i + co_i * b_r


def _ssm_prep(a_re, a_im, log_dt, bt_re, bt_im, n_squarings):
    g, n = a_re.shape
    gn = jax.ShapeDtypeStruct((g, n), F32)
    cgn = jax.ShapeDtypeStruct(bt_re.shape, F32)
    return pl.pallas_call(
        functools.partial(_ssm_prep_kernel, n_squarings=n_squarings),
        out_shape=(gn, gn, gn, gn, cgn, cgn),
        name="ssm_prep",
    )(a_re, a_im, log_dt, bt_re, bt_im)


def _ssm_kernel(x_ref, g_ref, d_ref, wbr_ref, wbi_ref, wcr_ref, wci_ref,
                ar_ref, ai_ref, pr_ref, pi_ref, o_ref,
                t3_ref, hp_ref, y_ref, sr_ref, si_ref, cr_ref, ci_ref):
    i, gb = pl.program_id(0), pl.program_id(1)
    t = x_ref.shape[0]
    nlb = t3_ref.shape[0]
    seg = t // SUBLANES
    cb = wbr_ref.shape[1]
    ns = wbr_ref.shape[2]

    @pl.when(gb == 0)
    def _():
        h = _rms_norm(x_ref[...], g_ref[...])
        for c in range(nlb):
            lanes = slice(c * LANES, (c + 1) * LANES)
            t3_ref[c] = h[:, lanes]
            for j in range(seg):
                hp_ref[j * SUBLANES:(j + 1) * SUBLANES, lanes] = (
                    t3_ref[c, pl.ds(j, SUBLANES, stride=seg), :])

    @pl.when(i == 0)
    def _():
        cr_ref[gb] = jnp.zeros(cr_ref.shape[1:], F32)
        ci_ref[gb] = jnp.zeros(ci_ref.shape[1:], F32)

    col = pl.multiple_of(gb * cb, LANES)
    hb = hp_ref[:, pl.ds(col, cb)].astype(BF16)
    sr_ref[...] = _dot(hb, wbr_ref[0])
    si_ref[...] = _dot(hb, wbi_ref[0])

    a_r = jnp.broadcast_to(ar_ref[0], (SUBLANES, ns))
    a_i = jnp.broadcast_to(ai_ref[0], (SUBLANES, ns))

    def step(j, carry, store):
        x_r, x_i = carry
        rows = pl.ds(pl.multiple_of(j * SUBLANES, SUBLANES), SUBLANES)
        n_r = a_r * x_r - a_i * x_i + sr_ref[rows, :]
        n_i = a_r * x_i + a_i * x_r + si_ref[rows, :]
        if store:
            sr_ref[rows, :] = n_r
            si_ref[rows, :] = n_i
        return n_r, n_i

    zeros = jnp.zeros((SUBLANES, ns), F32)
    f_r, f_i = lax.fori_loop(0, seg, functools.partial(step, store=False), (zeros, zeros), unroll=4)
    p_r, p_i = pr_ref[0], pi_ref[0]
    row_r, row_i = cr_ref[gb], ci_ref[gb]
    sub = lax.broadcasted_iota(jnp.int32, (SUBLANES, ns), 0)
    init_r, init_i = zeros, zeros
    for s_idx in range(SUBLANES):
        init_r = jnp.where(sub == s_idx, jnp.broadcast_to(row_r, (SUBLANES, ns)), init_r)
        init_i = jnp.where(sub == s_idx, jnp.broadcast_to(row_i, (SUBLANES, ns)), init_i)
        if s_idx + 1 < SUBLANES:
            row_r, row_i = (p_r * row_r - p_i * row_i + f_r[s_idx:s_idx + 1, :],
                            p_r * row_i + p_i * row_r + f_i[s_idx:s_idx + 1, :])
    l_r, l_i = lax.fori_loop(0, seg, functools.partial(step, store=True), (init_r, init_i), unroll=4)
    cr_ref[gb] = l_r[SUBLANES - 1:SUBLANES, :]
    ci_ref[gb] = l_i[SUBLANES - 1:SUBLANES, :]

    y_ref[:, pl.ds(col, cb)] = (_dot(sr_ref[...].astype(BF16), wcr_ref[0])
                                + _dot(si_ref[...].astype(BF16), wci_ref[0]))

    @pl.when(gb == pl.num_programs(1) - 1)
    def _():
        z = jax.nn.gelu(y_ref[...] + d_ref[...] * hp_ref[...])
        for c in range(nlb):
            lanes = slice(c * LANES, (c + 1) * LANES)
            t3_ref[c] = z[:, lanes]
            for s_idx in range(SUBLANES):
                o_ref[s_idx * seg:(s_idx + 1) * seg, lanes] = (
                    t3_ref[c, pl.ds(s_idx, seg, stride=SUBLANES), :].astype(BF16))


def _ssm_core(x, gain, d_skip, wb_r, wb_i, wc_r, wc_i, ab_r, ab_i, ap_r, ap_i):
    s, d = x.shape
    t = min(SSM_ROW_TILE, s)
    nblk, cb, ns = wb_r.shape
    wspec_b = pl.BlockSpec((1, cb, ns), lambda i, g: (g, 0, 0))
    wspec_c = pl.BlockSpec((1, ns, cb), lambda i, g: (g, 0, 0))
    aspec = pl.BlockSpec((1, 1, ns), lambda i, g: (g, 0, 0))
    return pl.pallas_call(
        _ssm_kernel,
        grid=(s // t, nblk),
        in_specs=[
            pl.BlockSpec((t, d), lambda i, g: (i, 0)),
            pl.BlockSpec((1, d), lambda i, g: (0, 0)),
            pl.BlockSpec((1, d), lambda i, g: (0, 0)),
            wspec_b, wspec_b, wspec_c, wspec_c,
            aspec, aspec, aspec, aspec,
        ],
        out_specs=pl.BlockSpec((t, d), lambda i, g: (i, 0)),
        out_shape=jax.ShapeDtypeStruct((s, d), BF16),
        scratch_shapes=[
            pltpu.VMEM((d // LANES, t, LANES), F32),
            pltpu.VMEM((t, d), F32),
            pltpu.VMEM((t, d), F32),
            pltpu.VMEM((t, ns), F32),
            pltpu.VMEM((t, ns), F32),
            pltpu.VMEM((nblk, 1, ns), F32),
            pltpu.VMEM((nblk, 1, ns), F32),
        ],
        compiler_params=_params("arbitrary", "arbitrary"),
        name="ssm_core",
    )(x, gain, d_skip, wb_r, wb_i, wc_r, wc_i, ab_r, ab_i, ap_r, ap_i)


def _block_diag(w, groups_per_block):
    g, r, c = w.shape
    nblk = g // groups_per_block
    w = w.reshape(nblk, groups_per_block, r, 1, c)
    eye = jnp.eye(groups_per_block, dtype=w.dtype).reshape(1, groups_per_block, 1, groups_per_block, 1)
    return (w * eye).reshape(nblk, groups_per_block * r, groups_per_block * c)


def _s5_mixer(x, gain, a_re, a_im, log_dt, b_re, b_im, c_re, c_im, d_skip, w_glu):
    s, d = x.shape
    g, n = a_re.shape
    gpb = SSM_GROUPS_PER_BLOCK
    seg = min(SSM_ROW_TILE, s) // SUBLANES
    n_squarings = int(math.log2(seg))
    assert 2 ** n_squarings == seg
    bt_re, bt_im = b_re.transpose(2, 0, 1), b_im.transpose(2, 0, 1)
    ab_r, ab_i, ap_r, ap_i, bb_r, bb_i = _ssm_prep(a_re, a_im, log_dt.reshape(g, 1), bt_re, bt_im,
                                                   n_squarings)
    wb_r = _block_diag(bb_r.transpose(1, 0, 2), gpb).astype(BF16)
    wb_i = _block_diag(bb_i.transpose(1, 0, 2), gpb).astype(BF16)
    wc_r = _block_diag(c_re.transpose(0, 2, 1), gpb).astype(BF16)
    wc_i = _block_diag(-c_im.transpose(0, 2, 1), gpb).astype(BF16)
    flat = lambda a: a.reshape(g // gpb, 1, gpb * n)
    z = _ssm_core(x, gain, d_skip.reshape(1, d), wb_r, wb_i, wc_r, wc_i,
                  flat(ab_r), flat(ab_i), flat(ap_r), flat(ap_i))
    return _glu_res(z, w_glu.astype(BF16), x)


def kernel(x, norm_mix, norm_mlp, mlp_w1, mlp_w2, conv_w_in, conv_w, conv_w_out, pool_w_in, pool_w_group, pool_scale, att_w_qkv, att_q_norm, att_k_norm, att_rel_bias, att_w_out, ssm_a_re, ssm_a_im, ssm_log_dt, ssm_b_re, ssm_b_im, ssm_c_re, ssm_c_im, ssm_d, ssm_w_glu):
    b, s, d = x.shape
    depth = norm_mix.shape[0]
    n_mixers = 4
    outs = []
    for bi in range(b):
        xs = x[bi]
        for i in range(depth):
            kind, j = i % n_mixers, i // n_mixers
            gain = norm_mix[i].reshape(1, d)
            if kind == 0:
                gated = _conv_front(xs, gain, conv_w_in[j].astype(BF16), conv_w[j])
                xs = _proj_res(gated, conv_w_out[j].astype(BF16), xs)
            elif kind == 1:
                xs = _pool_mixer(xs, gain, pool_w_in[j].astype(BF16), pool_w_group[j].astype(BF16),
                                 pool_scale[j].reshape(1, d))
            elif kind == 2:
                n_heads = d // ATT_HEAD_DIM
                head_gains = jnp.concatenate([jnp.tile(att_q_norm[j], n_heads),
                                              jnp.tile(att_k_norm[j], n_heads)]).reshape(1, 2 * d)
                qkv = _qkv_proj(xs, gain, att_w_qkv[j].astype(BF16), head_gains)
                att = _attention(qkv, _attention_bias(att_rel_bias[j]), n_heads)
                xs = _proj_res(att, att_w_out[j].astype(BF16), xs)
            else:
                xs = _s5_mixer(xs, gain, ssm_a_re[j], ssm_a_im[j], ssm_log_dt[j], ssm_b_re[j],
                               ssm_b_im[j], ssm_c_re[j], ssm_c_im[j], ssm_d[j], ssm_w_glu[j])
            xs = _mlp(xs, norm_mlp[i].reshape(1, d), mlp_w1[i].astype(BF16), mlp_w2[i].astype(BF16))
        outs.append(xs)
    return jnp.stack(outs)
```

```python
import functools
import math

import jax
import jax.numpy as jnp
from jax import lax
from jax.experimental import pallas as pl
from jax.experimental.pallas import tpu as pltpu

F32 = jnp.float32
BF16 = jnp.bfloat16

RMS_EPS = 1e-6
CHUNK = 64
ATT_HEAD_DIM = 128
ATT_LEFT_CHUNKS = 8
REL_CLIP = 256
MASK_VALUE = -1e30
POOL_WINDOWS = (2, 4, 8, 16)
CONV_WIDTH = 3
SSM_GROUP = 16
SSM_STATE = 64

SUBLANES = 8
LANES = 128
VMEM_LIMIT_BYTES = 56 * 1024 * 1024

ROW_TILE = 1024
COL_TILE = 512
FF_TILE = 256
ATT_QBLOCK = 256
ATT_HEADS_PER_STEP = 4
SSM_ROW_TILE = 512
SSM_GROUPS_PER_BLOCK = 8
HALO = 16


def _params(*semantics):
    return pltpu.CompilerParams(dimension_semantics=semantics, vmem_limit_bytes=VMEM_LIMIT_BYTES)


def _rms_norm(x, gain):
    return x * lax.rsqrt(jnp.mean(x * x, axis=-1, keepdims=True) + RMS_EPS) * gain


def _dot(a, b):
    return jnp.dot(a, b, preferred_element_type=F32)


def _wdot(a, w_ref):
    return _dot(a, w_ref[...].astype(BF16))


def _mlp_kernel(x_ref, g_ref, w1_ref, w2_ref, o_ref, h_ref):
    f = pl.program_id(1)

    @pl.when(f == 0)
    def _():
        x = x_ref[...]
        h_ref[...] = _rms_norm(x, g_ref[...]).astype(BF16)
        o_ref[...] = x

    a = jnp.maximum(_wdot(h_ref[...], w1_ref), 0.0)
    o_ref[...] += _wdot((a * a).astype(BF16), w2_ref)


def _mlp(x, gain, w1, w2, layer):
    s, d = x.shape
    dff = w1.shape[2]
    tm, tf = min(ROW_TILE, s), FF_TILE
    return pl.pallas_call(
        _mlp_kernel,
        grid=(s // tm, dff // tf),
        in_specs=[
            pl.BlockSpec((tm, d), lambda i, f: (i, 0)),
            pl.BlockSpec((1, d), lambda i, f: (0, 0)),
            pl.BlockSpec((None, d, tf), lambda i, f: (layer, 0, f)),
            pl.BlockSpec((None, tf, d), lambda i, f: (layer, f, 0)),
        ],
        out_specs=pl.BlockSpec((tm, d), lambda i, f: (i, 0)),
        out_shape=jax.ShapeDtypeStruct((s, d), F32),
        scratch_shapes=[pltpu.VMEM((tm, d), BF16)],
        compiler_params=_params("parallel", "arbitrary"),
        name="mlp",
    )(x, gain, w1, w2)


def _proj_res_kernel(a_ref, w_ref, x_ref, o_ref):
    o_ref[...] = x_ref[...] + _wdot(a_ref[...], w_ref)


def _proj_res(a, w, x, layer):
    s, k = a.shape
    n = w.shape[2]
    tm, tn = min(ROW_TILE, s), COL_TILE
    return pl.pallas_call(
        _proj_res_kernel,
        grid=(s // tm, n // tn),
        in_specs=[
            pl.BlockSpec((tm, k), lambda i, j: (i, 0)),
            pl.BlockSpec((None, k, tn), lambda i, j: (layer, 0, j)),
            pl.BlockSpec((tm, tn), lambda i, j: (i, j)),
        ],
        out_specs=pl.BlockSpec((tm, tn), lambda i, j: (i, j)),
        out_shape=jax.ShapeDtypeStruct((s, n), F32),
        compiler_params=_params("parallel", "arbitrary"),
        name="proj_res",
    )(a, w, x)


def _glu_res_kernel(a_ref, wv_ref, wg_ref, x_ref, o_ref):
    a = a_ref[...]
    val = _wdot(a, wv_ref)
    gate = _wdot(a, wg_ref)
    o_ref[...] = x_ref[...] + val * jax.nn.sigmoid(gate)


def _glu_res(a, w_glu, x, layer):
    s, k = a.shape
    n = w_glu.shape[2] // 2
    tm, tn = min(ROW_TILE, s), COL_TILE
    nj = n // tn
    return pl.pallas_call(
        _glu_res_kernel,
        grid=(s // tm, nj),
        in_specs=[
            pl.BlockSpec((tm, k), lambda i, j: (i, 0)),
            pl.BlockSpec((None, k, tn), lambda i, j: (layer, 0, j)),
            pl.BlockSpec((None, k, tn), lambda i, j: (layer, 0, j + nj)),
            pl.BlockSpec((tm, tn), lambda i, j: (i, j)),
        ],
        out_specs=pl.BlockSpec((tm, tn), lambda i, j: (i, j)),
        out_shape=jax.ShapeDtypeStruct((s, n), F32),
        compiler_params=_params("parallel", "arbitrary"),
        name="glu_res",
    )(a, w_glu, w_glu, x)


def _conv_front_kernel(x_ref, g_ref, wb_ref, wc_ref, wv_ref, cw_ref, o_ref, h_ref, ext_ref, carry_ref):
    i, j = pl.program_id(0), pl.program_id(1)
    tm = o_ref.shape[0]

    @pl.when(j == 0)
    def _():
        h_ref[...] = _rms_norm(x_ref[...], g_ref[...]).astype(BF16)

    @pl.when(i == 0)
    def _():
        carry_ref[j] = jnp.zeros(carry_ref.shape[1:], F32)

    h = h_ref[...]
    u = _wdot(h, wc_ref) * _wdot(h, wv_ref)
    ext_ref[0:HALO, :] = carry_ref[j]
    ext_ref[HALO:, :] = u
    carry_ref[j] = u[tm - HALO:, :]
    conv = cw_ref[CONV_WIDTH - 1:CONV_WIDTH, :] * u
    for lag in range(1, CONV_WIDTH):
        tap = CONV_WIDTH - 1 - lag
        conv += cw_ref[tap:tap + 1, :] * ext_ref[pl.ds(HALO - lag, tm), :]
    o_ref[...] = (_wdot(h, wb_ref) * conv).astype(BF16)


def _conv_front(x, gain, w_in, conv_w, layer):
    s, d = x.shape
    tm, tn = min(ROW_TILE, s), COL_TILE
    nj = d // tn
    return pl.pallas_call(
        _conv_front_kernel,
        grid=(s // tm, nj),
        in_specs=[
            pl.BlockSpec((tm, d), lambda i, j: (i, 0)),
            pl.BlockSpec((1, d), lambda i, j: (0, 0)),
            pl.BlockSpec((None, d, tn), lambda i, j: (layer, 0, j)),
            pl.BlockSpec((None, d, tn), lambda i, j: (layer, 0, j + nj)),
            pl.BlockSpec((None, d, tn), lambda i, j: (layer, 0, j + 2 * nj)),
            pl.BlockSpec((CONV_WIDTH, tn), lambda i, j: (0, j)),
        ],
        out_specs=pl.BlockSpec((tm, tn), lambda i, j: (i, j)),
        out_shape=jax.ShapeDtypeStruct((s, d), BF16),
        scratch_shapes=[
            pltpu.VMEM((tm, d), BF16),
            pltpu.VMEM((tm + HALO, tn), F32),
            pltpu.VMEM((nj, HALO, tn), F32),
        ],
        compiler_params=_params("arbitrary", "arbitrary"),
        name="conv_front",
    )(x, gain, w_in, w_in, w_in, conv_w)


def _pool_kernel(x_ref, g_ref, win_ref, wg_ref, sc_ref, xr_ref, o_ref, h_ref, ext_ref, carry_ref):
    i, grp = pl.program_id(0), pl.program_id(1)
    tm = o_ref.shape[0]

    @pl.when(grp == 0)
    def _():
        h_ref[...] = _rms_norm(x_ref[...], g_ref[...]).astype(BF16)

    @pl.when(i == 0)
    def _():
        carry_ref[grp] = jnp.zeros(carry_ref.shape[1:], F32)

    u = _wdot(h_ref[...], win_ref)
    ext_ref[0:HALO, :] = carry_ref[grp]
    ext_ref[HALO:, :] = u
    carry_ref[grp] = u[tm - HALO:, :]
    pos = (i * tm + 1 + lax.broadcasted_iota(jnp.int32, (tm, 1), 0)).astype(F32)

    for gi, w in enumerate(POOL_WINDOWS):
        @pl.when(grp == gi)
        def _(w=w):
            acc = u
            for lag in range(1, w):
                acc += ext_ref[pl.ds(HALO - lag, tm), :]
            inv_count = 1.0 / jnp.minimum(pos, float(w))
            pooled = (acc * inv_count - u).astype(BF16)
            o_ref[...] = xr_ref[...] + _dot(pooled, wg_ref[0].astype(BF16)) * sc_ref[...]


def _pool_mixer(x, gain, w_in, w_group, scale, layer):
    s, d = x.shape
    ng, pg = w_group.shape[1], w_group.shape[2]
    tm = min(ROW_TILE, s)
    return pl.pallas_call(
        _pool_kernel,
        grid=(s // tm, ng),
        in_specs=[
            pl.BlockSpec((tm, d), lambda i, g: (i, 0)),
            pl.BlockSpec((1, d), lambda i, g: (0, 0)),
            pl.BlockSpec((None, d, pg), lambda i, g: (layer, 0, g)),
            pl.BlockSpec((None, 1, pg, pg), lambda i, g: (layer, g, 0, 0)),
            pl.BlockSpec((1, pg), lambda i, g: (0, g)),
            pl.BlockSpec((tm, pg), lambda i, g: (i, g)),
        ],
        out_specs=pl.BlockSpec((tm, pg), lambda i, g: (i, g)),
        out_shape=jax.ShapeDtypeStruct((s, d), F32),
        scratch_shapes=[
            pltpu.VMEM((tm, d), BF16),
            pltpu.VMEM((tm + HALO, pg), F32),
            pltpu.VMEM((ng, HALO, pg), F32),
        ],
        compiler_params=_params("arbitrary", "arbitrary"),
        name="pool_mixer",
    )(x, gain, w_in, w_group, scale, x)


def _qkv_kernel(x_ref, g_ref, w_ref, hg_ref, o_ref, h_ref, *, n_norm_tiles):
    j = pl.program_id(1)

    @pl.when(j == 0)
    def _():
        h_ref[...] = _rms_norm(x_ref[...], g_ref[...]).astype(BF16)

    y = _wdot(h_ref[...], w_ref)

    @pl.when(j < n_norm_tiles)
    def _():
        for hd in range(y.shape[1] // ATT_HEAD_DIM):
            sl = slice(hd * ATT_HEAD_DIM, (hd + 1) * ATT_HEAD_DIM)
            o_ref[:, sl] = _rms_norm(y[:, sl], hg_ref[:, sl]).astype(BF16)

    @pl.when(j >= n_norm_tiles)
    def _():
        o_ref[...] = y.astype(BF16)


def _qkv_proj(x, gain, w_qkv, head_gains, layer):
    s, d = x.shape
    n = w_qkv.shape[2]
    tm, tn = min(ROW_TILE, s), COL_TILE
    n_norm_tiles = head_gains.shape[1] // tn
    return pl.pallas_call(
        functools.partial(_qkv_kernel, n_norm_tiles=n_norm_tiles),
        grid=(s // tm, n // tn),
        in_specs=[
            pl.BlockSpec((tm, d), lambda i, j: (i, 0)),
            pl.BlockSpec((1, d), lambda i, j: (0, 0)),
            pl.BlockSpec((None, d, tn), lambda i, j: (layer, 0, j)),
            pl.BlockSpec((1, tn), lambda i, j: (0, jnp.minimum(j, n_norm_tiles - 1))),
        ],
        out_specs=pl.BlockSpec((tm, tn), lambda i, j: (i, j)),
        out_shape=jax.ShapeDtypeStruct((s, n), BF16),
        scratch_shapes=[pltpu.VMEM((tm, d), BF16)],
        compiler_params=_params("parallel", "arbitrary"),
        name="qkv_proj",
    )(x, gain, w_qkv, head_gains)


def _attn_kernel(q_ref, k0_ref, k1_ref, k2_ref, v0_ref, v1_ref, v2_ref, bias_ref, o_ref):
    b = pl.program_id(1)
    qb = q_ref.shape[0]
    scale = ATT_HEAD_DIM ** -0.5
    for hh in range(q_ref.shape[1] // ATT_HEAD_DIM):
        cols = slice(hh * ATT_HEAD_DIM, (hh + 1) * ATT_HEAD_DIM)
        q = q_ref[:, cols]
        scores = []
        for c, k_ref in enumerate((k0_ref, k1_ref, k2_ref)):
            s_c = lax.dot_general(q, k_ref[:, cols], (((1,), (1,)), ((), ())),
                                  preferred_element_type=F32)
            s_c = s_c * scale + bias_ref[hh, :, c * qb:(c + 1) * qb]
            if c < 2:
                s_c = jnp.where(b - 2 + c >= 0, s_c, MASK_VALUE)
            scores.append(s_c)
        m = jnp.maximum(jnp.maximum(jnp.max(scores[0], axis=-1, keepdims=True),
                                    jnp.max(scores[1], axis=-1, keepdims=True)),
                        jnp.max(scores[2], axis=-1, keepdims=True))
        acc = jnp.zeros((qb, ATT_HEAD_DIM), F32)
        denom = jnp.zeros((qb, 1), F32)
        for s_c, v_ref in zip(scores, (v0_ref, v1_ref, v2_ref)):
            p = jnp.exp(s_c - m)
            denom += jnp.sum(p, axis=-1, keepdims=True)
            acc += _dot(p.astype(BF16), v_ref[:, cols])
        o_ref[:, cols] = (acc / denom).astype(BF16)


def _attention(qkv, bias, n_heads):
    s = qkv.shape[0]
    qb = ATT_QBLOCK
    nb = s // qb
    hps = ATT_HEADS_PER_STEP
    width = hps * ATT_HEAD_DIM
    ngroups = n_heads // hps

    def kv_spec(offset, back):
        return pl.BlockSpec((qb, width), lambda h, b: (jnp.maximum(b - back, 0), offset + h))

    return pl.pallas_call(
        _attn_kernel,
        grid=(ngroups, nb),
        in_specs=[
            pl.BlockSpec((qb, width), lambda h, b: (b, h)),
            kv_spec(ngroups, 2), kv_spec(ngroups, 1), kv_spec(ngroups, 0),
            kv_spec(2 * ngroups, 2), kv_spec(2 * ngroups, 1), kv_spec(2 * ngroups, 0),
            pl.BlockSpec((hps, qb, 3 * qb), lambda h, b: (h, 0, 0)),
        ],
        out_specs=pl.BlockSpec((qb, width), lambda h, b: (b, h)),
        out_shape=jax.ShapeDtypeStruct((s, n_heads * ATT_HEAD_DIM), BF16),
        compiler_params=_params("parallel", "arbitrary"),
        name="chunk_attention",
    )(qkv, qkv, qkv, qkv, qkv, qkv, qkv, bias)


def _attention_bias(rel_bias):
    qb = ATT_QBLOCK
    n_heads = rel_bias.shape[0]
    assert 2 * qb >= REL_CLIP and 2 * REL_CLIP + 1 == rel_bias.shape[1]
    n_flat = 2 * qb - REL_CLIP + qb - 1
    n_ramp = 4 * qb - 1 - n_flat
    profile = jnp.concatenate([
        jnp.broadcast_to(rel_bias[:, 2 * REL_CLIP:], (n_heads, n_flat)),
        jnp.flip(rel_bias[:, 2 * REL_CLIP + 1 - n_ramp:2 * REL_CLIP + 1], axis=1),
        jnp.zeros((n_heads, 1), rel_bias.dtype),
    ], axis=1)
    period = 4 * qb
    rows = jnp.tile(profile, (1, qb))[:, :qb * (period - 1)].reshape(n_heads, qb, period - 1)
    table = rows[:, :, qb - 1:]
    q_idx = jnp.arange(qb)[:, None] + 2 * qb
    k_idx = jnp.arange(3 * qb)[None, :]
    chunk_start = (q_idx // CHUNK) * CHUNK
    in_band = (k_idx >= chunk_start - ATT_LEFT_CHUNKS * CHUNK) & (k_idx < chunk_start + CHUNK)
    return jnp.where(in_band[None], table.astype(F32), MASK_VALUE)


def _ssm_prep_kernel(are_ref, aim_ref, ldt_ref, btr_ref, bti_ref,
                     abr_ref, abi_ref, apr_ref, api_ref, bbr_ref, bbi_ref, *, n_squarings):
    lam_r, lam_i = are_ref[...], aim_ref[...]
    dt = jnp.exp(ldt_ref[...])
    mag = jnp.exp(lam_r * dt)
    ab_r, ab_i = mag * jnp.cos(lam_i * dt), mag * jnp.sin(lam_i * dt)
    abr_ref[...] = ab_r
    abi_ref[...] = ab_i
    p_r, p_i = ab_r, ab_i
    for _ in range(n_squarings):
        p_r, p_i = p_r * p_r - p_i * p_i, 2.0 * p_r * p_i
    apr_ref[...] = p_r
    api_ref[...] = p_i
    num_r, num_i = ab_r - 1.0, ab_i
    den = lam_r * lam_r + lam_i * lam_i
    co_r = (num_r * lam_r + num_i * lam_i) / den
    co_i = (num_i * lam_r - num_r * lam_i) / den
    for c in range(btr_ref.shape[0]):
        b_r, b_i = btr_ref[c], bti_ref[c]
        bbr_ref[c] = co_r * b_r - co_i * b_i
        bbi_ref[c] = co_r * b_i + co_i * b_r


def _ssm_prep(a_re, a_im, log_dt, bt_re, bt_im, n_squarings):
    g, n = a_re.shape
    gn = jax.ShapeDtypeStruct((g, n), F32)
    cgn = jax.ShapeDtypeStruct(bt_re.shape, F32)
    return pl.pallas_call(
        functools.partial(_ssm_prep_kernel, n_squarings=n_squarings),
        out_shape=(gn, gn, gn, gn, cgn, cgn),
        name="ssm_prep",
    )(a_re, a_im, log_dt, bt_re, bt_im)


def _ssm_kernel(x_ref, g_ref, d_ref, wbr_ref, wbi_ref, wcr_ref, wci_ref,
                ar_ref, ai_ref, pr_ref, pi_ref, o_ref,
                t3_ref, hp_ref, y_ref, sr_ref, si_ref, cr_ref, ci_ref):
    i = pl.program_id(0)
    t = x_ref.shape[0]
    nlb = t3_ref.shape[0]
    seg = t // SUBLANES
    nblk, cb, ns = wbr_ref.shape

    h = _rms_norm(x_ref[...], g_ref[...])
    for c in range(nlb):
        lanes = slice(c * LANES, (c + 1) * LANES)
        t3_ref[c] = h[:, lanes]
        for j in range(seg):
            hp_ref[j * SUBLANES:(j + 1) * SUBLANES, lanes] = (
                t3_ref[c, pl.ds(j, SUBLANES, stride=seg), :])

    @pl.when(i == 0)
    def _():
        cr_ref[...] = jnp.zeros(cr_ref.shape, F32)
        ci_ref[...] = jnp.zeros(ci_ref.shape, F32)

    zeros = jnp.zeros((SUBLANES, ns), F32)
    sub = lax.broadcasted_iota(jnp.int32, (SUBLANES, ns), 0)

    def group_block(gb, _):
        col = pl.multiple_of(gb * cb, LANES)
        hb = hp_ref[:, pl.ds(col, cb)].astype(BF16)
        sr_ref[...] = _dot(hb, wbr_ref[gb])
        si_ref[...] = _dot(hb, wbi_ref[gb])
        a_r = jnp.broadcast_to(ar_ref[gb], (SUBLANES, ns))
        a_i = jnp.broadcast_to(ai_ref[gb], (SUBLANES, ns))

        def step(j, carry, store):
            x_r, x_i = carry
            rows = pl.ds(pl.multiple_of(j * SUBLANES, SUBLANES), SUBLANES)
            n_r = a_r * x_r - a_i * x_i + sr_ref[rows, :]
            n_i = a_r * x_i + a_i * x_r + si_ref[rows, :]
            if store:
                sr_ref[rows, :] = n_r
                si_ref[rows, :] = n_i
            return n_r, n_i

        f_r, f_i = lax.fori_loop(0, seg, functools.partial(step, store=False), (zeros, zeros),
                                 unroll=4)
        p_r, p_i = pr_ref[gb], pi_ref[gb]
        row_r, row_i = cr_ref[gb], ci_ref[gb]
        init_r, init_i = zeros, zeros
        for s_idx in range(SUBLANES):
            init_r = jnp.where(sub == s_idx, jnp.broadcast_to(row_r, (SUBLANES, ns)), init_r)
            init_i = jnp.where(sub == s_idx, jnp.broadcast_to(row_i, (SUBLANES, ns)), init_i)
            if s_idx + 1 < SUBLANES:
                row_r, row_i = (p_r * row_r - p_i * row_i + f_r[s_idx:s_idx + 1, :],
                                p_r * row_i + p_i * row_r + f_i[s_idx:s_idx + 1, :])
        l_r, l_i = lax.fori_loop(0, seg, functools.partial(step, store=True), (init_r, init_i),
                                 unroll=4)
        cr_ref[gb] = l_r[SUBLANES - 1:SUBLANES, :]
        ci_ref[gb] = l_i[SUBLANES - 1:SUBLANES, :]
        y_ref[:, pl.ds(col, cb)] = (_dot(sr_ref[...].astype(BF16), wcr_ref[gb])
                                    + _dot(si_ref[...].astype(BF16), wci_ref[gb]))
        return 0

    lax.fori_loop(0, nblk, group_block, 0)

    z = jax.nn.gelu(y_ref[...] + d_ref[...] * hp_ref[...])
    for c in range(nlb):
        lanes = slice(c * LANES, (c + 1) * LANES)
        t3_ref[c] = z[:, lanes]
        for s_idx in range(SUBLANES):
            o_ref[s_idx * seg:(s_idx + 1) * seg, lanes] = (
                t3_ref[c, pl.ds(s_idx, seg, stride=SUBLANES), :].astype(BF16))


def _ssm_core(x, gain, d_skip, wb_r, wb_i, wc_r, wc_i, ab_r, ab_i, ap_r, ap_i):
    s, d = x.shape
    t = min(SSM_ROW_TILE, s)
    nblk, cb, ns = wb_r.shape

    def whole(a):
        return pl.BlockSpec(a.shape, lambda i: (0,) * a.ndim)

    return pl.pallas_call(
        _ssm_kernel,
        grid=(s // t,),
        in_specs=[
            pl.BlockSpec((t, d), lambda i: (i, 0)),
            whole(gain), whole(d_skip),
            whole(wb_r), whole(wb_i), whole(wc_r), whole(wc_i),
            whole(ab_r), whole(ab_i), whole(ap_r), whole(ap_i),
        ],
        out_specs=pl.BlockSpec((t, d), lambda i: (i, 0)),
        out_shape=jax.ShapeDtypeStruct((s, d), BF16),
        scratch_shapes=[
            pltpu.VMEM((d // LANES, t, LANES), F32),
            pltpu.VMEM((t, d), F32),
            pltpu.VMEM((t, d), F32),
            pltpu.VMEM((t, ns), F32),
            pltpu.VMEM((t, ns), F32),
            pltpu.VMEM((nblk, 1, ns), F32),
            pltpu.VMEM((nblk, 1, ns), F32),
        ],
        compiler_params=_params("arbitrary"),
        name="ssm_core",
    )(x, gain, d_skip, wb_r, wb_i, wc_r, wc_i, ab_r, ab_i, ap_r, ap_i)


def _block_diag(w, groups_per_block):
    g, r, c = w.shape
    nblk = g // groups_per_block
    w = w.reshape(nblk, groups_per_block, r, 1, c)
    eye = jnp.eye(groups_per_block, dtype=w.dtype).reshape(1, groups_per_block, 1, groups_per_block, 1)
    return (w * eye).reshape(nblk, groups_per_block * r, groups_per_block * c)


def _s5_mixer(x, gain, a_re, a_im, log_dt, b_re, b_im, c_re, c_im, d_skip, w_glu, layer):
    s, d = x.shape
    g, n = a_re.shape
    gpb = SSM_GROUPS_PER_BLOCK
    seg = min(SSM_ROW_TILE, s) // SUBLANES
    n_squarings = int(math.log2(seg))
    assert 2 ** n_squarings == seg
    bt_re, bt_im = b_re.transpose(2, 0, 1), b_im.transpose(2, 0, 1)
    ab_r, ab_i, ap_r, ap_i, bb_r, bb_i = _ssm_prep(a_re, a_im, log_dt.reshape(g, 1), bt_re, bt_im,
                                                   n_squarings)
    wb_r = _block_diag(bb_r.transpose(1, 0, 2), gpb).astype(BF16)
    wb_i = _block_diag(bb_i.transpose(1, 0, 2), gpb).astype(BF16)
    wc_r = _block_diag(c_re.transpose(0, 2, 1), gpb).astype(BF16)
    wc_i = _block_diag(-c_im.transpose(0, 2, 1), gpb).astype(BF16)
    flat = lambda a: a.reshape(g // gpb, 1, gpb * n)
    z = _ssm_core(x, gain, d_skip.reshape(1, d), wb_r, wb_i, wc_r, wc_i,
                  flat(ab_r), flat(ab_i), flat(ap_r), flat(ap_i))
    return _glu_res(z, w_glu, x, layer)


def kernel(x, norm_mix, norm_mlp, mlp_w1, mlp_w2, conv_w_in, conv_w, conv_w_out, pool_w_in, pool_w_group, pool_scale, att_w_qkv, att_q_norm, att_k_norm, att_rel_bias, att_w_out, ssm_a_re, ssm_a_im, ssm_log_dt, ssm_b_re, ssm_b_im, ssm_c_re, ssm_c_im, ssm_d, ssm_w_glu):
    b, s, d = x.shape
    depth = norm_mix.shape[0]
    n_mixers = 4
    outs = []
    for bi in range(b):
        xs = x[bi]
        for i in range(depth):
            kind, j = i % n_mixers, i // n_mixers
            gain = norm_mix[i].reshape(1, d)
            if kind == 0:
                gated = _conv_front(xs, gain, conv_w_in, conv_w[j], j)
                xs = _proj_res(gated, conv_w_out, xs, j)
            elif kind == 1:
                xs = _pool_mixer(xs, gain, pool_w_in, pool_w_group, pool_scale[j].reshape(1, d), j)
            elif kind == 2:
                n_heads = d // ATT_HEAD_DIM
                head_gains = jnp.concatenate([jnp.tile(att_q_norm[j], n_heads),
                                              jnp.tile(att_k_norm[j], n_heads)]).reshape(1, 2 * d)
                qkv = _qkv_proj(xs, gain, att_w_qkv, head_gains, j)
                att = _attention(qkv, _attention_bias(att_rel_bias[j]), n_heads)
                xs = _proj_res(att, att_w_out, xs, j)
            else:
                xs = _s5_mixer(xs, gain, ssm_a_re[j], ssm_a_im[j], ssm_log_dt[j], ssm_b_re[j],
                               ssm_b_im[j], ssm_c_re[j], ssm_c_im[j], ssm_d[j], ssm_w_glu, j)
            xs = _mlp(xs, norm_mlp[i].reshape(1, d), mlp_w1, mlp_w2, i)
        outs.append(xs)
    return outs[0][None] if b == 1 else jnp.stack(outs)
```

```python
import functools
import math

import jax
import jax.numpy as jnp
from jax import lax
from jax.experimental import pallas as pl
from jax.experimental.pallas import tpu as pltpu

F32 = jnp.float32
BF16 = jnp.bfloat16

RMS_EPS = 1e-6
CHUNK = 64
ATT_HEAD_DIM = 128
ATT_LEFT_CHUNKS = 8
REL_CLIP = 256
MASK_VALUE = -1e30
POOL_WINDOWS = (2, 4, 8, 16)
CONV_WIDTH = 3
SSM_GROUP = 16
SSM_STATE = 64

SUBLANES = 8
LANES = 128
VMEM_LIMIT_BYTES = 56 * 1024 * 1024

ROW_TILE = 1024
COL_TILE = 512
FF_TILE = 512
MLP_OUT_TILE = 256
MLP_K_TILE = 4096
ATT_QBLOCK = 256
ATT_HEADS_PER_STEP = 4
SSM_ROW_TILE = 512
SSM_GROUPS_PER_BLOCK = 8
HALO = 16


def _params(*semantics):
    return pltpu.CompilerParams(dimension_semantics=semantics, vmem_limit_bytes=VMEM_LIMIT_BYTES)


def _rms_norm(x, gain):
    return x * lax.rsqrt(jnp.mean(x * x, axis=-1, keepdims=True) + RMS_EPS) * gain


def _dot(a, b):
    return jnp.dot(a, b, preferred_element_type=F32)


def _wdot(a, w_ref):
    return _dot(a, w_ref[...].astype(BF16))


def _mlp_kernel(x_ref, g_ref, w1_ref, w2_ref, xr_ref, o_ref, h_ref, a_ref, *, n_up, n_k):
    t = pl.program_id(1)
    tf = w1_ref.shape[1]
    tk = w2_ref.shape[0]

    @pl.when(t == 0)
    def _():
        h_ref[...] = _rms_norm(x_ref[...], g_ref[...]).astype(BF16)

    @pl.when(t < n_up)
    def _():
        a = jnp.maximum(_wdot(h_ref[...], w1_ref), 0.0)
        a_ref[:, pl.ds(pl.multiple_of(t * tf, tf), tf)] = (a * a).astype(BF16)

    @pl.when(t >= n_up)
    def _():
        kh = (t - n_up) % n_k
        part = _wdot(a_ref[:, pl.ds(pl.multiple_of(kh * tk, tk), tk)], w2_ref)

        @pl.when(kh == 0)
        def _():
            o_ref[...] = xr_ref[...] + part

        @pl.when(kh > 0)
        def _():
            o_ref[...] += part


def _mlp(x, gain, w1, w2, layer):
    s, d = x.shape
    dff = w1.shape[2]
    tm, tf, tn, tk = min(ROW_TILE, s), FF_TILE, MLP_OUT_TILE, MLP_K_TILE
    n_up, n_k = dff // tf, dff // tk
    n_steps = n_up + (d // tn) * n_k

    def down(t):
        return jnp.maximum(t - n_up, 0)

    return pl.pallas_call(
        functools.partial(_mlp_kernel, n_up=n_up, n_k=n_k),
        grid=(s // tm, n_steps),
        in_specs=[
            pl.BlockSpec((tm, d), lambda i, t: (i, 0), pipeline_mode=pl.Buffered(1)),
            pl.BlockSpec((1, d), lambda i, t: (0, 0)),
            pl.BlockSpec((None, d, tf), lambda i, t: (layer, 0, jnp.minimum(t, n_up - 1))),
            pl.BlockSpec((None, tk, tn), lambda i, t: (layer, down(t) % n_k, down(t) // n_k)),
            pl.BlockSpec((tm, tn), lambda i, t: (i, down(t) // n_k)),
        ],
        out_specs=pl.BlockSpec((tm, tn), lambda i, t: (i, down(t) // n_k)),
        out_shape=jax.ShapeDtypeStruct((s, d), F32),
        scratch_shapes=[pltpu.VMEM((tm, d), BF16), pltpu.VMEM((tm, dff), BF16)],
        compiler_params=_params("parallel", "arbitrary"),
        name="mlp",
    )(x, gain, w1, w2, x)


def _proj_res_kernel(a_ref, w_ref, x_ref, o_ref):
    o_ref[...] = x_ref[...] + _wdot(a_ref[...], w_ref)


def _proj_res(a, w, x, layer):
    s, k = a.shape
    n = w.shape[2]
    tm, tn = min(ROW_TILE, s), COL_TILE
    return pl.pallas_call(
        _proj_res_kernel,
        grid=(s // tm, n // tn),
        in_specs=[
            pl.BlockSpec((tm, k), lambda i, j: (i, 0)),
            pl.BlockSpec((None, k, tn), lambda i, j: (layer, 0, j)),
            pl.BlockSpec((tm, tn), lambda i, j: (i, j)),
        ],
        out_specs=pl.BlockSpec((tm, tn), lambda i, j: (i, j)),
        out_shape=jax.ShapeDtypeStruct((s, n), F32),
        compiler_params=_params("parallel", "arbitrary"),
        name="proj_res",
    )(a, w, x)


def _glu_res_kernel(a_ref, wv_ref, wg_ref, x_ref, o_ref):
    a = a_ref[...]
    val = _wdot(a, wv_ref)
    gate = _wdot(a, wg_ref)
    o_ref[...] = x_ref[...] + val * jax.nn.sigmoid(gate)


def _glu_res(a, w_glu, x, layer):
    s, k = a.shape
    n = w_glu.shape[2] // 2
    tm, tn = min(ROW_TILE, s), COL_TILE
    nj = n // tn
    return pl.pallas_call(
        _glu_res_kernel,
        grid=(s // tm, nj),
        in_specs=[
            pl.BlockSpec((tm, k), lambda i, j: (i, 0)),
            pl.BlockSpec((None, k, tn), lambda i, j: (layer, 0, j)),
            pl.BlockSpec((None, k, tn), lambda i, j: (layer, 0, j + nj)),
            pl.BlockSpec((tm, tn), lambda i, j: (i, j)),
        ],
        out_specs=pl.BlockSpec((tm, tn), lambda i, j: (i, j)),
        out_shape=jax.ShapeDtypeStruct((s, n), F32),
        compiler_params=_params("parallel", "arbitrary"),
        name="glu_res",
    )(a, w_glu, w_glu, x)


def _conv_front_kernel(x_ref, g_ref, wb_ref, wc_ref, wv_ref, cw_ref, o_ref, h_ref, ext_ref, carry_ref):
    i, j = pl.program_id(0), pl.program_id(1)
    tm = o_ref.shape[0]

    @pl.when(j == 0)
    def _():
        h_ref[...] = _rms_norm(x_ref[...], g_ref[...]).astype(BF16)

    @pl.when(i == 0)
    def _():
        carry_ref[j] = jnp.zeros(carry_ref.shape[1:], F32)

    h = h_ref[...]
    u = _wdot(h, wc_ref) * _wdot(h, wv_ref)
    ext_ref[0:HALO, :] = carry_ref[j]
    ext_ref[HALO:, :] = u
    carry_ref[j] = u[tm - HALO:, :]
    conv = cw_ref[CONV_WIDTH - 1:CONV_WIDTH, :] * u
    for lag in range(1, CONV_WIDTH):
        tap = CONV_WIDTH - 1 - lag
        conv += cw_ref[tap:tap + 1, :] * ext_ref[pl.ds(HALO - lag, tm), :]
    o_ref[...] = (_wdot(h, wb_ref) * conv).astype(BF16)


def _conv_front(x, gain, w_in, conv_w, layer):
    s, d = x.shape
    tm, tn = min(ROW_TILE, s), COL_TILE
    nj = d // tn
    return pl.pallas_call(
        _conv_front_kernel,
        grid=(s // tm, nj),
        in_specs=[
            pl.BlockSpec((tm, d), lambda i, j: (i, 0)),
            pl.BlockSpec((1, d), lambda i, j: (0, 0)),
            pl.BlockSpec((None, d, tn), lambda i, j: (layer, 0, j)),
            pl.BlockSpec((None, d, tn), lambda i, j: (layer, 0, j + nj)),
            pl.BlockSpec((None, d, tn), lambda i, j: (layer, 0, j + 2 * nj)),
            pl.BlockSpec((CONV_WIDTH, tn), lambda i, j: (0, j)),
        ],
        out_specs=pl.BlockSpec((tm, tn), lambda i, j: (i, j)),
        out_shape=jax.ShapeDtypeStruct((s, d), BF16),
        scratch_shapes=[
            pltpu.VMEM((tm, d), BF16),
            pltpu.VMEM((tm + HALO, tn), F32),
            pltpu.VMEM((nj, HALO, tn), F32),
        ],
        compiler_params=_params("arbitrary", "arbitrary"),
        name="conv_front",
    )(x, gain, w_in, w_in, w_in, conv_w)


def _pool_kernel(x_ref, g_ref, win_ref, wg_ref, sc_ref, xr_ref, o_ref, h_ref, ext_ref, carry_ref):
    i, grp = pl.program_id(0), pl.program_id(1)
    tm = o_ref.shape[0]

    @pl.when(grp == 0)
    def _():
        h_ref[...] = _rms_norm(x_ref[...], g_ref[...]).astype(BF16)

    @pl.when(i == 0)
    def _():
        carry_ref[grp] = jnp.zeros(carry_ref.shape[1:], F32)

    u = _wdot(h_ref[...], win_ref)
    ext_ref[0:HALO, :] = carry_ref[grp]
    ext_ref[HALO:, :] = u
    carry_ref[grp] = u[tm - HALO:, :]
    pos = (i * tm + 1 + lax.broadcasted_iota(jnp.int32, (tm, 1), 0)).astype(F32)

    for gi, w in enumerate(POOL_WINDOWS):
        @pl.when(grp == gi)
        def _(w=w):
            acc = u
            for lag in range(1, w):
                acc += ext_ref[pl.ds(HALO - lag, tm), :]
            inv_count = 1.0 / jnp.minimum(pos, float(w))
            pooled = (acc * inv_count - u).astype(BF16)
            o_ref[...] = xr_ref[...] + _dot(pooled, wg_ref[0].astype(BF16)) * sc_ref[...]


def _pool_mixer(x, gain, w_in, w_group, scale, layer):
    s, d = x.shape
    ng, pg = w_group.shape[1], w_group.shape[2]
    tm = min(ROW_TILE, s)
    return pl.pallas_call(
        _pool_kernel,
        grid=(s // tm, ng),
        in_specs=[
            pl.BlockSpec((tm, d), lambda i, g: (i, 0)),
            pl.BlockSpec((1, d), lambda i, g: (0, 0)),
            pl.BlockSpec((None, d, pg), lambda i, g: (layer, 0, g)),
            pl.BlockSpec((None, 1, pg, pg), lambda i, g: (layer, g, 0, 0)),
            pl.BlockSpec((1, pg), lambda i, g: (0, g)),
            pl.BlockSpec((tm, pg), lambda i, g: (i, g)),
        ],
        out_specs=pl.BlockSpec((tm, pg), lambda i, g: (i, g)),
        out_shape=jax.ShapeDtypeStruct((s, d), F32),
        scratch_shapes=[
            pltpu.VMEM((tm, d), BF16),
            pltpu.VMEM((tm + HALO, pg), F32),
            pltpu.VMEM((ng, HALO, pg), F32),
        ],
        compiler_params=_params("arbitrary", "arbitrary"),
        name="pool_mixer",
    )(x, gain, w_in, w_group, scale, x)


def _qkv_kernel(x_ref, g_ref, w_ref, hg_ref, o_ref, h_ref, *, n_norm_tiles):
    j = pl.program_id(1)

    @pl.when(j == 0)
    def _():
        h_ref[...] = _rms_norm(x_ref[...], g_ref[...]).astype(BF16)

    y = _wdot(h_ref[...], w_ref)

    @pl.when(j < n_norm_tiles)
    def _():
        for hd in range(y.shape[1] // ATT_HEAD_DIM):
            sl = slice(hd * ATT_HEAD_DIM, (hd + 1) * ATT_HEAD_DIM)
            o_ref[:, sl] = _rms_norm(y[:, sl], hg_ref[:, sl]).astype(BF16)

    @pl.when(j >= n_norm_tiles)
    def _():
        o_ref[...] = y.astype(BF16)


def _qkv_proj(x, gain, w_qkv, head_gains, layer):
    s, d = x.shape
    n = w_qkv.shape[2]
    tm, tn = min(ROW_TILE, s), COL_TILE
    n_norm_tiles = head_gains.shape[1] // tn
    return pl.pallas_call(
        functools.partial(_qkv_kernel, n_norm_tiles=n_norm_tiles),
        grid=(s // tm, n // tn),
        in_specs=[
            pl.BlockSpec((tm, d), lambda i, j: (i, 0)),
            pl.BlockSpec((1, d), lambda i, j: (0, 0)),
            pl.BlockSpec((None, d, tn), lambda i, j: (layer, 0, j)),
            pl.BlockSpec((1, tn), lambda i, j: (0, jnp.minimum(j, n_norm_tiles - 1))),
        ],
        out_specs=pl.BlockSpec((tm, tn), lambda i, j: (i, j)),
        out_shape=jax.ShapeDtypeStruct((s, n), BF16),
        scratch_shapes=[pltpu.VMEM((tm, d), BF16)],
        compiler_params=_params("parallel", "arbitrary"),
        name="qkv_proj",
    )(x, gain, w_qkv, head_gains)


def _attn_kernel(q_ref, k0_ref, k1_ref, k2_ref, v0_ref, v1_ref, v2_ref, bias_ref, o_ref):
    b = pl.program_id(1)
    qb = q_ref.shape[0]
    scale = ATT_HEAD_DIM ** -0.5
    for hh in range(q_ref.shape[1] // ATT_HEAD_DIM):
        cols = slice(hh * ATT_HEAD_DIM, (hh + 1) * ATT_HEAD_DIM)
        q = q_ref[:, cols]
        scores = []
        for c, k_ref in enumerate((k0_ref, k1_ref, k2_ref)):
            s_c = lax.dot_general(q, k_ref[:, cols], (((1,), (1,)), ((), ())),
                                  preferred_element_type=F32)
            s_c = s_c * scale + bias_ref[hh, :, c * qb:(c + 1) * qb]
            if c < 2:
                s_c = jnp.where(b - 2 + c >= 0, s_c, MASK_VALUE)
            scores.append(s_c)
        m = jnp.maximum(jnp.maximum(jnp.max(scores[0], axis=-1, keepdims=True),
                                    jnp.max(scores[1], axis=-1, keepdims=True)),
                        jnp.max(scores[2], axis=-1, keepdims=True))
        acc = jnp.zeros((qb, ATT_HEAD_DIM), F32)
        denom = jnp.zeros((qb, 1), F32)
        for s_c, v_ref in zip(scores, (v0_ref, v1_ref, v2_ref)):
            p = jnp.exp(s_c - m)
            denom += jnp.sum(p, axis=-1, keepdims=True)
            acc += _dot(p.astype(BF16), v_ref[:, cols])
        o_ref[:, cols] = (acc / denom).astype(BF16)


def _attention(qkv, bias, n_heads):
    s = qkv.shape[0]
    qb = ATT_QBLOCK
    nb = s // qb
    hps = ATT_HEADS_PER_STEP
    width = hps * ATT_HEAD_DIM
    ngroups = n_heads // hps

    def kv_spec(offset, back):
        return pl.BlockSpec((qb, width), lambda h, b: (jnp.maximum(b - back, 0), offset + h))

    return pl.pallas_call(
        _attn_kernel,
        grid=(ngroups, nb),
        in_specs=[
            pl.BlockSpec((qb, width), lambda h, b: (b, h)),
            kv_spec(ngroups, 2), kv_spec(ngroups, 1), kv_spec(ngroups, 0),
            kv_spec(2 * ngroups, 2), kv_spec(2 * ngroups, 1), kv_spec(2 * ngroups, 0),
            pl.BlockSpec((hps, qb, 3 * qb), lambda h, b: (h, 0, 0)),
        ],
        out_specs=pl.BlockSpec((qb, width), lambda h, b: (b, h)),
        out_shape=jax.ShapeDtypeStruct((s, n_heads * ATT_HEAD_DIM), BF16),
        compiler_params=_params("parallel", "arbitrary"),
        name="chunk_attention",
    )(qkv, qkv, qkv, qkv, qkv, qkv, qkv, bias)


def _attention_bias(rel_bias):
    qb = ATT_QBLOCK
    n_heads = rel_bias.shape[0]
    assert 2 * qb >= REL_CLIP and 2 * REL_CLIP + 1 == rel_bias.shape[1]
    n_flat = 2 * qb - REL_CLIP + qb - 1
    n_ramp = 4 * qb - 1 - n_flat
    profile = jnp.concatenate([
        jnp.broadcast_to(rel_bias[:, 2 * REL_CLIP:], (n_heads, n_flat)),
        jnp.flip(rel_bias[:, 2 * REL_CLIP + 1 - n_ramp:2 * REL_CLIP + 1], axis=1),
        jnp.zeros((n_heads, 1), rel_bias.dtype),
    ], axis=1)
    period = 4 * qb
    rows = jnp.tile(profile, (1, qb))[:, :qb * (period - 1)].reshape(n_heads, qb, period - 1)
    table = rows[:, :, qb - 1:]
    q_idx = jnp.arange(qb)[:, None] + 2 * qb
    k_idx = jnp.arange(3 * qb)[None, :]
    chunk_start = (q_idx // CHUNK) * CHUNK
    in_band = (k_idx >= chunk_start - ATT_LEFT_CHUNKS * CHUNK) & (k_idx < chunk_start + CHUNK)
    return jnp.where(in_band[None], table.astype(F32), MASK_VALUE)


def _ssm_prep_kernel(are_ref, aim_ref, ldt_ref, btr_ref, bti_ref,
                     abr_ref, abi_ref, apr_ref, api_ref, bbr_ref, bbi_ref, *, n_squarings):
    lam_r, lam_i = are_ref[...], aim_ref[...]
    dt = jnp.exp(ldt_ref[...])
    mag = jnp.exp(lam_r * dt)
    ab_r, ab_i = mag * jnp.cos(lam_i * dt), mag * jnp.sin(lam_i * dt)
    abr_ref[...] = ab_r
    abi_ref[...] = ab_i
    p_r, p_i = ab_r, ab_i
    for _ in range(n_squarings):
        p_r, p_i = p_r * p_r - p_i * p_i, 2.0 * p_r * p_i
    apr_ref[...] = p_r
    api_ref[...] = p_i
    num_r, num_i = ab_r - 1.0, ab_i
    den = lam_r * lam_r + lam_i * lam_i
    co_r = (num_r * lam_r + num_i * lam_i) / den
    co_i = (num_i * lam_r - num_r * lam_i) / den
    for c in range(btr_ref.shape[0]):
        b_r, b_i = btr_ref[c], bti_ref[c]
        bbr_ref[c] = co_r * b_r - co_i * b_i
        bbi_ref[c] = co_r * b_i + co_i * b_r


def _ssm_prep(a_re, a_im, log_dt, bt_re, bt_im, n_squarings):
    g, n = a_re.shape
    gn = jax.ShapeDtypeStruct((g, n), F32)
    cgn = jax.ShapeDtypeStruct(bt_re.shape, F32)
    return pl.pallas_call(
        functools.partial(_ssm_prep_kernel, n_squarings=n_squarings),
        out_shape=(gn, gn, gn, gn, cgn, cgn),
        name="ssm_prep",
    )(a_re, a_im, log_dt, bt_re, bt_im)


def _ssm_kernel(x_ref, g_ref, d_ref, wbr_ref, wbi_ref, wcr_ref, wci_ref,
                ar_ref, ai_ref, pr_ref, pi_ref, o_ref,
                hp_ref, y_ref, sr_ref, si_ref, cr_ref, ci_ref):
    i = pl.program_id(0)
    t = x_ref.shape[0]
    seg = t // SUBLANES
    nblk, cb, ns = wbr_ref.shape
    assert cb == LANES and hp_ref.shape[0] == nblk

    h = _rms_norm(x_ref[...], g_ref[...])
    for c in range(nblk):
        for s_idx in range(SUBLANES):
            hp_ref[c, pl.ds(s_idx, seg, stride=SUBLANES), :] = (
                h[s_idx * seg:(s_idx + 1) * seg, c * LANES:(c + 1) * LANES])

    @pl.when(i == 0)
    def _():
        cr_ref[...] = jnp.zeros(cr_ref.shape, F32)
        ci_ref[...] = jnp.zeros(ci_ref.shape, F32)

    zeros = jnp.zeros((SUBLANES, ns), F32)
    sub = lax.broadcasted_iota(jnp.int32, (SUBLANES, ns), 0)

    def group_block(gb, _):
        hb = hp_ref[gb].astype(BF16)
        sr_ref[...] = _dot(hb, wbr_ref[gb])
        si_ref[...] = _dot(hb, wbi_ref[gb])
        a_r = jnp.broadcast_to(ar_ref[gb], (SUBLANES, ns))
        a_i = jnp.broadcast_to(ai_ref[gb], (SUBLANES, ns))

        def step(j, carry, store):
            x_r, x_i = carry
            rows = pl.ds(pl.multiple_of(j * SUBLANES, SUBLANES), SUBLANES)
            n_r = a_r * x_r - a_i * x_i + sr_ref[rows, :]
            n_i = a_r * x_i + a_i * x_r + si_ref[rows, :]
            if store:
                sr_ref[rows, :] = n_r
                si_ref[rows, :] = n_i
            return n_r, n_i

        f_r, f_i = lax.fori_loop(0, seg, functools.partial(step, store=False), (zeros, zeros),
                                 unroll=4)
        p_r, p_i = pr_ref[gb], pi_ref[gb]
        row_r, row_i = cr_ref[gb], ci_ref[gb]
        init_r, init_i = zeros, zeros
        for s_idx in range(SUBLANES):
            init_r = jnp.where(sub == s_idx, jnp.broadcast_to(row_r, (SUBLANES, ns)), init_r)
            init_i = jnp.where(sub == s_idx, jnp.broadcast_to(row_i, (SUBLANES, ns)), init_i)
            if s_idx + 1 < SUBLANES:
                row_r, row_i = (p_r * row_r - p_i * row_i + f_r[s_idx:s_idx + 1, :],
                                p_r * row_i + p_i * row_r + f_i[s_idx:s_idx + 1, :])
        l_r, l_i = lax.fori_loop(0, seg, functools.partial(step, store=True), (init_r, init_i),
                                 unroll=4)
        cr_ref[gb] = l_r[SUBLANES - 1:SUBLANES, :]
        ci_ref[gb] = l_i[SUBLANES - 1:SUBLANES, :]
        y_ref[gb] = (_dot(sr_ref[...].astype(BF16), wcr_ref[gb])
                     + _dot(si_ref[...].astype(BF16), wci_ref[gb]))
        return 0

    lax.fori_loop(0, nblk, group_block, 0)

    for c in range(nblk):
        lanes = slice(c * LANES, (c + 1) * LANES)
        y_ref[c] = jax.nn.gelu(y_ref[c] + d_ref[:, lanes] * hp_ref[c])
        for s_idx in range(SUBLANES):
            o_ref[s_idx * seg:(s_idx + 1) * seg, lanes] = (
                y_ref[c, pl.ds(s_idx, seg, stride=SUBLANES), :].astype(BF16))


def _ssm_core(x, gain, d_skip, wb_r, wb_i, wc_r, wc_i, ab_r, ab_i, ap_r, ap_i):
    s, d = x.shape
    t = min(SSM_ROW_TILE, s)
    nblk, cb, ns = wb_r.shape

    def whole(a):
        return pl.BlockSpec(a.shape, lambda i: (0,) * a.ndim)

    return pl.pallas_call(
        _ssm_kernel,
        grid=(s // t,),
        in_specs=[
            pl.BlockSpec((t, d), lambda i: (i, 0)),
            whole(gain), whole(d_skip),
            whole(wb_r), whole(wb_i), whole(wc_r), whole(wc_i),
            whole(ab_r), whole(ab_i), whole(ap_r), whole(ap_i),
        ],
        out_specs=pl.BlockSpec((t, d), lambda i: (i, 0)),
        out_shape=jax.ShapeDtypeStruct((s, d), BF16),
        scratch_shapes=[
            pltpu.VMEM((nblk, t, LANES), F32),
            pltpu.VMEM((nblk, t, LANES), F32),
            pltpu.VMEM((t, ns), F32),
            pltpu.VMEM((t, ns), F32),
            pltpu.VMEM((nblk, 1, ns), F32),
            pltpu.VMEM((nblk, 1, ns), F32),
        ],
        compiler_params=_params("arbitrary"),
        name="ssm_core",
    )(x, gain, d_skip, wb_r, wb_i, wc_r, wc_i, ab_r, ab_i, ap_r, ap_i)


def _block_diag(w, groups_per_block):
    g, r, c = w.shape
    nblk = g // groups_per_block
    w = w.reshape(nblk, groups_per_block, r, 1, c)
    eye = jnp.eye(groups_per_block, dtype=w.dtype).reshape(1, groups_per_block, 1, groups_per_block, 1)
    return (w * eye).reshape(nblk, groups_per_block * r, groups_per_block * c)


def _s5_mixer(x, gain, a_re, a_im, log_dt, b_re, b_im, c_re, c_im, d_skip, w_glu, layer):
    s, d = x.shape
    g, n = a_re.shape
    gpb = SSM_GROUPS_PER_BLOCK
    seg = min(SSM_ROW_TILE, s) // SUBLANES
    n_squarings = int(math.log2(seg))
    assert 2 ** n_squarings == seg
    bt_re, bt_im = b_re.transpose(2, 0, 1), b_im.transpose(2, 0, 1)
    ab_r, ab_i, ap_r, ap_i, bb_r, bb_i = _ssm_prep(a_re, a_im, log_dt.reshape(g, 1), bt_re, bt_im,
                                                   n_squarings)
    wb_r = _block_diag(bb_r.transpose(1, 0, 2), gpb).astype(BF16)
    wb_i = _block_diag(bb_i.transpose(1, 0, 2), gpb).astype(BF16)
    wc_r = _block_diag(c_re.transpose(0, 2, 1), gpb).astype(BF16)
    wc_i = _block_diag(-c_im.transpose(0, 2, 1), gpb).astype(BF16)
    flat = lambda a: a.reshape(g // gpb, 1, gpb * n)
    z = _ssm_core(x, gain, d_skip.reshape(1, d), wb_r, wb_i, wc_r, wc_i,
                  flat(ab_r), flat(ab_i), flat(ap_r), flat(ap_i))
    return _glu_res(z, w_glu, x, layer)


def kernel(x, norm_mix, norm_mlp, mlp_w1, mlp_w2, conv_w_in, conv_w, conv_w_out, pool_w_in, pool_w_group, pool_scale, att_w_qkv, att_q_norm, att_k_norm, att_rel_bias, att_w_out, ssm_a_re, ssm_a_im, ssm_log_dt, ssm_b_re, ssm_b_im, ssm_c_re, ssm_c_im, ssm_d, ssm_w_glu):
    b, s, d = x.shape
    depth = norm_mix.shape[0]
    n_mixers = 4
    outs = []
    for bi in range(b):
        xs = x[bi]
        for i in range(depth):
            kind, j = i % n_mixers, i // n_mixers
            gain = norm_mix[i].reshape(1, d)
            if kind == 0:
                gated = _conv_front(xs, gain, conv_w_in, conv_w[j], j)
                xs = _proj_res(gated, conv_w_out, xs, j)
            elif kind == 1:
                xs = _pool_mixer(xs, gain, pool_w_in, pool_w_group, pool_scale[j].reshape(1, d), j)
            elif kind == 2:
                n_heads = d // ATT_HEAD_DIM
                head_gains = jnp.concatenate([jnp.tile(att_q_norm[j], n_heads),
                                              jnp.tile(att_k_norm[j], n_heads)]).reshape(1, 2 * d)
                qkv = _qkv_proj(xs, gain, att_w_qkv, head_gains, j)
                att = _attention(qkv, _attention_bias(att_rel_bias[j]), n_heads)
                xs = _proj_res(att, att_w_out, xs, j)
            else:
                xs = _s5_mixer(xs, gain, ssm_a_re[j], ssm_a_im[j], ssm_log_dt[j], ssm_b_re[j],
                               ssm_b_im[j], ssm_c_re[j], ssm_c_im[j], ssm_d[j], ssm_w_glu, j)
            xs = _mlp(xs, norm_mlp[i].reshape(1, d), mlp_w1, mlp_w2, i)
        outs.append(xs)
    return outs[0][None] if b == 1 else jnp.stack(outs)
```

```python
import functools
import math

import jax
import jax.numpy as jnp
from jax import lax
from jax.experimental import pallas as pl
from jax.experimental.pallas import tpu as pltpu

F32 = jnp.float32
BF16 = jnp.bfloat16

RMS_EPS = 1e-6
CHUNK = 64
ATT_HEAD_DIM = 128
ATT_LEFT_CHUNKS = 8
REL_CLIP = 256
MASK_VALUE = -1e30
POOL_WINDOWS = (2, 4, 8, 16)
CONV_WIDTH = 3
SSM_GROUP = 16
SSM_STATE = 64

SUBLANES = 8
LANES = 128
VMEM_LIMIT_BYTES = 56 * 1024 * 1024

ROW_TILE = 1024
COL_TILE = 512
PROJ_ROW_TILE = 512
PROJ_COL_TILE = 2048
GLU_COL_TILE = 1024
FF_TILE = 512
MLP_OUT_TILE = 256
MLP_K_TILE = 4096
ATT_QBLOCK = 256
ATT_HEADS_PER_STEP = 4
SSM_ROW_TILE = 512
SSM_GROUPS_PER_BLOCK = 8
HALO = 16


def _params(*semantics):
    return pltpu.CompilerParams(dimension_semantics=semantics, vmem_limit_bytes=VMEM_LIMIT_BYTES)


def _rms_norm(x, gain):
    return x * lax.rsqrt(jnp.mean(x * x, axis=-1, keepdims=True) + RMS_EPS) * gain


def _dot(a, b):
    return jnp.dot(a, b, preferred_element_type=F32)


def _wdot(a, w_ref):
    return _dot(a, w_ref[...].astype(BF16))


def _mlp_kernel(x_ref, g_ref, w1_ref, w2_ref, xr_ref, o_ref, h_ref, a_ref, *, n_up, n_k):
    t = pl.program_id(1)
    tf = w1_ref.shape[1]
    tk = w2_ref.shape[0]

    @pl.when(t == 0)
    def _():
        h_ref[...] = _rms_norm(x_ref[...], g_ref[...]).astype(BF16)

    @pl.when(t < n_up)
    def _():
        a = jnp.maximum(_wdot(h_ref[...], w1_ref), 0.0)
        a_ref[:, pl.ds(pl.multiple_of(t * tf, tf), tf)] = (a * a).astype(BF16)

    @pl.when(t >= n_up)
    def _():
        kh = (t - n_up) % n_k
        part = _wdot(a_ref[:, pl.ds(pl.multiple_of(kh * tk, tk), tk)], w2_ref)

        @pl.when(kh == 0)
        def _():
            o_ref[...] = xr_ref[...] + part

        @pl.when(kh > 0)
        def _():
            o_ref[...] += part


def _mlp(x, gain, w1, w2, layer):
    s, d = x.shape
    dff = w1.shape[2]
    tm, tf, tn, tk = min(ROW_TILE, s), FF_TILE, MLP_OUT_TILE, MLP_K_TILE
    n_up, n_k = dff // tf, dff // tk
    n_steps = n_up + (d // tn) * n_k

    def down(t):
        return jnp.maximum(t - n_up, 0)

    return pl.pallas_call(
        functools.partial(_mlp_kernel, n_up=n_up, n_k=n_k),
        grid=(s // tm, n_steps),
        in_specs=[
            pl.BlockSpec((tm, d), lambda i, t: (i, 0), pipeline_mode=pl.Buffered(1)),
            pl.BlockSpec((1, d), lambda i, t: (0, 0)),
            pl.BlockSpec((None, d, tf), lambda i, t: (layer, 0, jnp.minimum(t, n_up - 1))),
            pl.BlockSpec((None, tk, tn), lambda i, t: (layer, down(t) % n_k, down(t) // n_k)),
            pl.BlockSpec((tm, tn), lambda i, t: (i, down(t) // n_k)),
        ],
        out_specs=pl.BlockSpec((tm, tn), lambda i, t: (i, down(t) // n_k)),
        out_shape=jax.ShapeDtypeStruct((s, d), F32),
        scratch_shapes=[pltpu.VMEM((tm, d), BF16), pltpu.VMEM((tm, dff), BF16)],
        compiler_params=_params("parallel", "arbitrary"),
        name="mlp",
    )(x, gain, w1, w2, x)


def _proj_res_kernel(a_ref, w_ref, x_ref, o_ref, wb_ref):
    @pl.when(pl.program_id(1) == 0)
    def _():
        wb_ref[...] = w_ref[...].astype(BF16)

    o_ref[...] = x_ref[...] + _dot(a_ref[...], wb_ref[...])


def _proj_res(a, w, x, layer):
    s, k = a.shape
    n = w.shape[2]
    tm, tn = min(PROJ_ROW_TILE, s), min(PROJ_COL_TILE, n)
    return pl.pallas_call(
        _proj_res_kernel,
        grid=(n // tn, s // tm),
        in_specs=[
            pl.BlockSpec((tm, k), lambda j, i: (i, 0)),
            pl.BlockSpec((None, k, tn), lambda j, i: (layer, 0, j), pipeline_mode=pl.Buffered(1)),
            pl.BlockSpec((tm, tn), lambda j, i: (i, j)),
        ],
        out_specs=pl.BlockSpec((tm, tn), lambda j, i: (i, j)),
        out_shape=jax.ShapeDtypeStruct((s, n), F32),
        scratch_shapes=[pltpu.VMEM((k, tn), BF16)],
        compiler_params=_params("arbitrary", "arbitrary"),
        name="proj_res",
    )(a, w, x)


def _glu_res_kernel(a_ref, wv_ref, wg_ref, x_ref, o_ref, wvb_ref, wgb_ref):
    @pl.when(pl.program_id(1) == 0)
    def _():
        wvb_ref[...] = wv_ref[...].astype(BF16)
        wgb_ref[...] = wg_ref[...].astype(BF16)

    a = a_ref[...]
    val = _dot(a, wvb_ref[...])
    gate = _dot(a, wgb_ref[...])
    o_ref[...] = x_ref[...] + val * jax.nn.sigmoid(gate)


def _glu_res(a, w_glu, x, layer):
    s, k = a.shape
    n = w_glu.shape[2] // 2
    tm, tn = min(PROJ_ROW_TILE, s), min(GLU_COL_TILE, n)
    nj = n // tn
    return pl.pallas_call(
        _glu_res_kernel,
        grid=(nj, s // tm),
        in_specs=[
            pl.BlockSpec((tm, k), lambda j, i: (i, 0)),
            pl.BlockSpec((None, k, tn), lambda j, i: (layer, 0, j), pipeline_mode=pl.Buffered(1)),
            pl.BlockSpec((None, k, tn), lambda j, i: (layer, 0, j + nj), pipeline_mode=pl.Buffered(1)),
            pl.BlockSpec((tm, tn), lambda j, i: (i, j)),
        ],
        out_specs=pl.BlockSpec((tm, tn), lambda j, i: (i, j)),
        out_shape=jax.ShapeDtypeStruct((s, n), F32),
        scratch_shapes=[pltpu.VMEM((k, tn), BF16), pltpu.VMEM((k, tn), BF16)],
        compiler_params=_params("arbitrary", "arbitrary"),
        name="glu_res",
    )(a, w_glu, w_glu, x)


def _conv_front_kernel(x_ref, g_ref, wb_ref, wc_ref, wv_ref, cw_ref, o_ref, h_ref, ext_ref, carry_ref):
    i, j = pl.program_id(0), pl.program_id(1)
    tm = o_ref.shape[0]

    @pl.when(j == 0)
    def _():
        h_ref[...] = _rms_norm(x_ref[...], g_ref[...]).astype(BF16)

    @pl.when(i == 0)
    def _():
        carry_ref[j] = jnp.zeros(carry_ref.shape[1:], F32)

    h = h_ref[...]
    u = _wdot(h, wc_ref) * _wdot(h, wv_ref)
    ext_ref[0:HALO, :] = carry_ref[j]
    ext_ref[HALO:, :] = u
    carry_ref[j] = u[tm - HALO:, :]
    conv = cw_ref[CONV_WIDTH - 1:CONV_WIDTH, :] * u
    for lag in range(1, CONV_WIDTH):
        tap = CONV_WIDTH - 1 - lag
        conv += cw_ref[tap:tap + 1, :] * ext_ref[pl.ds(HALO - lag, tm), :]
    o_ref[...] = (_wdot(h, wb_ref) * conv).astype(BF16)


def _conv_front(x, gain, w_in, conv_w, layer):
    s, d = x.shape
    tm, tn = min(ROW_TILE, s), COL_TILE
    nj = d // tn
    return pl.pallas_call(
        _conv_front_kernel,
        grid=(s // tm, nj),
        in_specs=[
            pl.BlockSpec((tm, d), lambda i, j: (i, 0)),
            pl.BlockSpec((1, d), lambda i, j: (0, 0)),
            pl.BlockSpec((None, d, tn), lambda i, j: (layer, 0, j)),
            pl.BlockSpec((None, d, tn), lambda i, j: (layer, 0, j + nj)),
            pl.BlockSpec((None, d, tn), lambda i, j: (layer, 0, j + 2 * nj)),
            pl.BlockSpec((CONV_WIDTH, tn), lambda i, j: (0, j)),
        ],
        out_specs=pl.BlockSpec((tm, tn), lambda i, j: (i, j)),
        out_shape=jax.ShapeDtypeStruct((s, d), BF16),
        scratch_shapes=[
            pltpu.VMEM((tm, d), BF16),
            pltpu.VMEM((tm + HALO, tn), F32),
            pltpu.VMEM((nj, HALO, tn), F32),
        ],
        compiler_params=_params("arbitrary", "arbitrary"),
        name="conv_front",
    )(x, gain, w_in, w_in, w_in, conv_w)


def _pool_kernel(x_ref, g_ref, win_ref, wg_ref, sc_ref, xr_ref, o_ref, h_ref, ext_ref, carry_ref):
    i, grp = pl.program_id(0), pl.program_id(1)
    tm = o_ref.shape[0]

    @pl.when(grp == 0)
    def _():
        h_ref[...] = _rms_norm(x_ref[...], g_ref[...]).astype(BF16)

    @pl.when(i == 0)
    def _():
        carry_ref[grp] = jnp.zeros(carry_ref.shape[1:], F32)

    u = _wdot(h_ref[...], win_ref)
    ext_ref[0:HALO, :] = carry_ref[grp]
    ext_ref[HALO:, :] = u
    carry_ref[grp] = u[tm - HALO:, :]
    pos = (i * tm + 1 + lax.broadcasted_iota(jnp.int32, (tm, 1), 0)).astype(F32)

    for gi, w in enumerate(POOL_WINDOWS):
        @pl.when(grp == gi)
        def _(w=w):
            acc = u
            for lag in range(1, w):
                acc += ext_ref[pl.ds(HALO - lag, tm), :]
            inv_count = 1.0 / jnp.minimum(pos, float(w))
            pooled = (acc * inv_count - u).astype(BF16)
            o_ref[...] = xr_ref[...] + _dot(pooled, wg_ref[0].astype(BF16)) * sc_ref[...]


def _pool_mixer(x, gain, w_in, w_group, scale, layer):
    s, d = x.shape
    ng, pg = w_group.shape[1], w_group.shape[2]
    tm = min(ROW_TILE, s)
    return pl.pallas_call(
        _pool_kernel,
        grid=(s // tm, ng),
        in_specs=[
            pl.BlockSpec((tm, d), lambda i, g: (i, 0)),
            pl.BlockSpec((1, d), lambda i, g: (0, 0)),
            pl.BlockSpec((None, d, pg), lambda i, g: (layer, 0, g)),
            pl.BlockSpec((None, 1, pg, pg), lambda i, g: (layer, g, 0, 0)),
            pl.BlockSpec((1, pg), lambda i, g: (0, g)),
            pl.BlockSpec((tm, pg), lambda i, g: (i, g)),
        ],
        out_specs=pl.BlockSpec((tm, pg), lambda i, g: (i, g)),
        out_shape=jax.ShapeDtypeStruct((s, d), F32),
        scratch_shapes=[
            pltpu.VMEM((tm, d), BF16),
            pltpu.VMEM((tm + HALO, pg), F32),
            pltpu.VMEM((ng, HALO, pg), F32),
        ],
        compiler_params=_params("arbitrary", "arbitrary"),
        name="pool_mixer",
    )(x, gain, w_in, w_group, scale, x)


def _qkv_kernel(x_ref, g_ref, w_ref, hg_ref, o_ref, h_ref, *, n_norm_tiles):
    j = pl.program_id(1)

    @pl.when(j == 0)
    def _():
        h_ref[...] = _rms_norm(x_ref[...], g_ref[...]).astype(BF16)

    y = _wdot(h_ref[...], w_ref)

    @pl.when(j < n_norm_tiles)
    def _():
        for hd in range(y.shape[1] // ATT_HEAD_DIM):
            sl = slice(hd * ATT_HEAD_DIM, (hd + 1) * ATT_HEAD_DIM)
            o_ref[:, sl] = _rms_norm(y[:, sl], hg_ref[:, sl]).astype(BF16)

    @pl.when(j >= n_norm_tiles)
    def _():
        o_ref[...] = y.astype(BF16)


def _qkv_proj(x, gain, w_qkv, head_gains, layer):
    s, d = x.shape
    n = w_qkv.shape[2]
    tm, tn = min(ROW_TILE, s), COL_TILE
    n_norm_tiles = head_gains.shape[1] // tn
    return pl.pallas_call(
        functools.partial(_qkv_kernel, n_norm_tiles=n_norm_tiles),
        grid=(s // tm, n // tn),
        in_specs=[
            pl.BlockSpec((tm, d), lambda i, j: (i, 0)),
            pl.BlockSpec((1, d), lambda i, j: (0, 0)),
            pl.BlockSpec((None, d, tn), lambda i, j: (layer, 0, j)),
            pl.BlockSpec((1, tn), lambda i, j: (0, jnp.minimum(j, n_norm_tiles - 1))),
        ],
        out_specs=pl.BlockSpec((tm, tn), lambda i, j: (i, j)),
        out_shape=jax.ShapeDtypeStruct((s, n), BF16),
        scratch_shapes=[pltpu.VMEM((tm, d), BF16)],
        compiler_params=_params("parallel", "arbitrary"),
        name="qkv_proj",
    )(x, gain, w_qkv, head_gains)


def _attn_kernel(q_ref, k0_ref, k1_ref, k2_ref, v0_ref, v1_ref, v2_ref, prof_ref, o_ref, bias_ref):
    b = pl.program_id(1)
    qb = q_ref.shape[0]
    scale = ATT_HEAD_DIM ** -0.5
    n_local = q_ref.shape[1] // ATT_HEAD_DIM

    @pl.when(b == 0)
    def _():
        q_idx = lax.broadcasted_iota(jnp.int32, (qb, 3 * qb), 0) + 2 * qb
        k_idx = lax.broadcasted_iota(jnp.int32, (qb, 3 * qb), 1)
        chunk_start = (q_idx // CHUNK) * CHUNK
        in_band = (k_idx >= chunk_start - ATT_LEFT_CHUNKS * CHUNK) & (k_idx < chunk_start + CHUNK)
        for hh in range(n_local):
            profile = jnp.broadcast_to(prof_ref[hh], (qb, prof_ref.shape[2]))
            table = pltpu.roll(profile, 0, 1, stride=1, stride_axis=0)[:, :3 * qb]
            bias_ref[hh] = jnp.where(in_band, table, MASK_VALUE)

    for hh in range(n_local):
        cols = slice(hh * ATT_HEAD_DIM, (hh + 1) * ATT_HEAD_DIM)
        q = q_ref[:, cols]
        scores = []
        for c, k_ref in enumerate((k0_ref, k1_ref, k2_ref)):
            s_c = lax.dot_general(q, k_ref[:, cols], (((1,), (1,)), ((), ())),
                                  preferred_element_type=F32)
            s_c = s_c * scale + bias_ref[hh, :, c * qb:(c + 1) * qb]
            if c < 2:
                s_c = jnp.where(b - 2 + c >= 0, s_c, MASK_VALUE)
            scores.append(s_c)
        m = jnp.maximum(jnp.maximum(jnp.max(scores[0], axis=-1, keepdims=True),
                                    jnp.max(scores[1], axis=-1, keepdims=True)),
                        jnp.max(scores[2], axis=-1, keepdims=True))
        acc = jnp.zeros((qb, ATT_HEAD_DIM), F32)
        denom = jnp.zeros((qb, 1), F32)
        for s_c, v_ref in zip(scores, (v0_ref, v1_ref, v2_ref)):
            p = jnp.exp(s_c - m)
            denom += jnp.sum(p, axis=-1, keepdims=True)
            acc += _dot(p.astype(BF16), v_ref[:, cols])
        o_ref[:, cols] = (acc / denom).astype(BF16)


def _attention(qkv, profile, n_heads):
    s = qkv.shape[0]
    qb = ATT_QBLOCK
    nb = s // qb
    hps = ATT_HEADS_PER_STEP
    width = hps * ATT_HEAD_DIM
    ngroups = n_heads // hps

    def kv_spec(offset, back):
        return pl.BlockSpec((qb, width), lambda h, b: (jnp.maximum(b - back, 0), offset + h))

    return pl.pallas_call(
        _attn_kernel,
        grid=(ngroups, nb),
        in_specs=[
            pl.BlockSpec((qb, width), lambda h, b: (b, h)),
            kv_spec(ngroups, 2), kv_spec(ngroups, 1), kv_spec(ngroups, 0),
            kv_spec(2 * ngroups, 2), kv_spec(2 * ngroups, 1), kv_spec(2 * ngroups, 0),
            pl.BlockSpec((hps, 1, 4 * qb), lambda h, b: (h, 0, 0)),
        ],
        out_specs=pl.BlockSpec((qb, width), lambda h, b: (b, h)),
        out_shape=jax.ShapeDtypeStruct((s, n_heads * ATT_HEAD_DIM), BF16),
        scratch_shapes=[pltpu.VMEM((hps, qb, 3 * qb), F32)],
        compiler_params=_params("arbitrary", "arbitrary"),
        name="chunk_attention",
    )(qkv, qkv, qkv, qkv, qkv, qkv, qkv, profile)


def _attention_profile(rel_bias):
    qb = ATT_QBLOCK
    n_heads = rel_bias.shape[0]
    assert REL_CLIP <= 2 * qb and qb <= REL_CLIP + 1 and 2 * REL_CLIP + 1 == rel_bias.shape[1]
    n_flat = 2 * qb - REL_CLIP + 1
    top = rel_bias[:, 2 * REL_CLIP:]
    profile = jnp.concatenate([
        jnp.broadcast_to(top, (n_heads, n_flat)),
        jnp.flip(rel_bias[:, REL_CLIP - qb + 1:2 * REL_CLIP], axis=1),
        jnp.broadcast_to(top, (n_heads, qb)),
    ], axis=1)
    return profile.reshape(n_heads, 1, 4 * qb).astype(F32)


def _ssm_prep_kernel(are_ref, aim_ref, ldt_ref, btr_ref, bti_ref,
                     abr_ref, abi_ref, apr_ref, api_ref, bbr_ref, bbi_ref, *, n_squarings):
    lam_r, lam_i = are_ref[...], aim_ref[...]
    dt = jnp.exp(ldt_ref[...])
    mag = jnp.exp(lam_r * dt)
    ab_r, ab_i = mag * jnp.cos(lam_i * dt), mag * jnp.sin(lam_i * dt)
    abr_ref[...] = ab_r
    abi_ref[...] = ab_i
    p_r, p_i = ab_r, ab_i
    for _ in range(n_squarings):
        p_r, p_i = p_r * p_r - p_i * p_i, 2.0 * p_r * p_i
    apr_ref[...] = p_r
    api_ref[...] = p_i
    num_r, num_i = ab_r - 1.0, ab_i
    den = lam_r * lam_r + lam_i * lam_i
    co_r = (num_r * lam_r + num_i * lam_i) / den
    co_i = (num_i * lam_r - num_r * lam_i) / den
    for c in range(btr_ref.shape[0]):
        b_r, b_i = btr_ref[c], bti_ref[c]
        bbr_ref[c] = co_r * b_r - co_i * b_i
        bbi_ref[c] = co_r * b_i + co_i * b_r


def _ssm_prep(a_re, a_im, log_dt, bt_re, bt_im, n_squarings):
    g, n = a_re.shape
    gn = jax.ShapeDtypeStruct((g, n), F32)
    cgn = jax.ShapeDtypeStruct(bt_re.shape, F32)
    return pl.pallas_call(
        functools.partial(_ssm_prep_kernel, n_squarings=n_squarings),
        out_shape=(gn, gn, gn, gn, cgn, cgn),
        name="ssm_prep",
    )(a_re, a_im, log_dt, bt_re, bt_im)


def _ssm_kernel(x_ref, g_ref, d_ref, wbr_ref, wbi_ref, wcr_ref, wci_ref,
                ar_ref, ai_ref, pr_ref, pi_ref, o_ref,
                hp_ref, y_ref, sr_ref, si_ref, cr_ref, ci_ref):
    i = pl.program_id(0)
    t = x_ref.shape[0]
    seg = t // SUBLANES
    nblk, cb, ns = wbr_ref.shape
    assert cb == LANES and hp_ref.shape[0] == nblk

    h = _rms_norm(x_ref[...], g_ref[...])
    for c in range(nblk):
        for s_idx in range(SUBLANES):
            hp_ref[c, pl.ds(s_idx, seg, stride=SUBLANES), :] = (
                h[s_idx * seg:(s_idx + 1) * seg, c * LANES:(c + 1) * LANES])

    @pl.when(i == 0)
    def _():
        cr_ref[...] = jnp.zeros(cr_ref.shape, F32)
        ci_ref[...] = jnp.zeros(ci_ref.shape, F32)

    zeros = jnp.zeros((SUBLANES, ns), F32)
    sub = lax.broadcasted_iota(jnp.int32, (SUBLANES, ns), 0)

    def group_block(gb, _):
        hb = hp_ref[gb].astype(BF16)
        sr_ref[...] = _dot(hb, wbr_ref[gb])
        si_ref[...] = _dot(hb, wbi_ref[gb])
        a_r = jnp.broadcast_to(ar_ref[gb], (SUBLANES, ns))
        a_i = jnp.broadcast_to(ai_ref[gb], (SUBLANES, ns))

        def step(j, carry, store):
            x_r, x_i = carry
            rows = pl.ds(pl.multiple_of(j * SUBLANES, SUBLANES), SUBLANES)
            n_r = a_r * x_r - a_i * x_i + sr_ref[rows, :]
            n_i = a_r * x_i + a_i * x_r + si_ref[rows, :]
            if store:
                sr_ref[rows, :] = n_r
                si_ref[rows, :] = n_i
            return n_r, n_i

        f_r, f_i = lax.fori_loop(0, seg, functools.partial(step, store=False), (zeros, zeros),
                                 unroll=4)
        p_r, p_i = pr_ref[gb], pi_ref[gb]
        row_r, row_i = cr_ref[gb], ci_ref[gb]
        init_r, init_i = zeros, zeros
        for s_idx in range(SUBLANES):
            init_r = jnp.where(sub == s_idx, jnp.broadcast_to(row_r, (SUBLANES, ns)), init_r)
            init_i = jnp.where(sub == s_idx, jnp.broadcast_to(row_i, (SUBLANES, ns)), init_i)
            if s_idx + 1 < SUBLANES:
                row_r, row_i = (p_r * row_r - p_i * row_i + f_r[s_idx:s_idx + 1, :],
                                p_r * row_i + p_i * row_r + f_i[s_idx:s_idx + 1, :])
        l_r, l_i = lax.fori_loop(0, seg, functools.partial(step, store=True), (init_r, init_i),
                                 unroll=4)
        cr_ref[gb] = l_r[SUBLANES - 1:SUBLANES, :]
        ci_ref[gb] = l_i[SUBLANES - 1:SUBLANES, :]
        y_ref[gb] = (_dot(sr_ref[...].astype(BF16), wcr_ref[gb])
                     + _dot(si_ref[...].astype(BF16), wci_ref[gb]))
        return 0

    lax.fori_loop(0, nblk, group_block, 0)

    for c in range(nblk):
        lanes = slice(c * LANES, (c + 1) * LANES)
        y_ref[c] = jax.nn.gelu(y_ref[c] + d_ref[:, lanes] * hp_ref[c])
        for s_idx in range(SUBLANES):
            o_ref[s_idx * seg:(s_idx + 1) * seg, lanes] = (
                y_ref[c, pl.ds(s_idx, seg, stride=SUBLANES), :].astype(BF16))


def _ssm_core(x, gain, d_skip, wb_r, wb_i, wc_r, wc_i, ab_r, ab_i, ap_r, ap_i):
    s, d = x.shape
    t = min(SSM_ROW_TILE, s)
    nblk, cb, ns = wb_r.shape

    def whole(a):
        return pl.BlockSpec(a.shape, lambda i: (0,) * a.ndim)

    return pl.pallas_call(
        _ssm_kernel,
        grid=(s // t,),
        in_specs=[
            pl.BlockSpec((t, d), lambda i: (i, 0)),
            whole(gain), whole(d_skip),
            whole(wb_r), whole(wb_i), whole(wc_r), whole(wc_i),
            whole(ab_r), whole(ab_i), whole(ap_r), whole(ap_i),
        ],
        out_specs=pl.BlockSpec((t, d), lambda i: (i, 0)),
        out_shape=jax.ShapeDtypeStruct((s, d), BF16),
        scratch_shapes=[
            pltpu.VMEM((nblk, t, LANES), F32),
            pltpu.VMEM((nblk, t, LANES), F32),
            pltpu.VMEM((t, ns), F32),
            pltpu.VMEM((t, ns), F32),
            pltpu.VMEM((nblk, 1, ns), F32),
            pltpu.VMEM((nblk, 1, ns), F32),
        ],
        compiler_params=_params("arbitrary"),
        name="ssm_core",
    )(x, gain, d_skip, wb_r, wb_i, wc_r, wc_i, ab_r, ab_i, ap_r, ap_i)


def _block_diag(w, groups_per_block):
    g, r, c = w.shape
    nblk = g // groups_per_block
    tiled = jnp.tile(w.reshape(nblk, groups_per_block * r, c), (1, 1, groups_per_block))
    row_group = lax.broadcasted_iota(jnp.int32, tiled.shape, 1) // r
    col_group = lax.broadcasted_iota(jnp.int32, tiled.shape, 2) // c
    return jnp.where(row_group == col_group, tiled, jnp.zeros_like(tiled))


def _s5_mixer(x, gain, a_re, a_im, log_dt, b_re, b_im, c_re, c_im, d_skip, w_glu, layer):
    s, d = x.shape
    g, n = a_re.shape
    gpb = SSM_GROUPS_PER_BLOCK
    seg = min(SSM_ROW_TILE, s) // SUBLANES
    n_squarings = int(math.log2(seg))
    assert 2 ** n_squarings == seg
    bt_re, bt_im = b_re.transpose(2, 0, 1), b_im.transpose(2, 0, 1)
    ab_r, ab_i, ap_r, ap_i, bb_r, bb_i = _ssm_prep(a_re, a_im, log_dt.reshape(g, 1), bt_re, bt_im,
                                                   n_squarings)
    wb_r = _block_diag(bb_r.transpose(1, 0, 2), gpb).astype(BF16)
    wb_i = _block_diag(bb_i.transpose(1, 0, 2), gpb).astype(BF16)
    wc_r = _block_diag(c_re.transpose(0, 2, 1), gpb).astype(BF16)
    wc_i = _block_diag(-c_im.transpose(0, 2, 1), gpb).astype(BF16)
    flat = lambda a: a.reshape(g // gpb, 1, gpb * n)
    z = _ssm_core(x, gain, d_skip.reshape(1, d), wb_r, wb_i, wc_r, wc_i,
                  flat(ab_r), flat(ab_i), flat(ap_r), flat(ap_i))
    return _glu_res(z, w_glu, x, layer)


def kernel(x, norm_mix, norm_mlp, mlp_w1, mlp_w2, conv_w_in, conv_w, conv_w_out, pool_w_in, pool_w_group, pool_scale, att_w_qkv, att_q_norm, att_k_norm, att_rel_bias, att_w_out, ssm_a_re, ssm_a_im, ssm_log_dt, ssm_b_re, ssm_b_im, ssm_c_re, ssm_c_im, ssm_d, ssm_w_glu):
    b, s, d = x.shape
    depth = norm_mix.shape[0]
    n_mixers = 4
    outs = []
    for bi in range(b):
        xs = x[bi]
        for i in range(depth):
            kind, j = i % n_mixers, i // n_mixers
            gain = norm_mix[i].reshape(1, d)
            if kind == 0:
                gated = _conv_front(xs, gain, conv_w_in, conv_w[j], j)
                xs = _proj_res(gated, conv_w_out, xs, j)
            elif kind == 1:
                xs = _pool_mixer(xs, gain, pool_w_in, pool_w_group, pool_scale[j].reshape(1, d), j)
            elif kind == 2:
                n_heads = d // ATT_HEAD_DIM
                head_gains = jnp.concatenate([jnp.tile(att_q_norm[j], n_heads),
                                              jnp.tile(att_k_norm[j], n_heads)]).reshape(1, 2 * d)
                qkv = _qkv_proj(xs, gain, att_w_qkv, head_gains, j)
                att = _attention(qkv, _attention_profile(att_rel_bias[j]), n_heads)
                xs = _proj_res(att, att_w_out, xs, j)
            else:
                xs = _s5_mixer(xs, gain, ssm_a_re[j], ssm_a_im[j], ssm_log_dt[j], ssm_b_re[j],
                               ssm_b_im[j], ssm_c_re[j], ssm_c_im[j], ssm_d[j], ssm_w_glu, j)
            xs = _mlp(xs, norm_mlp[i].reshape(1, d), mlp_w1, mlp_w2, i)
        outs.append(xs)
    return outs[0][None] if b == 1 else jnp.stack(outs)
```

```python
import functools
import math

import jax
import jax.numpy as jnp
from jax import lax
from jax.experimental import pallas as pl
from jax.experimental.pallas import tpu as pltpu

F32 = jnp.float32
BF16 = jnp.bfloat16

RMS_EPS = 1e-6
CHUNK = 64
ATT_HEAD_DIM = 128
ATT_LEFT_CHUNKS = 8
REL_CLIP = 256
MASK_VALUE = -1e30
POOL_WINDOWS = (2, 4, 8, 16)
CONV_WIDTH = 3
SSM_GROUP = 16
SSM_STATE = 64

SUBLANES = 8
LANES = 128
VMEM_LIMIT_BYTES = 56 * 1024 * 1024

ROW_TILE = 1024
COL_TILE = 512
PROJ_ROW_TILE = 512
PROJ_COL_TILE = 2048
GLU_COL_TILE = 1024
FF_TILE = 512
MLP_OUT_TILE = 512
MLP_K_TILE = 2048
ATT_QBLOCK = 256
ATT_HEADS_PER_STEP = 4
SSM_ROW_TILE = 512
SSM_GROUPS_PER_BLOCK = 8
HALO = 16


def _params(*semantics):
    return pltpu.CompilerParams(dimension_semantics=semantics, vmem_limit_bytes=VMEM_LIMIT_BYTES)


def _rms_norm(x, gain):
    return x * lax.rsqrt(jnp.mean(x * x, axis=-1, keepdims=True) + RMS_EPS) * gain


def _dot(a, b):
    return jnp.dot(a, b, preferred_element_type=F32)


def _wdot(a, w_ref):
    return _dot(a, w_ref[...].astype(BF16))


def _mlp_kernel(x_ref, g_ref, w1_hbm, w2_hbm, xr_ref, o_ref, h_ref, a_ref, w1_buf, w2_buf, sem1, sem2,
                *, layer, n_rows):
    i, t = pl.program_id(0), pl.program_id(1)
    n_out = pl.num_programs(1) - 1
    tf, tk, tn = w1_buf.shape[2], w2_buf.shape[1], w2_buf.shape[2]
    n_up, n_k = a_ref.shape[1] // tf, a_ref.shape[1] // tk
    assert n_up % 2 == 0 and n_k % 2 == 0

    def w1_copy(f, slot):
        return pltpu.make_async_copy(w1_hbm.at[layer, :, pl.ds(f * tf, tf)], w1_buf.at[slot],
                                     sem1.at[slot])

    def w2_copy(n, kh):
        return pltpu.make_async_copy(w2_hbm.at[layer, pl.ds(kh * tk, tk), pl.ds(n * tn, tn)],
                                     w2_buf.at[kh % 2], sem2.at[kh % 2])

    @pl.when(t == 0)
    def _():
        @pl.when(i == 0)
        def _():
            w1_copy(0, 0).start()

        h_ref[...] = _rms_norm(x_ref[...], g_ref[...]).astype(BF16)

        def up_pair(p, _):
            for slot in (0, 1):
                f = 2 * p + slot
                w1_copy(f, slot).wait()

                @pl.when(f + 1 < n_up)
                def _():
                    w1_copy(f + 1, 1 - slot).start()

                a = jnp.maximum(_dot(h_ref[...], w1_buf[slot].astype(BF16)), 0.0)
                a_ref[:, pl.ds(pl.multiple_of(f * tf, tf), tf)] = (a * a).astype(BF16)
            return 0

        lax.fori_loop(0, n_up // 2, up_pair, 0)
        w2_copy(0, 0).start()
        w2_copy(0, 1).start()

    @pl.when(t > 0)
    def _():
        n = t - 1
        for kh in range(n_k):
            w2_copy(n, kh).wait()
            part = _dot(a_ref[:, kh * tk:(kh + 1) * tk], w2_buf[kh % 2].astype(BF16))
            if kh == 0:
                o_ref[...] = xr_ref[...] + part
            else:
                o_ref[...] += part
            if kh + 2 < n_k:
                w2_copy(n, kh + 2).start()
            else:
                @pl.when(n + 1 < n_out)
                def _(kh=kh):
                    w2_copy(n + 1, kh + 2 - n_k).start()

        @pl.when((n + 1 == n_out) & (i + 1 < n_rows))
        def _():
            w1_copy(0, 0).start()


def _mlp(x, gain, w1, w2, layer):
    s, d = x.shape
    dff = w1.shape[2]
    tm, tf, tn, tk = min(ROW_TILE, s), FF_TILE, MLP_OUT_TILE, MLP_K_TILE
    n_rows, n_out = s // tm, d // tn

    def out_tile(t):
        return jnp.maximum(t - 1, 0)

    return pl.pallas_call(
        functools.partial(_mlp_kernel, layer=layer, n_rows=n_rows),
        grid=(n_rows, 1 + n_out),
        in_specs=[
            pl.BlockSpec((tm, d), lambda i, t: (i, 0), pipeline_mode=pl.Buffered(1)),
            pl.BlockSpec((1, d), lambda i, t: (0, 0)),
            pl.BlockSpec(memory_space=pl.ANY),
            pl.BlockSpec(memory_space=pl.ANY),
            pl.BlockSpec((tm, tn), lambda i, t: (i, out_tile(t))),
        ],
        out_specs=pl.BlockSpec((tm, tn), lambda i, t: (i, out_tile(t))),
        out_shape=jax.ShapeDtypeStruct((s, d), F32),
        scratch_shapes=[
            pltpu.VMEM((tm, d), BF16),
            pltpu.VMEM((tm, dff), BF16),
            pltpu.VMEM((2, d, tf), F32),
            pltpu.VMEM((2, tk, tn), F32),
            pltpu.SemaphoreType.DMA((2,)),
            pltpu.SemaphoreType.DMA((2,)),
        ],
        compiler_params=_params("arbitrary", "arbitrary"),
        name="mlp",
    )(x, gain, w1, w2, x)


def _proj_res_kernel(a_ref, w_ref, x_ref, o_ref, wb_ref):
    @pl.when(pl.program_id(1) == 0)
    def _():
        wb_ref[...] = w_ref[...].astype(BF16)

    o_ref[...] = x_ref[...] + _dot(a_ref[...], wb_ref[...])


def _proj_res(a, w, x, layer):
    s, k = a.shape
    n = w.shape[2]
    tm, tn = min(PROJ_ROW_TILE, s), min(PROJ_COL_TILE, n)
    return pl.pallas_call(
        _proj_res_kernel,
        grid=(n // tn, s // tm),
        in_specs=[
            pl.BlockSpec((tm, k), lambda j, i: (i, 0)),
            pl.BlockSpec((None, k, tn), lambda j, i: (layer, 0, j), pipeline_mode=pl.Buffered(1)),
            pl.BlockSpec((tm, tn), lambda j, i: (i, j)),
        ],
        out_specs=pl.BlockSpec((tm, tn), lambda j, i: (i, j)),
        out_shape=jax.ShapeDtypeStruct((s, n), F32),
        scratch_shapes=[pltpu.VMEM((k, tn), BF16)],
        compiler_params=_params("arbitrary", "arbitrary"),
        name="proj_res",
    )(a, w, x)


def _glu_res_kernel(a_ref, wv_ref, wg_ref, x_ref, o_ref, wvb_ref, wgb_ref):
    @pl.when(pl.program_id(1) == 0)
    def _():
        wvb_ref[...] = wv_ref[...].astype(BF16)
        wgb_ref[...] = wg_ref[...].astype(BF16)

    a = a_ref[...]
    val = _dot(a, wvb_ref[...])
    gate = _dot(a, wgb_ref[...])
    o_ref[...] = x_ref[...] + val * jax.nn.sigmoid(gate)


def _glu_res(a, w_glu, x, layer):
    s, k = a.shape
    n = w_glu.shape[2] // 2
    tm, tn = min(PROJ_ROW_TILE, s), min(GLU_COL_TILE, n)
    nj = n // tn
    return pl.pallas_call(
        _glu_res_kernel,
        grid=(nj, s // tm),
        in_specs=[
            pl.BlockSpec((tm, k), lambda j, i: (i, 0)),
            pl.BlockSpec((None, k, tn), lambda j, i: (layer, 0, j), pipeline_mode=pl.Buffered(1)),
            pl.BlockSpec((None, k, tn), lambda j, i: (layer, 0, j + nj), pipeline_mode=pl.Buffered(1)),
            pl.BlockSpec((tm, tn), lambda j, i: (i, j)),
        ],
        out_specs=pl.BlockSpec((tm, tn), lambda j, i: (i, j)),
        out_shape=jax.ShapeDtypeStruct((s, n), F32),
        scratch_shapes=[pltpu.VMEM((k, tn), BF16), pltpu.VMEM((k, tn), BF16)],
        compiler_params=_params("arbitrary", "arbitrary"),
        name="glu_res",
    )(a, w_glu, w_glu, x)


def _conv_front_kernel(x_ref, g_ref, wb_ref, wc_ref, wv_ref, cw_ref, o_ref, h_ref, ext_ref, carry_ref):
    i, j = pl.program_id(0), pl.program_id(1)
    tm = o_ref.shape[0]

    @pl.when(j == 0)
    def _():
        h_ref[...] = _rms_norm(x_ref[...], g_ref[...]).astype(BF16)

    @pl.when(i == 0)
    def _():
        carry_ref[j] = jnp.zeros(carry_ref.shape[1:], F32)

    h = h_ref[...]
    u = _wdot(h, wc_ref) * _wdot(h, wv_ref)
    ext_ref[0:HALO, :] = carry_ref[j]
    ext_ref[HALO:, :] = u
    carry_ref[j] = u[tm - HALO:, :]
    conv = cw_ref[CONV_WIDTH - 1:CONV_WIDTH, :] * u
    for lag in range(1, CONV_WIDTH):
        tap = CONV_WIDTH - 1 - lag
        conv += cw_ref[tap:tap + 1, :] * ext_ref[pl.ds(HALO - lag, tm), :]
    o_ref[...] = (_wdot(h, wb_ref) * conv).astype(BF16)


def _conv_front(x, gain, w_in, conv_w, layer):
    s, d = x.shape
    tm, tn = min(ROW_TILE, s), COL_TILE
    nj = d // tn
    return pl.pallas_call(
        _conv_front_kernel,
        grid=(s // tm, nj),
        in_specs=[
            pl.BlockSpec((tm, d), lambda i, j: (i, 0)),
            pl.BlockSpec((1, d), lambda i, j: (0, 0)),
            pl.BlockSpec((None, d, tn), lambda i, j: (layer, 0, j)),
            pl.BlockSpec((None, d, tn), lambda i, j: (layer, 0, j + nj)),
            pl.BlockSpec((None, d, tn), lambda i, j: (layer, 0, j + 2 * nj)),
            pl.BlockSpec((CONV_WIDTH, tn), lambda i, j: (0, j)),
        ],
        out_specs=pl.BlockSpec((tm, tn), lambda i, j: (i, j)),
        out_shape=jax.ShapeDtypeStruct((s, d), BF16),
        scratch_shapes=[
            pltpu.VMEM((tm, d), BF16),
            pltpu.VMEM((tm + HALO, tn), F32),
            pltpu.VMEM((nj, HALO, tn), F32),
        ],
        compiler_params=_params("arbitrary", "arbitrary"),
        name="conv_front",
    )(x, gain, w_in, w_in, w_in, conv_w)


def _pool_kernel(x_ref, g_ref, win_ref, wg_ref, sc_ref, xr_ref, o_ref, h_ref, ext_ref, carry_ref):
    i, grp = pl.program_id(0), pl.program_id(1)
    tm = o_ref.shape[0]

    @pl.when(grp == 0)
    def _():
        h_ref[...] = _rms_norm(x_ref[...], g_ref[...]).astype(BF16)

    @pl.when(i == 0)
    def _():
        carry_ref[grp] = jnp.zeros(carry_ref.shape[1:], F32)

    u = _wdot(h_ref[...], win_ref)
    ext_ref[0:HALO, :] = carry_ref[grp]
    ext_ref[HALO:, :] = u
    carry_ref[grp] = u[tm - HALO:, :]
    pos = (i * tm + 1 + lax.broadcasted_iota(jnp.int32, (tm, 1), 0)).astype(F32)

    for gi, w in enumerate(POOL_WINDOWS):
        @pl.when(grp == gi)
        def _(w=w):
            acc = u
            for lag in range(1, w):
                acc += ext_ref[pl.ds(HALO - lag, tm), :]
            inv_count = 1.0 / jnp.minimum(pos, float(w))
            pooled = (acc * inv_count - u).astype(BF16)
            o_ref[...] = xr_ref[...] + _dot(pooled, wg_ref[0].astype(BF16)) * sc_ref[...]


def _pool_mixer(x, gain, w_in, w_group, scale, layer):
    s, d = x.shape
    ng, pg = w_group.shape[1], w_group.shape[2]
    tm = min(ROW_TILE, s)
    return pl.pallas_call(
        _pool_kernel,
        grid=(s // tm, ng),
        in_specs=[
            pl.BlockSpec((tm, d), lambda i, g: (i, 0)),
            pl.BlockSpec((1, d), lambda i, g: (0, 0)),
            pl.BlockSpec((None, d, pg), lambda i, g: (layer, 0, g)),
            pl.BlockSpec((None, 1, pg, pg), lambda i, g: (layer, g, 0, 0)),
            pl.BlockSpec((1, pg), lambda i, g: (0, g)),
            pl.BlockSpec((tm, pg), lambda i, g: (i, g)),
        ],
        out_specs=pl.BlockSpec((tm, pg), lambda i, g: (i, g)),
        out_shape=jax.ShapeDtypeStruct((s, d), F32),
        scratch_shapes=[
            pltpu.VMEM((tm, d), BF16),
            pltpu.VMEM((tm + HALO, pg), F32),
            pltpu.VMEM((ng, HALO, pg), F32),
        ],
        compiler_params=_params("arbitrary", "arbitrary"),
        name="pool_mixer",
    )(x, gain, w_in, w_group, scale, x)


def _qkv_kernel(x_ref, g_ref, w_ref, hg_ref, o_ref, h_ref, *, n_norm_tiles):
    j = pl.program_id(1)

    @pl.when(j == 0)
    def _():
        h_ref[...] = _rms_norm(x_ref[...], g_ref[...]).astype(BF16)

    y = _wdot(h_ref[...], w_ref)

    @pl.when(j < n_norm_tiles)
    def _():
        for hd in range(y.shape[1] // ATT_HEAD_DIM):
            sl = slice(hd * ATT_HEAD_DIM, (hd + 1) * ATT_HEAD_DIM)
            o_ref[:, sl] = _rms_norm(y[:, sl], hg_ref[:, sl]).astype(BF16)

    @pl.when(j >= n_norm_tiles)
    def _():
        o_ref[...] = y.astype(BF16)


def _qkv_proj(x, gain, w_qkv, head_gains, layer):
    s, d = x.shape
    n = w_qkv.shape[2]
    tm, tn = min(ROW_TILE, s), COL_TILE
    n_norm_tiles = head_gains.shape[1] // tn
    return pl.pallas_call(
        functools.partial(_qkv_kernel, n_norm_tiles=n_norm_tiles),
        grid=(s // tm, n // tn),
        in_specs=[
            pl.BlockSpec((tm, d), lambda i, j: (i, 0)),
            pl.BlockSpec((1, d), lambda i, j: (0, 0)),
            pl.BlockSpec((None, d, tn), lambda i, j: (layer, 0, j)),
            pl.BlockSpec((1, tn), lambda i, j: (0, jnp.minimum(j, n_norm_tiles - 1))),
        ],
        out_specs=pl.BlockSpec((tm, tn), lambda i, j: (i, j)),
        out_shape=jax.ShapeDtypeStruct((s, n), BF16),
        scratch_shapes=[pltpu.VMEM((tm, d), BF16)],
        compiler_params=_params("parallel", "arbitrary"),
        name="qkv_proj",
    )(x, gain, w_qkv, head_gains)


def _attn_kernel(q_ref, k0_ref, k1_ref, k2_ref, v0_ref, v1_ref, v2_ref, prof_ref, o_ref, bias_ref):
    b = pl.program_id(1)
    qb = q_ref.shape[0]
    scale = ATT_HEAD_DIM ** -0.5
    n_local = q_ref.shape[1] // ATT_HEAD_DIM

    @pl.when(b == 0)
    def _():
        q_idx = lax.broadcasted_iota(jnp.int32, (qb, 3 * qb), 0) + 2 * qb
        k_idx = lax.broadcasted_iota(jnp.int32, (qb, 3 * qb), 1)
        chunk_start = (q_idx // CHUNK) * CHUNK
        in_band = (k_idx >= chunk_start - ATT_LEFT_CHUNKS * CHUNK) & (k_idx < chunk_start + CHUNK)
        for hh in range(n_local):
            profile = jnp.broadcast_to(prof_ref[hh], (qb, prof_ref.shape[2]))
            table = pltpu.roll(profile, 0, 1, stride=1, stride_axis=0)[:, :3 * qb]
            bias_ref[hh] = jnp.where(in_band, table, MASK_VALUE)

    for hh in range(n_local):
        cols = slice(hh * ATT_HEAD_DIM, (hh + 1) * ATT_HEAD_DIM)
        q = q_ref[:, cols]
        scores = []
        for c, k_ref in enumerate((k0_ref, k1_ref, k2_ref)):
            s_c = lax.dot_general(q, k_ref[:, cols], (((1,), (1,)), ((), ())),
                                  preferred_element_type=F32)
            s_c = s_c * scale + bias_ref[hh, :, c * qb:(c + 1) * qb]
            if c < 2:
                s_c = jnp.where(b - 2 + c >= 0, s_c, MASK_VALUE)
            scores.append(s_c)
        m = jnp.maximum(jnp.maximum(jnp.max(scores[0], axis=-1, keepdims=True),
                                    jnp.max(scores[1], axis=-1, keepdims=True)),
                        jnp.max(scores[2], axis=-1, keepdims=True))
        acc = jnp.zeros((qb, ATT_HEAD_DIM), F32)
        denom = jnp.zeros((qb, 1), F32)
        for s_c, v_ref in zip(scores, (v0_ref, v1_ref, v2_ref)):
            p = jnp.exp(s_c - m)
            denom += jnp.sum(p, axis=-1, keepdims=True)
            acc += _dot(p.astype(BF16), v_ref[:, cols])
        o_ref[:, cols] = (acc / denom).astype(BF16)


def _attention(qkv, profile, n_heads):
    s = qkv.shape[0]
    qb = ATT_QBLOCK
    nb = s // qb
    hps = ATT_HEADS_PER_STEP
    width = hps * ATT_HEAD_DIM
    ngroups = n_heads // hps

    def kv_spec(offset, back):
        return pl.BlockSpec((qb, width), lambda h, b: (jnp.maximum(b - back, 0), offset + h))

    return pl.pallas_call(
        _attn_kernel,
        grid=(ngroups, nb),
        in_specs=[
            pl.BlockSpec((qb, width), lambda h, b: (b, h)),
            kv_spec(ngroups, 2), kv_spec(ngroups, 1), kv_spec(ngroups, 0),
            kv_spec(2 * ngroups, 2), kv_spec(2 * ngroups, 1), kv_spec(2 * ngroups, 0),
            pl.BlockSpec((hps, 1, 4 * qb), lambda h, b: (h, 0, 0)),
        ],
        out_specs=pl.BlockSpec((qb, width), lambda h, b: (b, h)),
        out_shape=jax.ShapeDtypeStruct((s, n_heads * ATT_HEAD_DIM), BF16),
        scratch_shapes=[pltpu.VMEM((hps, qb, 3 * qb), F32)],
        compiler_params=_params("arbitrary", "arbitrary"),
        name="chunk_attention",
    )(qkv, qkv, qkv, qkv, qkv, qkv, qkv, profile)


def _attention_profile(rel_bias):
    qb = ATT_QBLOCK
    n_heads = rel_bias.shape[0]
    assert REL_CLIP <= 2 * qb and qb <= REL_CLIP + 1 and 2 * REL_CLIP + 1 == rel_bias.shape[1]
    n_flat = 2 * qb - REL_CLIP + 1
    top = rel_bias[:, 2 * REL_CLIP:]
    profile = jnp.concatenate([
        jnp.broadcast_to(top, (n_heads, n_flat)),
        jnp.flip(rel_bias[:, REL_CLIP - qb + 1:2 * REL_CLIP], axis=1),
        jnp.broadcast_to(top, (n_heads, qb)),
    ], axis=1)
    return profile.reshape(n_heads, 1, 4 * qb).astype(F32)


def _ssm_prep_kernel(are_ref, aim_ref, ldt_ref, btr_ref, bti_ref,
                     abr_ref, abi_ref, apr_ref, api_ref, bbr_ref, bbi_ref, *, n_squarings):
    lam_r, lam_i = are_ref[...], aim_ref[...]
    dt = jnp.exp(ldt_ref[...])
    mag = jnp.exp(lam_r * dt)
    ab_r, ab_i = mag * jnp.cos(lam_i * dt), mag * jnp.sin(lam_i * dt)
    abr_ref[...] = ab_r
    abi_ref[...] = ab_i
    p_r, p_i = ab_r, ab_i
    for _ in range(n_squarings):
        p_r, p_i = p_r * p_r - p_i * p_i, 2.0 * p_r * p_i
    apr_ref[...] = p_r
    api_ref[...] = p_i
    num_r, num_i = ab_r - 1.0, ab_i
    den = lam_r * lam_r + lam_i * lam_i
    co_r = (num_r * lam_r + num_i * lam_i) / den
    co_i = (num_i * lam_r - num_r * lam_i) / den
    for c in range(btr_ref.shape[0]):
        b_r, b_i = btr_ref[c], bti_ref[c]
        bbr_ref[c] = co_r * b_r - co_i * b_i
        bbi_ref[c] = co_r * b_i + co_i * b_r


def _ssm_prep(a_re, a_im, log_dt, bt_re, bt_im, n_squarings):
    g, n = a_re.shape
    gn = jax.ShapeDtypeStruct((g, n), F32)
    cgn = jax.ShapeDtypeStruct(bt_re.shape, F32)
    return pl.pallas_call(
        functools.partial(_ssm_prep_kernel, n_squarings=n_squarings),
        out_shape=(gn, gn, gn, gn, cgn, cgn),
        name="ssm_prep",
    )(a_re, a_im, log_dt, bt_re, bt_im)


def _ssm_kernel(x_ref, g_ref, d_ref, wbr_ref, wbi_ref, wcr_ref, wci_ref,
                ar_ref, ai_ref, pr_ref, pi_ref, o_ref,
                hp_ref, y_ref, sr_ref, si_ref, cr_ref, ci_ref):
    i = pl.program_id(0)
    t = x_ref.shape[0]
    seg = t // SUBLANES
    nblk, cb, ns = wbr_ref.shape
    assert cb == LANES and hp_ref.shape[0] == nblk

    h = _rms_norm(x_ref[...], g_ref[...])
    for c in range(nblk):
        for s_idx in range(SUBLANES):
            hp_ref[c, pl.ds(s_idx, seg, stride=SUBLANES), :] = (
                h[s_idx * seg:(s_idx + 1) * seg, c * LANES:(c + 1) * LANES])

    @pl.when(i == 0)
    def _():
        cr_ref[...] = jnp.zeros(cr_ref.shape, F32)
        ci_ref[...] = jnp.zeros(ci_ref.shape, F32)

    zeros = jnp.zeros((SUBLANES, ns), F32)
    sub = lax.broadcasted_iota(jnp.int32, (SUBLANES, ns), 0)

    def group_block(gb, _):
        hb = hp_ref[gb].astype(BF16)
        sr_ref[...] = _dot(hb, wbr_ref[gb])
        si_ref[...] = _dot(hb, wbi_ref[gb])
        a_r = jnp.broadcast_to(ar_ref[gb], (SUBLANES, ns))
        a_i = jnp.broadcast_to(ai_ref[gb], (SUBLANES, ns))

        def step(j, carry, store):
            x_r, x_i = carry
            rows = pl.ds(pl.multiple_of(j * SUBLANES, SUBLANES), SUBLANES)
            n_r = a_r * x_r - a_i * x_i + sr_ref[rows, :]
            n_i = a_r * x_i + a_i * x_r + si_ref[rows, :]
            if store:
                sr_ref[rows, :] = n_r
                si_ref[rows, :] = n_i
            return n_r, n_i

        f_r, f_i = lax.fori_loop(0, seg, functools.partial(step, store=False), (zeros, zeros),
                                 unroll=4)
        p_r, p_i = pr_ref[gb], pi_ref[gb]
        row_r, row_i = cr_ref[gb], ci_ref[gb]
        init_r, init_i = zeros, zeros
        for s_idx in range(SUBLANES):
            init_r = jnp.where(sub == s_idx, jnp.broadcast_to(row_r, (SUBLANES, ns)), init_r)
            init_i = jnp.where(sub == s_idx, jnp.broadcast_to(row_i, (SUBLANES, ns)), init_i)
            if s_idx + 1 < SUBLANES:
                row_r, row_i = (p_r * row_r - p_i * row_i + f_r[s_idx:s_idx + 1, :],
                                p_r * row_i + p_i * row_r + f_i[s_idx:s_idx + 1, :])
        l_r, l_i = lax.fori_loop(0, seg, functools.partial(step, store=True), (init_r, init_i),
                                 unroll=4)
        cr_ref[gb] = l_r[SUBLANES - 1:SUBLANES, :]
        ci_ref[gb] = l_i[SUBLANES - 1:SUBLANES, :]
        y_ref[gb] = (_dot(sr_ref[...].astype(BF16), wcr_ref[gb])
                     + _dot(si_ref[...].astype(BF16), wci_ref[gb]))
        return 0

    lax.fori_loop(0, nblk, group_block, 0)

    for c in range(nblk):
        lanes = slice(c * LANES, (c + 1) * LANES)
        y_ref[c] = jax.nn.gelu(y_ref[c] + d_ref[:, lanes] * hp_ref[c])
        for s_idx in range(SUBLANES):
            o_ref[s_idx * seg:(s_idx + 1) * seg, lanes] = (
                y_ref[c, pl.ds(s_idx, seg, stride=SUBLANES), :].astype(BF16))


def _ssm_core(x, gain, d_skip, wb_r, wb_i, wc_r, wc_i, ab_r, ab_i, ap_r, ap_i):
    s, d = x.shape
    t = min(SSM_ROW_TILE, s)
    nblk, cb, ns = wb_r.shape

    def whole(a):
        return pl.BlockSpec(a.shape, lambda i: (0,) * a.ndim)

    return pl.pallas_call(
        _ssm_kernel,
        grid=(s // t,),
        in_specs=[
            pl.BlockSpec((t, d), lambda i: (i, 0)),
            whole(gain), whole(d_skip),
            whole(wb_r), whole(wb_i), whole(wc_r), whole(wc_i),
            whole(ab_r), whole(ab_i), whole(ap_r), whole(ap_i),
        ],
        out_specs=pl.BlockSpec((t, d), lambda i: (i, 0)),
        out_shape=jax.ShapeDtypeStruct((s, d), BF16),
        scratch_shapes=[
            pltpu.VMEM((nblk, t, LANES), F32),
            pltpu.VMEM((nblk, t, LANES), F32),
            pltpu.VMEM((t, ns), F32),
            pltpu.VMEM((t, ns), F32),
            pltpu.VMEM((nblk, 1, ns), F32),
            pltpu.VMEM((nblk, 1, ns), F32),
        ],
        compiler_params=_params("arbitrary"),
        name="ssm_core",
    )(x, gain, d_skip, wb_r, wb_i, wc_r, wc_i, ab_r, ab_i, ap_r, ap_i)


def _block_diag(w, groups_per_block):
    g, r, c = w.shape
    nblk = g // groups_per_block
    tiled = jnp.tile(w.reshape(nblk, groups_per_block * r, c), (1, 1, groups_per_block))
    row_group = lax.broadcasted_iota(jnp.int32, tiled.shape, 1) // r
    col_group = lax.broadcasted_iota(jnp.int32, tiled.shape, 2) // c
    return jnp.where(row_group == col_group, tiled, jnp.zeros_like(tiled))


def _s5_mixer(x, gain, a_re, a_im, log_dt, b_re, b_im, c_re, c_im, d_skip, w_glu, layer):
    s, d = x.shape
    g, n = a_re.shape
    gpb = SSM_GROUPS_PER_BLOCK
    seg = min(SSM_ROW_TILE, s) // SUBLANES
    n_squarings = int(math.log2(seg))
    assert 2 ** n_squarings == seg
    bt_re, bt_im = b_re.transpose(2, 0, 1), b_im.transpose(2, 0, 1)
    ab_r, ab_i, ap_r, ap_i, bb_r, bb_i = _ssm_prep(a_re, a_im, log_dt.reshape(g, 1), bt_re, bt_im,
                                                   n_squarings)
    wb_r = _block_diag(bb_r.transpose(1, 0, 2), gpb).astype(BF16)
    wb_i = _block_diag(bb_i.transpose(1, 0, 2), gpb).astype(BF16)
    wc_r = _block_diag(c_re.transpose(0, 2, 1), gpb).astype(BF16)
    wc_i = _block_diag(-c_im.transpose(0, 2, 1), gpb).astype(BF16)
    flat = lambda a: a.reshape(g // gpb, 1, gpb * n)
    z = _ssm_core(x, gain, d_skip.reshape(1, d), wb_r, wb_i, wc_r, wc_i,
                  flat(ab_r), flat(ab_i), flat(ap_r), flat(ap_i))
    return _glu_res(z, w_glu, x, layer)


def kernel(x, norm_mix, norm_mlp, mlp_w1, mlp_w2, conv_w_in, conv_w, conv_w_out, pool_w_in, pool_w_group, pool_scale, att_w_qkv, att_q_norm, att_k_norm, att_rel_bias, att_w_out, ssm_a_re, ssm_a_im, ssm_log_dt, ssm_b_re, ssm_b_im, ssm_c_re, ssm_c_im, ssm_d, ssm_w_glu):
    b, s, d = x.shape
    depth = norm_mix.shape[0]
    n_mixers = 4
    outs = []
    for bi in range(b):
        xs = x[bi]
        for i in range(depth):
            kind, j = i % n_mixers, i // n_mixers
            gain = norm_mix[i].reshape(1, d)
            if kind == 0:
                gated = _conv_front(xs, gain, conv_w_in, conv_w[j], j)
                xs = _proj_res(gated, conv_w_out, xs, j)
            elif kind == 1:
                xs = _pool_mixer(xs, gain, pool_w_in, pool_w_group, pool_scale[j].reshape(1, d), j)
            elif kind == 2:
                n_heads = d // ATT_HEAD_DIM
                head_gains = jnp.concatenate([jnp.tile(att_q_norm[j], n_heads),
                                              jnp.tile(att_k_norm[j], n_heads)]).reshape(1, 2 * d)
                qkv = _qkv_proj(xs, gain, att_w_qkv, head_gains, j)
                att = _attention(qkv, _attention_profile(att_rel_bias[j]), n_heads)
                xs = _proj_res(att, att_w_out, xs, j)
            else:
                xs = _s5_mixer(xs, gain, ssm_a_re[j], ssm_a_im[j], ssm_log_dt[j], ssm_b_re[j],
                               ssm_b_im[j], ssm_c_re[j], ssm_c_im[j], ssm_d[j], ssm_w_glu, j)
            xs = _mlp(xs, norm_mlp[i].reshape(1, d), mlp_w1, mlp_w2, i)
        outs.append(xs)
    return outs[0][None] if b == 1 else jnp.stack(outs)
```

```python
import functools
import math

import jax
import jax.numpy as jnp
from jax import lax
from jax.experimental import pallas as pl
from jax.experimental.pallas import tpu as pltpu

F32 = jnp.float32
BF16 = jnp.bfloat16

RMS_EPS = 1e-6
CHUNK = 64
ATT_HEAD_DIM = 128
ATT_LEFT_CHUNKS = 8
REL_CLIP = 256
MASK_VALUE = -1e30
POOL_WINDOWS = (2, 4, 8, 16)
CONV_WIDTH = 3
SSM_GROUP = 16
SSM_STATE = 64

SUBLANES = 8
LANES = 128
VMEM_LIMIT_BYTES = 56 * 1024 * 1024

ROW_TILE = 1024
COL_TILE = 512
QKV_COL_TILE = 1024
PROJ_ROW_TILE = 512
PROJ_COL_TILE = 2048
GLU_COL_TILE = 1024
FF_TILE = 512
MLP_OUT_TILE = 512
MLP_K_TILE = 2048
ATT_QBLOCK = 256
ATT_HEADS_PER_STEP = 8
SSM_ROW_TILE = 512
SSM_GROUPS_PER_BLOCK = 8
HALO = 16


def _params(*semantics):
    return pltpu.CompilerParams(dimension_semantics=semantics, vmem_limit_bytes=VMEM_LIMIT_BYTES)


def _rms_norm(x, gain):
    return x * lax.rsqrt(jnp.mean(x * x, axis=-1, keepdims=True) + RMS_EPS) * gain


def _dot(a, b):
    return jnp.dot(a, b, preferred_element_type=F32)


def _wdot(a, w_ref):
    return _dot(a, w_ref[...].astype(BF16))


def _mlp_kernel(x_hbm, g_ref, w1_hbm, w2_hbm, xr_ref, o_ref, x_buf, h_ref, a_ref, w1_buf, w2_buf,
                semx, sem1, sem2, *, layer, n_rows):
    i, t = pl.program_id(0), pl.program_id(1)
    n_out = pl.num_programs(1) - 1
    tf, tk, tn = w1_buf.shape[2], w2_buf.shape[1], w2_buf.shape[2]
    n_up, n_k = a_ref.shape[1] // tf, a_ref.shape[1] // tk
    assert n_up % 2 == 0 and n_k % 2 == 0

    def w1_copy(f, slot):
        return pltpu.make_async_copy(w1_hbm.at[layer, :, pl.ds(f * tf, tf)], w1_buf.at[slot],
                                     sem1.at[slot])

    def w2_copy(n, kh):
        return pltpu.make_async_copy(w2_hbm.at[layer, pl.ds(kh * tk, tk), pl.ds(n * tn, tn)],
                                     w2_buf.at[kh % 2], sem2.at[kh % 2])

    def x_copy(row):
        tm = x_buf.shape[0]
        return pltpu.make_async_copy(x_hbm.at[pl.ds(row * tm, tm), :], x_buf, semx.at[0])

    @pl.when(t == 0)
    def _():
        @pl.when(i == 0)
        def _():
            x_copy(0).start()
            w1_copy(0, 0).start()

        w2_copy(0, 0).start()
        w2_copy(0, 1).start()
        x_copy(i).wait()
        h_ref[...] = _rms_norm(x_buf[...], g_ref[...]).astype(BF16)

        @pl.when(i + 1 < n_rows)
        def _():
            x_copy(i + 1).start()

        def up_pair(p, _):
            for slot in (0, 1):
                f = 2 * p + slot
                w1_copy(f, slot).wait()

                @pl.when(f + 1 < n_up)
                def _():
                    w1_copy(f + 1, 1 - slot).start()

                a = jnp.maximum(_dot(h_ref[...], w1_buf[slot].astype(BF16)), 0.0)
                a_ref[:, pl.ds(pl.multiple_of(f * tf, tf), tf)] = (a * a).astype(BF16)
            return 0

        lax.fori_loop(0, n_up // 2, up_pair, 0)

    @pl.when(t > 0)
    def _():
        n = t - 1
        for kh in range(n_k):
            w2_copy(n, kh).wait()
            part = _dot(a_ref[:, kh * tk:(kh + 1) * tk], w2_buf[kh % 2].astype(BF16))
            if kh == 0:
                o_ref[...] = xr_ref[...] + part
            else:
                o_ref[...] += part
            if kh + 2 < n_k:
                w2_copy(n, kh + 2).start()
            else:
                @pl.when(n + 1 < n_out)
                def _(kh=kh):
                    w2_copy(n + 1, kh + 2 - n_k).start()

        @pl.when((n + 1 == n_out) & (i + 1 < n_rows))
        def _():
            w1_copy(0, 0).start()


def _mlp(x, gain, w1, w2, layer):
    s, d = x.shape
    dff = w1.shape[2]
    tm, tf, tn, tk = min(ROW_TILE, s), FF_TILE, MLP_OUT_TILE, MLP_K_TILE
    n_rows, n_out = s // tm, d // tn

    def out_tile(t):
        return jnp.maximum(t - 1, 0)

    return pl.pallas_call(
        functools.partial(_mlp_kernel, layer=layer, n_rows=n_rows),
        grid=(n_rows, 1 + n_out),
        in_specs=[
            pl.BlockSpec(memory_space=pl.ANY),
            pl.BlockSpec((1, d), lambda i, t: (0, 0)),
            pl.BlockSpec(memory_space=pl.ANY),
            pl.BlockSpec(memory_space=pl.ANY),
            pl.BlockSpec((tm, tn), lambda i, t: (i, out_tile(t))),
        ],
        out_specs=pl.BlockSpec((tm, tn), lambda i, t: (i, out_tile(t))),
        out_shape=jax.ShapeDtypeStruct((s, d), F32),
        scratch_shapes=[
            pltpu.VMEM((tm, d), F32),
            pltpu.VMEM((tm, d), BF16),
            pltpu.VMEM((tm, dff), BF16),
            pltpu.VMEM((2, d, tf), F32),
            pltpu.VMEM((2, tk, tn), F32),
            pltpu.SemaphoreType.DMA((1,)),
            pltpu.SemaphoreType.DMA((2,)),
            pltpu.SemaphoreType.DMA((2,)),
        ],
        compiler_params=_params("arbitrary", "arbitrary"),
        name="mlp",
    )(x, gain, w1, w2, x)


def _proj_res_kernel(a_ref, w_ref, x_ref, o_ref, wb_ref):
    @pl.when(pl.program_id(1) == 0)
    def _():
        wb_ref[...] = w_ref[...].astype(BF16)

    o_ref[...] = x_ref[...] + _dot(a_ref[...], wb_ref[...])


def _proj_res(a, w, x, layer):
    s, k = a.shape
    n = w.shape[2]
    tm, tn = min(PROJ_ROW_TILE, s), min(PROJ_COL_TILE, n)
    return pl.pallas_call(
        _proj_res_kernel,
        grid=(n // tn, s // tm),
        in_specs=[
            pl.BlockSpec((tm, k), lambda j, i: (i, 0)),
            pl.BlockSpec((None, k, tn), lambda j, i: (layer, 0, j), pipeline_mode=pl.Buffered(1)),
            pl.BlockSpec((tm, tn), lambda j, i: (i, j)),
        ],
        out_specs=pl.BlockSpec((tm, tn), lambda j, i: (i, j)),
        out_shape=jax.ShapeDtypeStruct((s, n), F32),
        scratch_shapes=[pltpu.VMEM((k, tn), BF16)],
        compiler_params=_params("arbitrary", "arbitrary"),
        name="proj_res",
    )(a, w, x)


def _glu_res_kernel(a_ref, wv_ref, wg_ref, x_ref, o_ref, wvb_ref, wgb_ref):
    @pl.when(pl.program_id(1) == 0)
    def _():
        wvb_ref[...] = wv_ref[...].astype(BF16)
        wgb_ref[...] = wg_ref[...].astype(BF16)

    a = a_ref[...]
    val = _dot(a, wvb_ref[...])
    gate = _dot(a, wgb_ref[...])
    o_ref[...] = x_ref[...] + val * jax.nn.sigmoid(gate)


def _glu_res(a, w_glu, x, layer):
    s, k = a.shape
    n = w_glu.shape[2] // 2
    tm, tn = min(PROJ_ROW_TILE, s), min(GLU_COL_TILE, n)
    nj = n // tn
    return pl.pallas_call(
        _glu_res_kernel,
        grid=(nj, s // tm),
        in_specs=[
            pl.BlockSpec((tm, k), lambda j, i: (i, 0)),
            pl.BlockSpec((None, k, tn), lambda j, i: (layer, 0, j), pipeline_mode=pl.Buffered(1)),
            pl.BlockSpec((None, k, tn), lambda j, i: (layer, 0, j + nj), pipeline_mode=pl.Buffered(1)),
            pl.BlockSpec((tm, tn), lambda j, i: (i, j)),
        ],
        out_specs=pl.BlockSpec((tm, tn), lambda j, i: (i, j)),
        out_shape=jax.ShapeDtypeStruct((s, n), F32),
        scratch_shapes=[pltpu.VMEM((k, tn), BF16), pltpu.VMEM((k, tn), BF16)],
        compiler_params=_params("arbitrary", "arbitrary"),
        name="glu_res",
    )(a, w_glu, w_glu, x)


def _conv_front_kernel(x_ref, g_ref, wb_ref, wc_ref, wv_ref, cw_ref, o_ref, h_ref, ext_ref, carry_ref):
    i, j = pl.program_id(0), pl.program_id(1)
    tm = o_ref.shape[0]

    @pl.when(j == 0)
    def _():
        h_ref[...] = _rms_norm(x_ref[...], g_ref[...]).astype(BF16)

    @pl.when(i == 0)
    def _():
        carry_ref[j] = jnp.zeros(carry_ref.shape[1:], F32)

    h = h_ref[...]
    u = _wdot(h, wc_ref) * _wdot(h, wv_ref)
    ext_ref[0:HALO, :] = carry_ref[j]
    ext_ref[HALO:, :] = u
    carry_ref[j] = u[tm - HALO:, :]
    conv = cw_ref[CONV_WIDTH - 1:CONV_WIDTH, :] * u
    for lag in range(1, CONV_WIDTH):
        tap = CONV_WIDTH - 1 - lag
        conv += cw_ref[tap:tap + 1, :] * ext_ref[pl.ds(HALO - lag, tm), :]
    o_ref[...] = (_wdot(h, wb_ref) * conv).astype(BF16)


def _conv_front(x, gain, w_in, conv_w, layer):
    s, d = x.shape
    tm, tn = min(ROW_TILE, s), COL_TILE
    nj = d // tn
    return pl.pallas_call(
        _conv_front_kernel,
        grid=(s // tm, nj),
        in_specs=[
            pl.BlockSpec((tm, d), lambda i, j: (i, 0)),
            pl.BlockSpec((1, d), lambda i, j: (0, 0)),
            pl.BlockSpec((None, d, tn), lambda i, j: (layer, 0, j)),
            pl.BlockSpec((None, d, tn), lambda i, j: (layer, 0, j + nj)),
            pl.BlockSpec((None, d, tn), lambda i, j: (layer, 0, j + 2 * nj)),
            pl.BlockSpec((CONV_WIDTH, tn), lambda i, j: (0, j)),
        ],
        out_specs=pl.BlockSpec((tm, tn), lambda i, j: (i, j)),
        out_shape=jax.ShapeDtypeStruct((s, d), BF16),
        scratch_shapes=[
            pltpu.VMEM((tm, d), BF16),
            pltpu.VMEM((tm + HALO, tn), F32),
            pltpu.VMEM((nj, HALO, tn), F32),
        ],
        compiler_params=_params("arbitrary", "arbitrary"),
        name="conv_front",
    )(x, gain, w_in, w_in, w_in, conv_w)


def _pool_kernel(x_ref, g_ref, win_ref, wg_ref, sc_ref, xr_ref, o_ref, h_ref, ext_ref, carry_ref):
    i, grp = pl.program_id(0), pl.program_id(1)
    tm = o_ref.shape[0]

    @pl.when(grp == 0)
    def _():
        h_ref[...] = _rms_norm(x_ref[...], g_ref[...]).astype(BF16)

    @pl.when(i == 0)
    def _():
        carry_ref[grp] = jnp.zeros(carry_ref.shape[1:], F32)

    u = _wdot(h_ref[...], win_ref)
    ext_ref[0:HALO, :] = carry_ref[grp]
    ext_ref[HALO:, :] = u
    carry_ref[grp] = u[tm - HALO:, :]
    pos = (i * tm + 1 + lax.broadcasted_iota(jnp.int32, (tm, 1), 0)).astype(F32)

    for gi, w in enumerate(POOL_WINDOWS):
        @pl.when(grp == gi)
        def _(w=w):
            acc = u
            for lag in range(1, w):
                acc += ext_ref[pl.ds(HALO - lag, tm), :]
            inv_count = 1.0 / jnp.minimum(pos, float(w))
            pooled = (acc * inv_count - u).astype(BF16)
            o_ref[...] = xr_ref[...] + _dot(pooled, wg_ref[0].astype(BF16)) * sc_ref[...]


def _pool_mixer(x, gain, w_in, w_group, scale, layer):
    s, d = x.shape
    ng, pg = w_group.shape[1], w_group.shape[2]
    tm = min(ROW_TILE, s)
    return pl.pallas_call(
        _pool_kernel,
        grid=(s // tm, ng),
        in_specs=[
            pl.BlockSpec((tm, d), lambda i, g: (i, 0)),
            pl.BlockSpec((1, d), lambda i, g: (0, 0)),
            pl.BlockSpec((None, d, pg), lambda i, g: (layer, 0, g)),
            pl.BlockSpec((None, 1, pg, pg), lambda i, g: (layer, g, 0, 0)),
            pl.BlockSpec((1, pg), lambda i, g: (0, g)),
            pl.BlockSpec((tm, pg), lambda i, g: (i, g)),
        ],
        out_specs=pl.BlockSpec((tm, pg), lambda i, g: (i, g)),
        out_shape=jax.ShapeDtypeStruct((s, d), F32),
        scratch_shapes=[
            pltpu.VMEM((tm, d), BF16),
            pltpu.VMEM((tm + HALO, pg), F32),
            pltpu.VMEM((ng, HALO, pg), F32),
        ],
        compiler_params=_params("arbitrary", "arbitrary"),
        name="pool_mixer",
    )(x, gain, w_in, w_group, scale, x)


def _qkv_kernel(x_ref, g_ref, w_ref, hg_ref, o_ref, h_ref, *, n_norm_tiles):
    j = pl.program_id(1)

    @pl.when(j == 0)
    def _():
        h_ref[...] = _rms_norm(x_ref[...], g_ref[...]).astype(BF16)

    y = _wdot(h_ref[...], w_ref)

    @pl.when(j < n_norm_tiles)
    def _():
        for hd in range(y.shape[1] // ATT_HEAD_DIM):
            sl = slice(hd * ATT_HEAD_DIM, (hd + 1) * ATT_HEAD_DIM)
            o_ref[:, sl] = _rms_norm(y[:, sl], hg_ref[:, sl]).astype(BF16)

    @pl.when(j >= n_norm_tiles)
    def _():
        o_ref[...] = y.astype(BF16)


def _qkv_proj(x, gain, w_qkv, head_gains, layer):
    s, d = x.shape
    n = w_qkv.shape[2]
    tm, tn = min(ROW_TILE, s), QKV_COL_TILE
    n_norm_tiles = head_gains.shape[1] // tn
    return pl.pallas_call(
        functools.partial(_qkv_kernel, n_norm_tiles=n_norm_tiles),
        grid=(s // tm, n // tn),
        in_specs=[
            pl.BlockSpec((tm, d), lambda i, j: (i, 0)),
            pl.BlockSpec((1, d), lambda i, j: (0, 0)),
            pl.BlockSpec((None, d, tn), lambda i, j: (layer, 0, j)),
            pl.BlockSpec((1, tn), lambda i, j: (0, jnp.minimum(j, n_norm_tiles - 1))),
        ],
        out_specs=pl.BlockSpec((tm, tn), lambda i, j: (i, j)),
        out_shape=jax.ShapeDtypeStruct((s, n), BF16),
        scratch_shapes=[pltpu.VMEM((tm, d), BF16)],
        compiler_params=_params("parallel", "arbitrary"),
        name="qkv_proj",
    )(x, gain, w_qkv, head_gains)


def _attn_kernel(q_ref, k0_ref, k1_ref, k2_ref, v0_ref, v1_ref, v2_ref, prof_ref, o_ref, bias_ref):
    b = pl.program_id(1)
    qb = q_ref.shape[0]
    scale = ATT_HEAD_DIM ** -0.5
    n_local = q_ref.shape[1] // ATT_HEAD_DIM

    @pl.when(b == 0)
    def _():
        q_idx = lax.broadcasted_iota(jnp.int32, (qb, 3 * qb), 0) + 2 * qb
        k_idx = lax.broadcasted_iota(jnp.int32, (qb, 3 * qb), 1)
        chunk_start = (q_idx // CHUNK) * CHUNK
        in_band = (k_idx >= chunk_start - ATT_LEFT_CHUNKS * CHUNK) & (k_idx < chunk_start + CHUNK)
        for hh in range(n_local):
            profile = jnp.broadcast_to(prof_ref[hh], (qb, prof_ref.shape[2]))
            table = pltpu.roll(profile, 0, 1, stride=1, stride_axis=0)[:, :3 * qb]
            bias_ref[hh] = jnp.where(in_band, table, MASK_VALUE)

    for hh in range(n_local):
        cols = slice(hh * ATT_HEAD_DIM, (hh + 1) * ATT_HEAD_DIM)
        q = q_ref[:, cols]
        scores = []
        for c, k_ref in enumerate((k0_ref, k1_ref, k2_ref)):
            s_c = lax.dot_general(q, k_ref[:, cols], (((1,), (1,)), ((), ())),
                                  preferred_element_type=F32)
            s_c = s_c * scale + bias_ref[hh, :, c * qb:(c + 1) * qb]
            if c < 2:
                s_c = jnp.where(b - 2 + c >= 0, s_c, MASK_VALUE)
            scores.append(s_c)
        def lane_slabs(a):
            return [a[:, k * LANES:(k + 1) * LANES] for k in range(a.shape[1] // LANES)]

        m = jnp.max(functools.reduce(jnp.maximum, [sl for s_c in scores for sl in lane_slabs(s_c)]),
                    axis=-1, keepdims=True)
        acc = jnp.zeros((qb, ATT_HEAD_DIM), F32)
        p_slabs = []
        for s_c, v_ref in zip(scores, (v0_ref, v1_ref, v2_ref)):
            p = jnp.exp(s_c - m)
            p_slabs += lane_slabs(p)
            acc += _dot(p.astype(BF16), v_ref[:, cols])
        denom = jnp.sum(functools.reduce(jnp.add, p_slabs), axis=-1, keepdims=True)
        o_ref[:, cols] = (acc / denom).astype(BF16)


def _attention(qkv, profile, n_heads):
    s = qkv.shape[0]
    qb = ATT_QBLOCK
    nb = s // qb
    hps = ATT_HEADS_PER_STEP
    width = hps * ATT_HEAD_DIM
    ngroups = n_heads // hps

    def kv_spec(offset, back):
        return pl.BlockSpec((qb, width), lambda h, b: (jnp.maximum(b - back, 0), offset + h))

    return pl.pallas_call(
        _attn_kernel,
        grid=(ngroups, nb),
        in_specs=[
            pl.BlockSpec((qb, width), lambda h, b: (b, h)),
            kv_spec(ngroups, 2), kv_spec(ngroups, 1), kv_spec(ngroups, 0),
            kv_spec(2 * ngroups, 2), kv_spec(2 * ngroups, 1), kv_spec(2 * ngroups, 0),
            pl.BlockSpec((hps, 1, 4 * qb), lambda h, b: (h, 0, 0)),
        ],
        out_specs=pl.BlockSpec((qb, width), lambda h, b: (b, h)),
        out_shape=jax.ShapeDtypeStruct((s, n_heads * ATT_HEAD_DIM), BF16),
        scratch_shapes=[pltpu.VMEM((hps, qb, 3 * qb), F32)],
        compiler_params=_params("arbitrary", "arbitrary"),
        name="chunk_attention",
    )(qkv, qkv, qkv, qkv, qkv, qkv, qkv, profile)


def _attention_profile(rel_bias):
    qb = ATT_QBLOCK
    n_heads = rel_bias.shape[0]
    assert REL_CLIP <= 2 * qb and qb <= REL_CLIP + 1 and 2 * REL_CLIP + 1 == rel_bias.shape[1]
    n_flat = 2 * qb - REL_CLIP + 1
    top = rel_bias[:, 2 * REL_CLIP:]
    profile = jnp.concatenate([
        jnp.broadcast_to(top, (n_heads, n_flat)),
        jnp.flip(rel_bias[:, REL_CLIP - qb + 1:2 * REL_CLIP], axis=1),
        jnp.broadcast_to(top, (n_heads, qb)),
    ], axis=1)
    return profile.reshape(n_heads, 1, 4 * qb).astype(F32)


def _ssm_prep_kernel(are_ref, aim_ref, ldt_ref, btr_ref, bti_ref,
                     abr_ref, abi_ref, apr_ref, api_ref, bbr_ref, bbi_ref, *, n_squarings):
    lam_r, lam_i = are_ref[...], aim_ref[...]
    dt = jnp.exp(ldt_ref[...])
    mag = jnp.exp(lam_r * dt)
    ab_r, ab_i = mag * jnp.cos(lam_i * dt), mag * jnp.sin(lam_i * dt)
    abr_ref[...] = ab_r
    abi_ref[...] = ab_i
    p_r, p_i = ab_r, ab_i
    for _ in range(n_squarings):
        p_r, p_i = p_r * p_r - p_i * p_i, 2.0 * p_r * p_i
    apr_ref[...] = p_r
    api_ref[...] = p_i
    num_r, num_i = ab_r - 1.0, ab_i
    den = lam_r * lam_r + lam_i * lam_i
    co_r = (num_r * lam_r + num_i * lam_i) / den
    co_i = (num_i * lam_r - num_r * lam_i) / den
    for c in range(btr_ref.shape[0]):
        b_r, b_i = btr_ref[c], bti_ref[c]
        bbr_ref[c] = co_r * b_r - co_i * b_i
        bbi_ref[c] = co_r * b_i + co_i * b_r


def _ssm_prep(a_re, a_im, log_dt, bt_re, bt_im, n_squarings):
    g, n = a_re.shape
    gn = jax.ShapeDtypeStruct((g, n), F32)
    cgn = jax.ShapeDtypeStruct(bt_re.shape, F32)
    return pl.pallas_call(
        functools.partial(_ssm_prep_kernel, n_squarings=n_squarings),
        out_shape=(gn, gn, gn, gn, cgn, cgn),
        name="ssm_prep",
    )(a_re, a_im, log_dt, bt_re, bt_im)


def _ssm_kernel(x_ref, g_ref, d_ref, wbr_ref, wbi_ref, wcr_ref, wci_ref,
                ar_ref, ai_ref, pr_ref, pi_ref, o_ref,
                hp_ref, y_ref, sra_ref, sia_ref, srb_ref, sib_ref, cr_ref, ci_ref):
    i = pl.program_id(0)
    t = x_ref.shape[0]
    seg = t // SUBLANES
    nblk, cb, ns = wbr_ref.shape
    assert cb == LANES and hp_ref.shape[0] == nblk

    h = _rms_norm(x_ref[...], g_ref[...])
    for c in range(nblk):
        for s_idx in range(SUBLANES):
            hp_ref[c, pl.ds(s_idx, seg, stride=SUBLANES), :] = (
                h[s_idx * seg:(s_idx + 1) * seg, c * LANES:(c + 1) * LANES])

    @pl.when(i == 0)
    def _():
        cr_ref[...] = jnp.zeros(cr_ref.shape, F32)
        ci_ref[...] = jnp.zeros(ci_ref.shape, F32)

    zeros = jnp.zeros((SUBLANES, ns), F32)
    sub = lax.broadcasted_iota(jnp.int32, (SUBLANES, ns), 0)

    def project_in(gb, sr_ref, si_ref):
        hb = hp_ref[gb].astype(BF16)
        sr_ref[...] = _dot(hb, wbr_ref[gb])
        si_ref[...] = _dot(hb, wbi_ref[gb])

    def group_block(gb, sr_ref, si_ref, nxt_r_ref, nxt_i_ref):
        a_r = jnp.broadcast_to(ar_ref[gb], (SUBLANES, ns))
        a_i = jnp.broadcast_to(ai_ref[gb], (SUBLANES, ns))

        def step(j, carry, store):
            x_r, x_i = carry
            rows = pl.ds(pl.multiple_of(j * SUBLANES, SUBLANES), SUBLANES)
            n_r = a_r * x_r - a_i * x_i + sr_ref[rows, :]
            n_i = a_r * x_i + a_i * x_r + si_ref[rows, :]
            if store:
                sr_ref[rows, :] = n_r
                si_ref[rows, :] = n_i
            return n_r, n_i

        f_r, f_i = lax.fori_loop(0, seg, functools.partial(step, store=False), (zeros, zeros),
                                 unroll=4)
        p_r, p_i = pr_ref[gb], pi_ref[gb]
        row_r, row_i = cr_ref[gb], ci_ref[gb]
        init_r, init_i = zeros, zeros
        for s_idx in range(SUBLANES):
            init_r = jnp.where(sub == s_idx, jnp.broadcast_to(row_r, (SUBLANES, ns)), init_r)
            init_i = jnp.where(sub == s_idx, jnp.broadcast_to(row_i, (SUBLANES, ns)), init_i)
            if s_idx + 1 < SUBLANES:
                row_r, row_i = (p_r * row_r - p_i * row_i + f_r[s_idx:s_idx + 1, :],
                                p_r * row_i + p_i * row_r + f_i[s_idx:s_idx + 1, :])
        l_r, l_i = lax.fori_loop(0, seg, functools.partial(step, store=True), (init_r, init_i),
                                 unroll=4)
        cr_ref[gb] = l_r[SUBLANES - 1:SUBLANES, :]
        ci_ref[gb] = l_i[SUBLANES - 1:SUBLANES, :]
        y_ref[gb] = (_dot(sr_ref[...].astype(BF16), wcr_ref[gb])
                     + _dot(si_ref[...].astype(BF16), wci_ref[gb]))
        project_in(jnp.minimum(gb + 1, nblk - 1), nxt_r_ref, nxt_i_ref)

    def block_pair(p, _):
        group_block(2 * p, sra_ref, sia_ref, srb_ref, sib_ref)
        group_block(2 * p + 1, srb_ref, sib_ref, sra_ref, sia_ref)
        return 0

    assert nblk % 2 == 0
    project_in(0, sra_ref, sia_ref)
    lax.fori_loop(0, nblk // 2, block_pair, 0)

    for c in range(nblk):
        lanes = slice(c * LANES, (c + 1) * LANES)
        y_ref[c] = jax.nn.gelu(y_ref[c] + d_ref[:, lanes] * hp_ref[c])
        for s_idx in range(SUBLANES):
            o_ref[s_idx * seg:(s_idx + 1) * seg, lanes] = (
                y_ref[c, pl.ds(s_idx, seg, stride=SUBLANES), :].astype(BF16))


def _ssm_core(x, gain, d_skip, wb_r, wb_i, wc_r, wc_i, ab_r, ab_i, ap_r, ap_i):
    s, d = x.shape
    t = min(SSM_ROW_TILE, s)
    nblk, cb, ns = wb_r.shape

    def whole(a):
        return pl.BlockSpec(a.shape, lambda i: (0,) * a.ndim)

    return pl.pallas_call(
        _ssm_kernel,
        grid=(s // t,),
        in_specs=[
            pl.BlockSpec((t, d), lambda i: (i, 0)),
            whole(gain), whole(d_skip),
            whole(wb_r), whole(wb_i), whole(wc_r), whole(wc_i),
            whole(ab_r), whole(ab_i), whole(ap_r), whole(ap_i),
        ],
        out_specs=pl.BlockSpec((t, d), lambda i: (i, 0)),
        out_shape=jax.ShapeDtypeStruct((s, d), BF16),
        scratch_shapes=[
            pltpu.VMEM((nblk, t, LANES), F32),
            pltpu.VMEM((nblk, t, LANES), F32),
            pltpu.VMEM((t, ns), F32),
            pltpu.VMEM((t, ns), F32),
            pltpu.VMEM((t, ns), F32),
            pltpu.VMEM((t, ns), F32),
            pltpu.VMEM((nblk, 1, ns), F32),
            pltpu.VMEM((nblk, 1, ns), F32),
        ],
        compiler_params=_params("arbitrary"),
        name="ssm_core",
    )(x, gain, d_skip, wb_r, wb_i, wc_r, wc_i, ab_r, ab_i, ap_r, ap_i)


def _block_diag(w, groups_per_block):
    g, r, c = w.shape
    nblk = g // groups_per_block
    tiled = jnp.tile(w.reshape(nblk, groups_per_block * r, c), (1, 1, groups_per_block))
    row_group = lax.broadcasted_iota(jnp.int32, tiled.shape, 1) // r
    col_group = lax.broadcasted_iota(jnp.int32, tiled.shape, 2) // c
    return jnp.where(row_group == col_group, tiled, jnp.zeros_like(tiled))


def _s5_mixer(x, gain, a_re, a_im, log_dt, b_re, b_im, c_re, c_im, d_skip, w_glu, layer):
    s, d = x.shape
    g, n = a_re.shape
    gpb = SSM_GROUPS_PER_BLOCK
    seg = min(SSM_ROW_TILE, s) // SUBLANES
    n_squarings = int(math.log2(seg))
    assert 2 ** n_squarings == seg
    bt_re, bt_im = b_re.transpose(2, 0, 1), b_im.transpose(2, 0, 1)
    ab_r, ab_i, ap_r, ap_i, bb_r, bb_i = _ssm_prep(a_re, a_im, log_dt.reshape(g, 1), bt_re, bt_im,
                                                   n_squarings)
    wb_r = _block_diag(bb_r.transpose(1, 0, 2), gpb).astype(BF16)
    wb_i = _block_diag(bb_i.transpose(1, 0, 2), gpb).astype(BF16)
    wc_r = _block_diag(c_re.transpose(0, 2, 1), gpb).astype(BF16)
    wc_i = _block_diag(-c_im.transpose(0, 2, 1), gpb).astype(BF16)
    flat = lambda a: a.reshape(g // gpb, 1, gpb * n)
    z = _ssm_core(x, gain, d_skip.reshape(1, d), wb_r, wb_i, wc_r, wc_i,
                  flat(ab_r), flat(ab_i), flat(ap_r), flat(ap_i))
    return _glu_res(z, w_glu, x, layer)


def kernel(x, norm_mix, norm_mlp, mlp_w1, mlp_w2, conv_w_in, conv_w, conv_w_out, pool_w_in, pool_w_group, pool_scale, att_w_qkv, att_q_norm, att_k_norm, att_rel_bias, att_w_out, ssm_a_re, ssm_a_im, ssm_log_dt, ssm_b_re, ssm_b_im, ssm_c_re, ssm_c_im, ssm_d, ssm_w_glu):
    b, s, d = x.shape
    depth = norm_mix.shape[0]
    n_mixers = 4
    outs = []
    for bi in range(b):
        xs = x[bi]
        for i in range(depth):
            kind, j = i % n_mixers, i // n_mixers
            gain = norm_mix[i].reshape(1, d)
            if kind == 0:
                gated = _conv_front(xs, gain, conv_w_in, conv_w[j], j)
                xs = _proj_res(gated, conv_w_out, xs, j)
            elif kind == 1:
                xs = _pool_mixer(xs, gain, pool_w_in, pool_w_group, pool_scale[j].reshape(1, d), j)
            elif kind == 2:
                n_heads = d // ATT_HEAD_DIM
                head_gains = jnp.concatenate([jnp.tile(att_q_norm[j], n_heads),
                                              jnp.tile(att_k_norm[j], n_heads)]).reshape(1, 2 * d)
                qkv = _qkv_proj(xs, gain, att_w_qkv, head_gains, j)
                att = _attention(qkv, _attention_profile(att_rel_bias[j]), n_heads)
                xs = _proj_res(att, att_w_out, xs, j)
            else:
                xs = _s5_mixer(xs, gain, ssm_a_re[j], ssm_a_im[j], ssm_log_dt[j], ssm_b_re[j],
                               ssm_b_im[j], ssm_c_re[j], ssm_c_im[j], ssm_d[j], ssm_w_glu, j)
            xs = _mlp(xs, norm_mlp[i].reshape(1, d), mlp_w1, mlp_w2, i)
        outs.append(xs)
    return outs[0][None] if b == 1 else jnp.stack(outs)
```

```python
import functools
import math

import jax
import jax.numpy as jnp
from jax import lax
from jax.experimental import pallas as pl
from jax.experimental.pallas import tpu as pltpu

F32 = jnp.float32
BF16 = jnp.bfloat16

RMS_EPS = 1e-6
CHUNK = 64
ATT_HEAD_DIM = 128
ATT_LEFT_CHUNKS = 8
REL_CLIP = 256
MASK_VALUE = -1e30
POOL_WINDOWS = (2, 4, 8, 16)
CONV_WIDTH = 3
SSM_GROUP = 16
SSM_STATE = 64

SUBLANES = 8
LANES = 128
VMEM_LIMIT_BYTES = 56 * 1024 * 1024

ROW_TILE = 1024
COL_TILE = 512
QKV_COL_TILE = 1024
PROJ_ROW_TILE = 512
PROJ_COL_TILE = 2048
GLU_COL_TILE = 1024
FF_TILE = 512
MLP_OUT_TILE = 512
MLP_K_TILE = 2048
ATT_QBLOCK = 256
ATT_HEADS_PER_STEP = 8
SSM_ROW_TILE = 512
SSM_GROUPS_PER_BLOCK = 8
HALO = 16
POOL_HALO = 32


def _params(*semantics):
    return pltpu.CompilerParams(dimension_semantics=semantics, vmem_limit_bytes=VMEM_LIMIT_BYTES)


def _rms_norm(x, gain):
    return x * lax.rsqrt(jnp.mean(x * x, axis=-1, keepdims=True) + RMS_EPS) * gain


def _dot(a, b):
    return jnp.dot(a, b, preferred_element_type=F32)


def _wdot(a, w_ref):
    return _dot(a, w_ref[...].astype(BF16))


def _mlp_kernel(x_hbm, g_ref, w1_hbm, w2_hbm, xr_ref, o_ref, x_buf, h_ref, a_ref, w1_buf, w2_buf,
                semx, sem1, sem2, *, layer, n_rows):
    i, t = pl.program_id(0), pl.program_id(1)
    n_out = pl.num_programs(1) - 1
    tf, tk, tn = w1_buf.shape[2], w2_buf.shape[1], w2_buf.shape[2]
    n_up, n_k = a_ref.shape[1] // tf, a_ref.shape[1] // tk
    assert n_up % 2 == 0 and n_k % 2 == 0

    def w1_copy(f, slot):
        return pltpu.make_async_copy(w1_hbm.at[layer, :, pl.ds(f * tf, tf)], w1_buf.at[slot],
                                     sem1.at[slot])

    def w2_copy(n, kh):
        return pltpu.make_async_copy(w2_hbm.at[layer, pl.ds(kh * tk, tk), pl.ds(n * tn, tn)],
                                     w2_buf.at[kh % 2], sem2.at[kh % 2])

    def x_copy(row):
        tm = x_buf.shape[0]
        return pltpu.make_async_copy(x_hbm.at[pl.ds(row * tm, tm), :], x_buf, semx.at[0])

    @pl.when(t == 0)
    def _():
        @pl.when(i == 0)
        def _():
            x_copy(0).start()
            w1_copy(0, 0).start()

        w2_copy(0, 0).start()
        w2_copy(0, 1).start()
        x_copy(i).wait()
        h_ref[...] = _rms_norm(x_buf[...], g_ref[...]).astype(BF16)

        @pl.when(i + 1 < n_rows)
        def _():
            x_copy(i + 1).start()

        def up_pair(p, _):
            for slot in (0, 1):
                f = 2 * p + slot
                w1_copy(f, slot).wait()

                @pl.when(f + 1 < n_up)
                def _():
                    w1_copy(f + 1, 1 - slot).start()

                a = jnp.maximum(_dot(h_ref[...], w1_buf[slot].astype(BF16)), 0.0)
                a_ref[:, pl.ds(pl.multiple_of(f * tf, tf), tf)] = (a * a).astype(BF16)
            return 0

        lax.fori_loop(0, n_up // 2, up_pair, 0)

    @pl.when(t > 0)
    def _():
        n = t - 1
        for kh in range(n_k):
            w2_copy(n, kh).wait()
            part = _dot(a_ref[:, kh * tk:(kh + 1) * tk], w2_buf[kh % 2].astype(BF16))
            if kh == 0:
                o_ref[...] = xr_ref[...] + part
            else:
                o_ref[...] += part
            if kh + 2 < n_k:
                w2_copy(n, kh + 2).start()
            else:
                @pl.when(n + 1 < n_out)
                def _(kh=kh):
                    w2_copy(n + 1, kh + 2 - n_k).start()

        @pl.when((n + 1 == n_out) & (i + 1 < n_rows))
        def _():
            w1_copy(0, 0).start()


def _mlp(x, gain, w1, w2, layer):
    s, d = x.shape
    dff = w1.shape[2]
    tm, tf, tn, tk = min(ROW_TILE, s), FF_TILE, MLP_OUT_TILE, MLP_K_TILE
    n_rows, n_out = s // tm, d // tn

    def out_tile(t):
        return jnp.maximum(t - 1, 0)

    return pl.pallas_call(
        functools.partial(_mlp_kernel, layer=layer, n_rows=n_rows),
        grid=(n_rows, 1 + n_out),
        in_specs=[
            pl.BlockSpec(memory_space=pl.ANY),
            pl.BlockSpec((1, d), lambda i, t: (0, 0)),
            pl.BlockSpec(memory_space=pl.ANY),
            pl.BlockSpec(memory_space=pl.ANY),
            pl.BlockSpec((tm, tn), lambda i, t: (i, out_tile(t))),
        ],
        out_specs=pl.BlockSpec((tm, tn), lambda i, t: (i, out_tile(t))),
        out_shape=jax.ShapeDtypeStruct((s, d), F32),
        scratch_shapes=[
            pltpu.VMEM((tm, d), F32),
            pltpu.VMEM((tm, d), BF16),
            pltpu.VMEM((tm, dff), BF16),
            pltpu.VMEM((2, d, tf), F32),
            pltpu.VMEM((2, tk, tn), F32),
            pltpu.SemaphoreType.DMA((1,)),
            pltpu.SemaphoreType.DMA((2,)),
            pltpu.SemaphoreType.DMA((2,)),
        ],
        compiler_params=_params("arbitrary", "arbitrary"),
        name="mlp",
    )(x, gain, w1, w2, x)


def _proj_res_kernel(a_ref, w_ref, x_ref, o_ref, wb_ref):
    @pl.when(pl.program_id(1) == 0)
    def _():
        wb_ref[...] = w_ref[...].astype(BF16)

    o_ref[...] = x_ref[...] + _dot(a_ref[...], wb_ref[...])


def _proj_res(a, w, x, layer):
    s, k = a.shape
    n = w.shape[2]
    tm, tn = min(PROJ_ROW_TILE, s), min(PROJ_COL_TILE, n)
    return pl.pallas_call(
        _proj_res_kernel,
        grid=(n // tn, s // tm),
        in_specs=[
            pl.BlockSpec((tm, k), lambda j, i: (i, 0)),
            pl.BlockSpec((None, k, tn), lambda j, i: (layer, 0, j), pipeline_mode=pl.Buffered(1)),
            pl.BlockSpec((tm, tn), lambda j, i: (i, j)),
        ],
        out_specs=pl.BlockSpec((tm, tn), lambda j, i: (i, j)),
        out_shape=jax.ShapeDtypeStruct((s, n), F32),
        scratch_shapes=[pltpu.VMEM((k, tn), BF16)],
        compiler_params=_params("arbitrary", "arbitrary"),
        name="proj_res",
    )(a, w, x)


def _glu_res_kernel(a_ref, wv_ref, wg_ref, x_ref, o_ref, wvb_ref, wgb_ref):
    @pl.when(pl.program_id(1) == 0)
    def _():
        wvb_ref[...] = wv_ref[...].astype(BF16)
        wgb_ref[...] = wg_ref[...].astype(BF16)

    a = a_ref[...]
    val = _dot(a, wvb_ref[...])
    gate = _dot(a, wgb_ref[...])
    o_ref[...] = x_ref[...] + val * jax.nn.sigmoid(gate)


def _glu_res(a, w_glu, x, layer):
    s, k = a.shape
    n = w_glu.shape[2] // 2
    tm, tn = min(PROJ_ROW_TILE, s), min(GLU_COL_TILE, n)
    nj = n // tn
    return pl.pallas_call(
        _glu_res_kernel,
        grid=(nj, s // tm),
        in_specs=[
            pl.BlockSpec((tm, k), lambda j, i: (i, 0)),
            pl.BlockSpec((None, k, tn), lambda j, i: (layer, 0, j), pipeline_mode=pl.Buffered(1)),
            pl.BlockSpec((None, k, tn), lambda j, i: (layer, 0, j + nj), pipeline_mode=pl.Buffered(1)),
            pl.BlockSpec((tm, tn), lambda j, i: (i, j)),
        ],
        out_specs=pl.BlockSpec((tm, tn), lambda j, i: (i, j)),
        out_shape=jax.ShapeDtypeStruct((s, n), F32),
        scratch_shapes=[pltpu.VMEM((k, tn), BF16), pltpu.VMEM((k, tn), BF16)],
        compiler_params=_params("arbitrary", "arbitrary"),
        name="glu_res",
    )(a, w_glu, w_glu, x)


def _conv_front_kernel(x_ref, g_ref, wb_ref, wc_ref, wv_ref, cw_ref, o_ref, h_ref, ext_ref, carry_ref):
    i, j = pl.program_id(0), pl.program_id(1)
    tm = o_ref.shape[0]

    @pl.when(j == 0)
    def _():
        h_ref[...] = _rms_norm(x_ref[...], g_ref[...]).astype(BF16)

    @pl.when(i == 0)
    def _():
        carry_ref[j] = jnp.zeros(carry_ref.shape[1:], F32)

    h = h_ref[...]
    u = _wdot(h, wc_ref) * _wdot(h, wv_ref)
    ext_ref[0:HALO, :] = carry_ref[j]
    ext_ref[HALO:, :] = u
    carry_ref[j] = u[tm - HALO:, :]
    conv = cw_ref[CONV_WIDTH - 1:CONV_WIDTH, :] * u
    for lag in range(1, CONV_WIDTH):
        tap = CONV_WIDTH - 1 - lag
        conv += cw_ref[tap:tap + 1, :] * ext_ref[pl.ds(HALO - lag, tm), :]
    o_ref[...] = (_wdot(h, wb_ref) * conv).astype(BF16)


def _conv_front(x, gain, w_in, conv_w, layer):
    s, d = x.shape
    tm, tn = min(ROW_TILE, s), COL_TILE
    nj = d // tn
    return pl.pallas_call(
        _conv_front_kernel,
        grid=(s // tm, nj),
        in_specs=[
            pl.BlockSpec((tm, d), lambda i, j: (i, 0)),
            pl.BlockSpec((1, d), lambda i, j: (0, 0)),
            pl.BlockSpec((None, d, tn), lambda i, j: (layer, 0, j)),
            pl.BlockSpec((None, d, tn), lambda i, j: (layer, 0, j + nj)),
            pl.BlockSpec((None, d, tn), lambda i, j: (layer, 0, j + 2 * nj)),
            pl.BlockSpec((CONV_WIDTH, tn), lambda i, j: (0, j)),
        ],
        out_specs=pl.BlockSpec((tm, tn), lambda i, j: (i, j)),
        out_shape=jax.ShapeDtypeStruct((s, d), BF16),
        scratch_shapes=[
            pltpu.VMEM((tm, d), BF16),
            pltpu.VMEM((tm + HALO, tn), F32),
            pltpu.VMEM((nj, HALO, tn), F32),
        ],
        compiler_params=_params("arbitrary", "arbitrary"),
        name="conv_front",
    )(x, gain, w_in, w_in, w_in, conv_w)


def _pool_kernel(x_ref, g_ref, win_ref, wg_ref, sc_ref, xr_ref, o_ref, h_ref, ext_ref, la_ref, lb_ref,
                 carry_ref):
    i, grp = pl.program_id(0), pl.program_id(1)
    tm = o_ref.shape[0]
    halo = POOL_HALO
    n_ext = tm + halo

    @pl.when(grp == 0)
    def _():
        h_ref[...] = _rms_norm(x_ref[...], g_ref[...]).astype(BF16)

    @pl.when(i == 0)
    def _():
        carry_ref[grp] = jnp.zeros(carry_ref.shape[1:], F32)

    u = _wdot(h_ref[...], win_ref)
    ext_ref[0:halo, :] = carry_ref[grp]
    ext_ref[halo:, :] = u
    carry_ref[grp] = u[tm - halo:, :]
    pos = (i * tm + 1 + lax.broadcasted_iota(jnp.int32, (tm, 1), 0)).astype(F32)

    for gi, w in enumerate(POOL_WINDOWS):
        @pl.when(grp == gi)
        def _(w=w):
            levels = int(math.log2(w))
            assert 2 ** levels == w and SUBLANES * levels <= halo
            src = ext_ref
            for lv in range(1, levels):
                dst = la_ref if lv % 2 else lb_ref
                lo, shift = SUBLANES * lv, 2 ** (lv - 1)
                dst[lo:, :] = src[lo:, :] + src[pl.ds(lo - shift, n_ext - lo), :]
                src = dst
            acc = src[halo:, :] + src[pl.ds(halo - w // 2, tm), :]
            inv_count = 1.0 / jnp.minimum(pos, float(w))
            pooled = (acc * inv_count - u).astype(BF16)
            o_ref[...] = xr_ref[...] + _dot(pooled, wg_ref[0].astype(BF16)) * sc_ref[...]


def _pool_mixer(x, gain, w_in, w_group, scale, layer):
    s, d = x.shape
    ng, pg = w_group.shape[1], w_group.shape[2]
    tm = min(ROW_TILE, s)
    return pl.pallas_call(
        _pool_kernel,
        grid=(s // tm, ng),
        in_specs=[
            pl.BlockSpec((tm, d), lambda i, g: (i, 0)),
            pl.BlockSpec((1, d), lambda i, g: (0, 0)),
            pl.BlockSpec((None, d, pg), lambda i, g: (layer, 0, g)),
            pl.BlockSpec((None, 1, pg, pg), lambda i, g: (layer, g, 0, 0)),
            pl.BlockSpec((1, pg), lambda i, g: (0, g)),
            pl.BlockSpec((tm, pg), lambda i, g: (i, g)),
        ],
        out_specs=pl.BlockSpec((tm, pg), lambda i, g: (i, g)),
        out_shape=jax.ShapeDtypeStruct((s, d), F32),
        scratch_shapes=[
            pltpu.VMEM((tm, d), BF16),
            pltpu.VMEM((tm + POOL_HALO, pg), F32),
            pltpu.VMEM((tm + POOL_HALO, pg), F32),
            pltpu.VMEM((tm + POOL_HALO, pg), F32),
            pltpu.VMEM((ng, POOL_HALO, pg), F32),
        ],
        compiler_params=_params("arbitrary", "arbitrary"),
        name="pool_mixer",
    )(x, gain, w_in, w_group, scale, x)


def _qkv_kernel(x_ref, g_ref, w_ref, hg_ref, o_ref, h_ref, *, n_norm_tiles):
    j = pl.program_id(1)

    @pl.when(j == 0)
    def _():
        h_ref[...] = _rms_norm(x_ref[...], g_ref[...]).astype(BF16)

    y = _wdot(h_ref[...], w_ref)

    @pl.when(j < n_norm_tiles)
    def _():
        for hd in range(y.shape[1] // ATT_HEAD_DIM):
            sl = slice(hd * ATT_HEAD_DIM, (hd + 1) * ATT_HEAD_DIM)
            o_ref[:, sl] = _rms_norm(y[:, sl], hg_ref[:, sl]).astype(BF16)

    @pl.when(j >= n_norm_tiles)
    def _():
        o_ref[...] = y.astype(BF16)


def _qkv_proj(x, gain, w_qkv, head_gains, layer):
    s, d = x.shape
    n = w_qkv.shape[2]
    tm, tn = min(ROW_TILE, s), QKV_COL_TILE
    n_norm_tiles = head_gains.shape[1] // tn
    return pl.pallas_call(
        functools.partial(_qkv_kernel, n_norm_tiles=n_norm_tiles),
        grid=(s // tm, n // tn),
        in_specs=[
            pl.BlockSpec((tm, d), lambda i, j: (i, 0)),
            pl.BlockSpec((1, d), lambda i, j: (0, 0)),
            pl.BlockSpec((None, d, tn), lambda i, j: (layer, 0, j)),
            pl.BlockSpec((1, tn), lambda i, j: (0, jnp.minimum(j, n_norm_tiles - 1))),
        ],
        out_specs=pl.BlockSpec((tm, tn), lambda i, j: (i, j)),
        out_shape=jax.ShapeDtypeStruct((s, n), BF16),
        scratch_shapes=[pltpu.VMEM((tm, d), BF16)],
        compiler_params=_params("parallel", "arbitrary"),
        name="qkv_proj",
    )(x, gain, w_qkv, head_gains)


def _attn_kernel(q_ref, k0_ref, k1_ref, k2_ref, v0_ref, v1_ref, v2_ref, prof_ref, o_ref, bias_ref):
    b = pl.program_id(1)
    qb = q_ref.shape[0]
    n_local = q_ref.shape[1] // ATT_HEAD_DIM
    log2e = math.log2(math.e)
    scale = ATT_HEAD_DIM ** -0.5 * log2e

    @pl.when(b == 0)
    def _():
        q_idx = lax.broadcasted_iota(jnp.int32, (qb, 3 * qb), 0) + 2 * qb
        k_idx = lax.broadcasted_iota(jnp.int32, (qb, 3 * qb), 1)
        chunk_start = (q_idx // CHUNK) * CHUNK
        in_band = (k_idx >= chunk_start - ATT_LEFT_CHUNKS * CHUNK) & (k_idx < chunk_start + CHUNK)
        for hh in range(n_local):
            profile = jnp.broadcast_to(prof_ref[hh], (qb, prof_ref.shape[2]))
            table = pltpu.roll(profile, 0, 1, stride=1, stride_axis=0)[:, :3 * qb]
            bias_ref[hh] = jnp.where(in_band, table * log2e, MASK_VALUE)

    def lane_slabs(a):
        return [a[:, k * LANES:(k + 1) * LANES] for k in range(a.shape[1] // LANES)]

    def heads(first_blocks):
        for hh in range(n_local):
            cols = slice(hh * ATT_HEAD_DIM, (hh + 1) * ATT_HEAD_DIM)
            q = q_ref[:, cols]
            scores = []
            for c, k_ref in enumerate((k0_ref, k1_ref, k2_ref)):
                s_c = lax.dot_general(q, k_ref[:, cols], (((1,), (1,)), ((), ())),
                                      preferred_element_type=F32)
                s_c = s_c * scale + bias_ref[hh, :, c * qb:(c + 1) * qb]
                if first_blocks and c < 2:
                    s_c = jnp.where(b - 2 + c >= 0, s_c, MASK_VALUE)
                scores.append(s_c)
            m = jnp.max(functools.reduce(jnp.maximum,
                                         [sl for s_c in scores for sl in lane_slabs(s_c)]),
                        axis=-1, keepdims=True)
            acc = jnp.zeros((qb, ATT_HEAD_DIM), F32)
            p_slabs = []
            for s_c, v_ref in zip(scores, (v0_ref, v1_ref, v2_ref)):
                p = jnp.exp2(s_c - m)
                p_slabs += lane_slabs(p)
                acc += _dot(p.astype(BF16), v_ref[:, cols])
            denom = jnp.sum(functools.reduce(jnp.add, p_slabs), axis=-1, keepdims=True)
            o_ref[:, cols] = (acc / denom).astype(BF16)

    @pl.when(b < 2)
    def _():
        heads(True)

    @pl.when(b >= 2)
    def _():
        heads(False)


def _attention(qkv, profile, n_heads):
    s = qkv.shape[0]
    qb = ATT_QBLOCK
    nb = s // qb
    hps = ATT_HEADS_PER_STEP
    width = hps * ATT_HEAD_DIM
    ngroups = n_heads // hps

    def kv_spec(offset, back):
        return pl.BlockSpec((qb, width), lambda h, b: (jnp.maximum(b - back, 0), offset + h))

    return pl.pallas_call(
        _attn_kernel,
        grid=(ngroups, nb),
        in_specs=[
            pl.BlockSpec((qb, width), lambda h, b: (b, h)),
            kv_spec(ngroups, 2), kv_spec(ngroups, 1), kv_spec(ngroups, 0),
            kv_spec(2 * ngroups, 2), kv_spec(2 * ngroups, 1), kv_spec(2 * ngroups, 0),
            pl.BlockSpec((hps, 1, 4 * qb), lambda h, b: (h, 0, 0)),
        ],
        out_specs=pl.BlockSpec((qb, width), lambda h, b: (b, h)),
        out_shape=jax.ShapeDtypeStruct((s, n_heads * ATT_HEAD_DIM), BF16),
        scratch_shapes=[pltpu.VMEM((hps, qb, 3 * qb), F32)],
        compiler_params=_params("arbitrary", "arbitrary"),
        name="chunk_attention",
    )(qkv, qkv, qkv, qkv, qkv, qkv, qkv, profile)


def _attention_profile(rel_bias):
    qb = ATT_QBLOCK
    n_heads = rel_bias.shape[0]
    assert REL_CLIP <= 2 * qb and qb <= REL_CLIP + 1 and 2 * REL_CLIP + 1 == rel_bias.shape[1]
    n_flat = 2 * qb - REL_CLIP + 1
    top = rel_bias[:, 2 * REL_CLIP:]
    profile = jnp.concatenate([
        jnp.broadcast_to(top, (n_heads, n_flat)),
        jnp.flip(rel_bias[:, REL_CLIP - qb + 1:2 * REL_CLIP], axis=1),
        jnp.broadcast_to(top, (n_heads, qb)),
    ], axis=1)
    return profile.reshape(n_heads, 1, 4 * qb).astype(F32)


def _ssm_prep_kernel(are_ref, aim_ref, ldt_ref, btr_ref, bti_ref,
                     abr_ref, abi_ref, apr_ref, api_ref, bbr_ref, bbi_ref, *, n_squarings):
    lam_r, lam_i = are_ref[...], aim_ref[...]
    dt = jnp.exp(ldt_ref[...])
    mag = jnp.exp(lam_r * dt)
    ab_r, ab_i = mag * jnp.cos(lam_i * dt), mag * jnp.sin(lam_i * dt)
    abr_ref[...] = ab_r
    abi_ref[...] = ab_i
    p_r, p_i = ab_r, ab_i
    for _ in range(n_squarings):
        p_r, p_i = p_r * p_r - p_i * p_i, 2.0 * p_r * p_i
    apr_ref[...] = p_r
    api_ref[...] = p_i
    num_r, num_i = ab_r - 1.0, ab_i
    den = lam_r * lam_r + lam_i * lam_i
    co_r = (num_r * lam_r + num_i * lam_i) / den
    co_i = (num_i * lam_r - num_r * lam_i) / den
    for c in range(btr_ref.shape[0]):
        b_r, b_i = btr_ref[c], bti_ref[c]
        bbr_ref[c] = co_r * b_r - co_i * b_i
        bbi_ref[c] = co_r * b_i + co_i * b_r


def _ssm_prep(a_re, a_im, log_dt, bt_re, bt_im, n_squarings):
    g, n = a_re.shape
    gn = jax.ShapeDtypeStruct((g, n), F32)
    cgn = jax.ShapeDtypeStruct(bt_re.shape, F32)
    return pl.pallas_call(
        functools.partial(_ssm_prep_kernel, n_squarings=n_squarings),
        out_shape=(gn, gn, gn, gn, cgn, cgn),
        name="ssm_prep",
    )(a_re, a_im, log_dt, bt_re, bt_im)


def _ssm_kernel(x_ref, g_ref, d_ref, wbr_ref, wbi_ref, wcr_ref, wci_ref,
                ar_ref, ai_ref, pr_ref, pi_ref, o_ref,
                hp_ref, y_ref, sra_ref, sia_ref, srb_ref, sib_ref, cr_ref, ci_ref):
    i = pl.program_id(0)
    t = x_ref.shape[0]
    seg = t // SUBLANES
    nblk, cb, ns = wbr_ref.shape
    assert cb == LANES and hp_ref.shape[0] == nblk

    h = _rms_norm(x_ref[...], g_ref[...])
    for c in range(nblk):
        for s_idx in range(SUBLANES):
            hp_ref[c, pl.ds(s_idx, seg, stride=SUBLANES), :] = (
                h[s_idx * seg:(s_idx + 1) * seg, c * LANES:(c + 1) * LANES])

    @pl.when(i == 0)
    def _():
        cr_ref[...] = jnp.zeros(cr_ref.shape, F32)
        ci_ref[...] = jnp.zeros(ci_ref.shape, F32)

    zeros = jnp.zeros((SUBLANES, ns), F32)
    sub = lax.broadcasted_iota(jnp.int32, (SUBLANES, ns), 0)

    def project_in(gb, sr_ref, si_ref):
        hb = hp_ref[gb].astype(BF16)
        sr_ref[...] = _dot(hb, wbr_ref[gb])
        si_ref[...] = _dot(hb, wbi_ref[gb])

    def group_block(gb, sr_ref, si_ref, nxt_r_ref, nxt_i_ref):
        a_r = jnp.broadcast_to(ar_ref[gb], (SUBLANES, ns))
        a_i = jnp.broadcast_to(ai_ref[gb], (SUBLANES, ns))

        def step(j, carry, store):
            x_r, x_i = carry
            rows = pl.ds(pl.multiple_of(j * SUBLANES, SUBLANES), SUBLANES)
            n_r = a_r * x_r - a_i * x_i + sr_ref[rows, :]
            n_i = a_r * x_i + a_i * x_r + si_ref[rows, :]
            if store:
                sr_ref[rows, :] = n_r
                si_ref[rows, :] = n_i
            return n_r, n_i

        f_r, f_i = lax.fori_loop(0, seg, functools.partial(step, store=False), (zeros, zeros),
                                 unroll=4)
        p_r, p_i = pr_ref[gb], pi_ref[gb]
        row_r, row_i = cr_ref[gb], ci_ref[gb]
        init_r, init_i = zeros, zeros
        for s_idx in range(SUBLANES):
            init_r = jnp.where(sub == s_idx, jnp.broadcast_to(row_r, (SUBLANES, ns)), init_r)
            init_i = jnp.where(sub == s_idx, jnp.broadcast_to(row_i, (SUBLANES, ns)), init_i)
            if s_idx + 1 < SUBLANES:
                row_r, row_i = (p_r * row_r - p_i * row_i + f_r[s_idx:s_idx + 1, :],
                                p_r * row_i + p_i * row_r + f_i[s_idx:s_idx + 1, :])
        l_r, l_i = lax.fori_loop(0, seg, functools.partial(step, store=True), (init_r, init_i),
                                 unroll=4)
        cr_ref[gb] = l_r[SUBLANES - 1:SUBLANES, :]
        ci_ref[gb] = l_i[SUBLANES - 1:SUBLANES, :]
        y_ref[gb] = (_dot(sr_ref[...].astype(BF16), wcr_ref[gb])
                     + _dot(si_ref[...].astype(BF16), wci_ref[gb]))
        project_in(jnp.minimum(gb + 1, nblk - 1), nxt_r_ref, nxt_i_ref)

    def block_pair(p, _):
        group_block(2 * p, sra_ref, sia_ref, srb_ref, sib_ref)
        group_block(2 * p + 1, srb_ref, sib_ref, sra_ref, sia_ref)
        return 0

    assert nblk % 2 == 0
    project_in(0, sra_ref, sia_ref)
    lax.fori_loop(0, nblk // 2, block_pair, 0)

    for c in range(nblk):
        lanes = slice(c * LANES, (c + 1) * LANES)
        y_ref[c] = jax.nn.gelu(y_ref[c] + d_ref[:, lanes] * hp_ref[c])
        for s_idx in range(SUBLANES):
            o_ref[s_idx * seg:(s_idx + 1) * seg, lanes] = (
                y_ref[c, pl.ds(s_idx, seg, stride=SUBLANES), :].astype(BF16))


def _ssm_core(x, gain, d_skip, wb_r, wb_i, wc_r, wc_i, ab_r, ab_i, ap_r, ap_i):
    s, d = x.shape
    t = min(SSM_ROW_TILE, s)
    nblk, cb, ns = wb_r.shape

    def whole(a):
        return pl.BlockSpec(a.shape, lambda i: (0,) * a.ndim)

    return pl.pallas_call(
        _ssm_kernel,
        grid=(s // t,),
        in_specs=[
            pl.BlockSpec((t, d), lambda i: (i, 0)),
            whole(gain), whole(d_skip),
            whole(wb_r), whole(wb_i), whole(wc_r), whole(wc_i),
            whole(ab_r), whole(ab_i), whole(ap_r), whole(ap_i),
        ],
        out_specs=pl.BlockSpec((t, d), lambda i: (i, 0)),
        out_shape=jax.ShapeDtypeStruct((s, d), BF16),
        scratch_shapes=[
            pltpu.VMEM((nblk, t, LANES), F32),
            pltpu.VMEM((nblk, t, LANES), F32),
            pltpu.VMEM((t, ns), F32),
            pltpu.VMEM((t, ns), F32),
            pltpu.VMEM((t, ns), F32),
            pltpu.VMEM((t, ns), F32),
            pltpu.VMEM((nblk, 1, ns), F32),
            pltpu.VMEM((nblk, 1, ns), F32),
        ],
        compiler_params=_params("arbitrary"),
        name="ssm_core",
    )(x, gain, d_skip, wb_r, wb_i, wc_r, wc_i, ab_r, ab_i, ap_r, ap_i)


def _block_diag(w, groups_per_block):
    g, r, c = w.shape
    nblk = g // groups_per_block
    tiled = jnp.tile(w.reshape(nblk, groups_per_block * r, c), (1, 1, groups_per_block))
    row_group = lax.broadcasted_iota(jnp.int32, tiled.shape, 1) // r
    col_group = lax.broadcasted_iota(jnp.int32, tiled.shape, 2) // c
    return jnp.where(row_group == col_group, tiled, jnp.zeros_like(tiled))


def _s5_mixer(x, gain, a_re, a_im, log_dt, b_re, b_im, c_re, c_im, d_skip, w_glu, layer):
    s, d = x.shape
    g, n = a_re.shape
    gpb = SSM_GROUPS_PER_BLOCK
    seg = min(SSM_ROW_TILE, s) // SUBLANES
    n_squarings = int(math.log2(seg))
    assert 2 ** n_squarings == seg
    bt_re, bt_im = b_re.transpose(2, 0, 1), b_im.transpose(2, 0, 1)
    ab_r, ab_i, ap_r, ap_i, bb_r, bb_i = _ssm_prep(a_re, a_im, log_dt.reshape(g, 1), bt_re, bt_im,
                                                   n_squarings)
    wb_r = _block_diag(bb_r.transpose(1, 0, 2), gpb).astype(BF16)
    wb_i = _block_diag(bb_i.transpose(1, 0, 2), gpb).astype(BF16)
    wc_r = _block_diag(c_re.transpose(0, 2, 1), gpb).astype(BF16)
    wc_i = _block_diag(-c_im.transpose(0, 2, 1), gpb).astype(BF16)
    flat = lambda a: a.reshape(g // gpb, 1, gpb * n)
    z = _ssm_core(x, gain, d_skip.reshape(1, d), wb_r, wb_i, wc_r, wc_i,
                  flat(ab_r), flat(ab_i), flat(ap_r), flat(ap_i))
    return _glu_res(z, w_glu, x, layer)


def kernel(x, norm_mix, norm_mlp, mlp_w1, mlp_w2, conv_w_in, conv_w, conv_w_out, pool_w_in, pool_w_group, pool_scale, att_w_qkv, att_q_norm, att_k_norm, att_rel_bias, att_w_out, ssm_a_re, ssm_a_im, ssm_log_dt, ssm_b_re, ssm_b_im, ssm_c_re, ssm_c_im, ssm_d, ssm_w_glu):
    b, s, d = x.shape
    depth = norm_mix.shape[0]
    n_mixers = 4
    outs = []
    for bi in range(b):
        xs = x[bi]
        for i in range(depth):
            kind, j = i % n_mixers, i // n_mixers
            gain = norm_mix[i].reshape(1, d)
            if kind == 0:
                gated = _conv_front(xs, gain, conv_w_in, conv_w[j], j)
                xs = _proj_res(gated, conv_w_out, xs, j)
            elif kind == 1:
                xs = _pool_mixer(xs, gain, pool_w_in, pool_w_group, pool_scale[j].reshape(1, d), j)
            elif kind == 2:
                n_heads = d // ATT_HEAD_DIM
                head_gains = jnp.concatenate([jnp.tile(att_q_norm[j], n_heads),
                                              jnp.tile(att_k_norm[j], n_heads)]).reshape(1, 2 * d)
                qkv = _qkv_proj(xs, gain, att_w_qkv, head_gains, j)
                att = _attention(qkv, _attention_profile(att_rel_bias[j]), n_heads)
                xs = _proj_res(att, att_w_out, xs, j)
            else:
                xs = _s5_mixer(xs, gain, ssm_a_re[j], ssm_a_im[j], ssm_log_dt[j], ssm_b_re[j],
                               ssm_b_im[j], ssm_c_re[j], ssm_c_im[j], ssm_d[j], ssm_w_glu, j)
            xs = _mlp(xs, norm_mlp[i].reshape(1, d), mlp_w1, mlp_w2, i)
        outs.append(xs)
    return outs[0][None] if b == 1 else jnp.stack(outs)
```

```python
import functools
import math

import jax
import jax.numpy as jnp
from jax import lax
from jax.experimental import pallas as pl
from jax.experimental.pallas import tpu as pltpu

F32 = jnp.float32
BF16 = jnp.bfloat16

RMS_EPS = 1e-6
CHUNK = 64
ATT_HEAD_DIM = 128
ATT_LEFT_CHUNKS = 8
REL_CLIP = 256
MASK_VALUE = -1e30
POOL_WINDOWS = (2, 4, 8, 16)
CONV_WIDTH = 3
SSM_GROUP = 16
SSM_STATE = 64

SUBLANES = 8
LANES = 128
VMEM_LIMIT_BYTES = 56 * 1024 * 1024

ROW_TILE = 1024
COL_TILE = 512
QKV_COL_TILE = 1024
PROJ_ROW_TILE = 512
PROJ_COL_TILE = 2048
GLU_COL_TILE = 1024
FF_TILE = 512
MLP_OUT_TILE = 512
MLP_K_TILE = 2048
ATT_QBLOCK = 256
ATT_HEADS_PER_STEP = 8
SSM_ROW_TILE = 512
SSM_GROUPS_PER_BLOCK = 8
HALO = 16
POOL_HALO = 32


def _params(*semantics):
    return pltpu.CompilerParams(dimension_semantics=semantics, vmem_limit_bytes=VMEM_LIMIT_BYTES)


def _rms_norm(x, gain):
    return x * lax.rsqrt(jnp.mean(x * x, axis=-1, keepdims=True) + RMS_EPS) * gain


def _dot(a, b):
    return jnp.dot(a, b, preferred_element_type=F32)


def _wdot(a, w_ref):
    return _dot(a, w_ref[...].astype(BF16))


def _mlp_kernel(x_hbm, g_ref, w1_hbm, w2_hbm, xr_ref, o_ref, x_buf, h_ref, a_ref, w1_buf, w2_buf,
                semx, sem1, sem2, *, layer, n_rows):
    i, t = pl.program_id(0), pl.program_id(1)
    n_out = pl.num_programs(1) - 1
    tf, tk, tn = w1_buf.shape[2], w2_buf.shape[1], w2_buf.shape[2]
    n_up, n_k = a_ref.shape[1] // tf, a_ref.shape[1] // tk
    assert n_up % 2 == 0 and n_k % 2 == 0

    def w1_copy(f, slot):
        return pltpu.make_async_copy(w1_hbm.at[layer, :, pl.ds(f * tf, tf)], w1_buf.at[slot],
                                     sem1.at[slot])

    def w2_copy(n, kh):
        return pltpu.make_async_copy(w2_hbm.at[layer, pl.ds(kh * tk, tk), pl.ds(n * tn, tn)],
                                     w2_buf.at[kh % 2], sem2.at[kh % 2])

    def x_copy(row):
        tm = x_buf.shape[0]
        return pltpu.make_async_copy(x_hbm.at[pl.ds(row * tm, tm), :], x_buf, semx.at[0])

    @pl.when(t == 0)
    def _():
        @pl.when(i == 0)
        def _():
            x_copy(0).start()
            w1_copy(0, 0).start()

        w2_copy(0, 0).start()
        w2_copy(0, 1).start()
        x_copy(i).wait()
        h_ref[...] = _rms_norm(x_buf[...], g_ref[...]).astype(BF16)

        @pl.when(i + 1 < n_rows)
        def _():
            x_copy(i + 1).start()

        def up_pair(p, _):
            for slot in (0, 1):
                f = 2 * p + slot
                w1_copy(f, slot).wait()

                @pl.when(f + 1 < n_up)
                def _():
                    w1_copy(f + 1, 1 - slot).start()

                a = jnp.maximum(_dot(h_ref[...], w1_buf[slot].astype(BF16)), 0.0)
                a_ref[:, pl.ds(pl.multiple_of(f * tf, tf), tf)] = (a * a).astype(BF16)
            return 0

        lax.fori_loop(0, n_up // 2, up_pair, 0)

    @pl.when(t > 0)
    def _():
        n = t - 1
        for kh in range(n_k):
            w2_copy(n, kh).wait()
            part = _dot(a_ref[:, kh * tk:(kh + 1) * tk], w2_buf[kh % 2].astype(BF16))
            if kh == 0:
                o_ref[...] = xr_ref[...] + part
            else:
                o_ref[...] += part
            if kh + 2 < n_k:
                w2_copy(n, kh + 2).start()
            else:
                @pl.when(n + 1 < n_out)
                def _(kh=kh):
                    w2_copy(n + 1, kh + 2 - n_k).start()

        @pl.when((n + 1 == n_out) & (i + 1 < n_rows))
        def _():
            w1_copy(0, 0).start()


def _mlp(x, gain, w1, w2, layer):
    s, d = x.shape
    dff = w1.shape[2]
    tm, tf, tn, tk = min(ROW_TILE, s), FF_TILE, MLP_OUT_TILE, MLP_K_TILE
    n_rows, n_out = s // tm, d // tn

    def out_tile(t):
        return jnp.maximum(t - 1, 0)

    return pl.pallas_call(
        functools.partial(_mlp_kernel, layer=layer, n_rows=n_rows),
        grid=(n_rows, 1 + n_out),
        in_specs=[
            pl.BlockSpec(memory_space=pl.ANY),
            pl.BlockSpec((1, d), lambda i, t: (0, 0)),
            pl.BlockSpec(memory_space=pl.ANY),
            pl.BlockSpec(memory_space=pl.ANY),
            pl.BlockSpec((tm, tn), lambda i, t: (i, out_tile(t))),
        ],
        out_specs=pl.BlockSpec((tm, tn), lambda i, t: (i, out_tile(t))),
        out_shape=jax.ShapeDtypeStruct((s, d), F32),
        scratch_shapes=[
            pltpu.VMEM((tm, d), F32),
            pltpu.VMEM((tm, d), BF16),
            pltpu.VMEM((tm, dff), BF16),
            pltpu.VMEM((2, d, tf), F32),
            pltpu.VMEM((2, tk, tn), F32),
            pltpu.SemaphoreType.DMA((1,)),
            pltpu.SemaphoreType.DMA((2,)),
            pltpu.SemaphoreType.DMA((2,)),
        ],
        compiler_params=_params("arbitrary", "arbitrary"),
        name="mlp",
    )(x, gain, w1, w2, x)


def _proj_res_kernel(a_ref, w_ref, x_ref, o_ref, wb_ref):
    @pl.when(pl.program_id(1) == 0)
    def _():
        wb_ref[...] = w_ref[...].astype(BF16)

    o_ref[...] = x_ref[...] + _dot(a_ref[...], wb_ref[...])


def _proj_res(a, w, x, layer):
    s, k = a.shape
    n = w.shape[2]
    tm, tn = min(PROJ_ROW_TILE, s), min(PROJ_COL_TILE, n)
    return pl.pallas_call(
        _proj_res_kernel,
        grid=(n // tn, s // tm),
        in_specs=[
            pl.BlockSpec((tm, k), lambda j, i: (i, 0)),
            pl.BlockSpec((None, k, tn), lambda j, i: (layer, 0, j), pipeline_mode=pl.Buffered(1)),
            pl.BlockSpec((tm, tn), lambda j, i: (i, j)),
        ],
        out_specs=pl.BlockSpec((tm, tn), lambda j, i: (i, j)),
        out_shape=jax.ShapeDtypeStruct((s, n), F32),
        scratch_shapes=[pltpu.VMEM((k, tn), BF16)],
        compiler_params=_params("arbitrary", "arbitrary"),
        name="proj_res",
    )(a, w, x)


def _glu_res_kernel(a_ref, wv_ref, wg_ref, x_ref, o_ref, wvb_ref, wgb_ref):
    @pl.when(pl.program_id(1) == 0)
    def _():
        wvb_ref[...] = wv_ref[...].astype(BF16)
        wgb_ref[...] = wg_ref[...].astype(BF16)

    a = a_ref[...]
    val = _dot(a, wvb_ref[...])
    gate = _dot(a, wgb_ref[...])
    o_ref[...] = x_ref[...] + val * jax.nn.sigmoid(gate)


def _glu_res(a, w_glu, x, layer):
    s, k = a.shape
    n = w_glu.shape[2] // 2
    tm, tn = min(PROJ_ROW_TILE, s), min(GLU_COL_TILE, n)
    nj = n // tn
    return pl.pallas_call(
        _glu_res_kernel,
        grid=(nj, s // tm),
        in_specs=[
            pl.BlockSpec((tm, k), lambda j, i: (i, 0)),
            pl.BlockSpec((None, k, tn), lambda j, i: (layer, 0, j), pipeline_mode=pl.Buffered(1)),
            pl.BlockSpec((None, k, tn), lambda j, i: (layer, 0, j + nj), pipeline_mode=pl.Buffered(1)),
            pl.BlockSpec((tm, tn), lambda j, i: (i, j)),
        ],
        out_specs=pl.BlockSpec((tm, tn), lambda j, i: (i, j)),
        out_shape=jax.ShapeDtypeStruct((s, n), F32),
        scratch_shapes=[pltpu.VMEM((k, tn), BF16), pltpu.VMEM((k, tn), BF16)],
        compiler_params=_params("arbitrary", "arbitrary"),
        name="glu_res",
    )(a, w_glu, w_glu, x)


def _conv_front_kernel(x_ref, g_ref, wb_ref, wc_ref, wv_ref, cw_ref, o_ref, h_ref, ext_ref, carry_ref):
    i, j = pl.program_id(0), pl.program_id(1)
    tm = o_ref.shape[0]

    @pl.when(j == 0)
    def _():
        h_ref[...] = _rms_norm(x_ref[...], g_ref[...]).astype(BF16)

    @pl.when(i == 0)
    def _():
        carry_ref[j] = jnp.zeros(carry_ref.shape[1:], F32)

    h = h_ref[...]
    u = _wdot(h, wc_ref) * _wdot(h, wv_ref)
    ext_ref[0:HALO, :] = carry_ref[j]
    ext_ref[HALO:, :] = u
    carry_ref[j] = u[tm - HALO:, :]
    conv = cw_ref[CONV_WIDTH - 1:CONV_WIDTH, :] * u
    for lag in range(1, CONV_WIDTH):
        tap = CONV_WIDTH - 1 - lag
        conv += cw_ref[tap:tap + 1, :] * ext_ref[pl.ds(HALO - lag, tm), :]
    o_ref[...] = (_wdot(h, wb_ref) * conv).astype(BF16)


def _conv_front(x, gain, w_in, conv_w, layer):
    s, d = x.shape
    tm, tn = min(ROW_TILE, s), COL_TILE
    nj = d // tn
    return pl.pallas_call(
        _conv_front_kernel,
        grid=(s // tm, nj),
        in_specs=[
            pl.BlockSpec((tm, d), lambda i, j: (i, 0)),
            pl.BlockSpec((1, d), lambda i, j: (0, 0)),
            pl.BlockSpec((None, d, tn), lambda i, j: (layer, 0, j)),
            pl.BlockSpec((None, d, tn), lambda i, j: (layer, 0, j + nj)),
            pl.BlockSpec((None, d, tn), lambda i, j: (layer, 0, j + 2 * nj)),
            pl.BlockSpec((CONV_WIDTH, tn), lambda i, j: (0, j)),
        ],
        out_specs=pl.BlockSpec((tm, tn), lambda i, j: (i, j)),
        out_shape=jax.ShapeDtypeStruct((s, d), BF16),
        scratch_shapes=[
            pltpu.VMEM((tm, d), BF16),
            pltpu.VMEM((tm + HALO, tn), F32),
            pltpu.VMEM((nj, HALO, tn), F32),
        ],
        compiler_params=_params("arbitrary", "arbitrary"),
        name="conv_front",
    )(x, gain, w_in, w_in, w_in, conv_w)


def _pool_kernel(x_ref, g_ref, win_ref, wg_ref, sc_ref, xr_ref, o_ref, h_ref, ext_ref, la_ref, lb_ref,
                 carry_ref, winb_ref, wgb_ref):
    i, grp = pl.program_id(0), pl.program_id(1)
    tm = o_ref.shape[0]
    halo = POOL_HALO
    n_ext = tm + halo

    @pl.when(grp == 0)
    def _():
        h_ref[...] = _rms_norm(x_ref[...], g_ref[...]).astype(BF16)

    @pl.when(i == 0)
    def _():
        carry_ref[grp] = jnp.zeros(carry_ref.shape[1:], F32)
        winb_ref[grp] = win_ref[...].astype(BF16)
        wgb_ref[grp] = wg_ref[0].astype(BF16)

    u = _dot(h_ref[...], winb_ref[grp])
    ext_ref[0:halo, :] = carry_ref[grp]
    ext_ref[halo:, :] = u
    carry_ref[grp] = u[tm - halo:, :]
    pos = (i * tm + 1 + lax.broadcasted_iota(jnp.int32, (tm, 1), 0)).astype(F32)

    for gi, w in enumerate(POOL_WINDOWS):
        @pl.when(grp == gi)
        def _(w=w):
            levels = int(math.log2(w))
            assert 2 ** levels == w and SUBLANES * levels <= halo
            src = ext_ref
            for lv in range(1, levels):
                dst = la_ref if lv % 2 else lb_ref
                lo, shift = SUBLANES * lv, 2 ** (lv - 1)
                dst[lo:, :] = src[lo:, :] + src[pl.ds(lo - shift, n_ext - lo), :]
                src = dst
            acc = src[halo:, :] + src[pl.ds(halo - w // 2, tm), :]
            inv_count = 1.0 / jnp.minimum(pos, float(w))
            pooled = (acc * inv_count - u).astype(BF16)
            o_ref[...] = xr_ref[...] + _dot(pooled, wgb_ref[gi]) * sc_ref[...]


def _pool_mixer(x, gain, w_in, w_group, scale, layer):
    s, d = x.shape
    ng, pg = w_group.shape[1], w_group.shape[2]
    tm = min(ROW_TILE, s)

    def first_tile_only(i, g):
        return jnp.where(i == 0, g, ng - 1)

    return pl.pallas_call(
        _pool_kernel,
        grid=(s // tm, ng),
        in_specs=[
            pl.BlockSpec((tm, d), lambda i, g: (i, 0)),
            pl.BlockSpec((1, d), lambda i, g: (0, 0)),
            pl.BlockSpec((None, d, pg), lambda i, g: (layer, 0, first_tile_only(i, g)),
                         pipeline_mode=pl.Buffered(1)),
            pl.BlockSpec((None, 1, pg, pg), lambda i, g: (layer, first_tile_only(i, g), 0, 0)),
            pl.BlockSpec((1, pg), lambda i, g: (0, g)),
            pl.BlockSpec((tm, pg), lambda i, g: (i, g)),
        ],
        out_specs=pl.BlockSpec((tm, pg), lambda i, g: (i, g)),
        out_shape=jax.ShapeDtypeStruct((s, d), F32),
        scratch_shapes=[
            pltpu.VMEM((tm, d), BF16),
            pltpu.VMEM((tm + POOL_HALO, pg), F32),
            pltpu.VMEM((tm + POOL_HALO, pg), F32),
            pltpu.VMEM((tm + POOL_HALO, pg), F32),
            pltpu.VMEM((ng, POOL_HALO, pg), F32),
            pltpu.VMEM((ng, d, pg), BF16),
            pltpu.VMEM((ng, pg, pg), BF16),
        ],
        compiler_params=_params("arbitrary", "arbitrary"),
        name="pool_mixer",
    )(x, gain, w_in, w_group, scale, x)


def _qkv_kernel(x_ref, g_ref, w_ref, hg_ref, o_ref, h_ref, *, n_norm_tiles):
    j = pl.program_id(1)

    @pl.when(j == 0)
    def _():
        h_ref[...] = _rms_norm(x_ref[...], g_ref[...]).astype(BF16)

    y = _wdot(h_ref[...], w_ref)

    @pl.when(j < n_norm_tiles)
    def _():
        for hd in range(y.shape[1] // ATT_HEAD_DIM):
            sl = slice(hd * ATT_HEAD_DIM, (hd + 1) * ATT_HEAD_DIM)
            o_ref[:, sl] = _rms_norm(y[:, sl], hg_ref[:, sl]).astype(BF16)

    @pl.when(j >= n_norm_tiles)
    def _():
        o_ref[...] = y.astype(BF16)


def _qkv_proj(x, gain, w_qkv, head_gains, layer):
    s, d = x.shape
    n = w_qkv.shape[2]
    tm, tn = min(ROW_TILE, s), QKV_COL_TILE
    n_norm_tiles = head_gains.shape[1] // tn
    return pl.pallas_call(
        functools.partial(_qkv_kernel, n_norm_tiles=n_norm_tiles),
        grid=(s // tm, n // tn),
        in_specs=[
            pl.BlockSpec((tm, d), lambda i, j: (i, 0)),
            pl.BlockSpec((1, d), lambda i, j: (0, 0)),
            pl.BlockSpec((None, d, tn), lambda i, j: (layer, 0, j)),
            pl.BlockSpec((1, tn), lambda i, j: (0, jnp.minimum(j, n_norm_tiles - 1))),
        ],
        out_specs=pl.BlockSpec((tm, tn), lambda i, j: (i, j)),
        out_shape=jax.ShapeDtypeStruct((s, n), BF16),
        scratch_shapes=[pltpu.VMEM((tm, d), BF16)],
        compiler_params=_params("parallel", "arbitrary"),
        name="qkv_proj",
    )(x, gain, w_qkv, head_gains)


def _attn_kernel(q_ref, k0_ref, k1_ref, k2_ref, v0_ref, v1_ref, v2_ref, prof_ref, o_ref, bias_ref):
    b = pl.program_id(1)
    qb = q_ref.shape[0]
    n_local = q_ref.shape[1] // ATT_HEAD_DIM
    log2e = math.log2(math.e)
    scale = ATT_HEAD_DIM ** -0.5 * log2e

    @pl.when(b == 0)
    def _():
        q_idx = lax.broadcasted_iota(jnp.int32, (qb, 3 * qb), 0) + 2 * qb
        k_idx = lax.broadcasted_iota(jnp.int32, (qb, 3 * qb), 1)
        chunk_start = (q_idx // CHUNK) * CHUNK
        in_band = (k_idx >= chunk_start - ATT_LEFT_CHUNKS * CHUNK) & (k_idx < chunk_start + CHUNK)
        for hh in range(n_local):
            profile = jnp.broadcast_to(prof_ref[hh], (qb, prof_ref.shape[2]))
            table = pltpu.roll(profile, 0, 1, stride=1, stride_axis=0)[:, :3 * qb]
            bias_ref[hh] = jnp.where(in_band, table * log2e, MASK_VALUE)

    def lane_slabs(a):
        return [a[:, k * LANES:(k + 1) * LANES] for k in range(a.shape[1] // LANES)]

    def heads(first_blocks):
        for hh in range(n_local):
            cols = slice(hh * ATT_HEAD_DIM, (hh + 1) * ATT_HEAD_DIM)
            q = q_ref[:, cols]
            scores = []
            for c, k_ref in enumerate((k0_ref, k1_ref, k2_ref)):
                s_c = lax.dot_general(q, k_ref[:, cols], (((1,), (1,)), ((), ())),
                                      preferred_element_type=F32)
                s_c = s_c * scale + bias_ref[hh, :, c * qb:(c + 1) * qb]
                if first_blocks and c < 2:
                    s_c = jnp.where(b - 2 + c >= 0, s_c, MASK_VALUE)
                scores.append(s_c)
            m = jnp.max(functools.reduce(jnp.maximum,
                                         [sl for s_c in scores for sl in lane_slabs(s_c)]),
                        axis=-1, keepdims=True)
            acc = jnp.zeros((qb, 2 * ATT_HEAD_DIM), F32)
            ones = jnp.ones((qb, ATT_HEAD_DIM), BF16)
            for s_c, v_ref in zip(scores, (v0_ref, v1_ref, v2_ref)):
                p = jnp.exp2((s_c - m).astype(BF16))
                acc += _dot(p, jnp.concatenate([v_ref[:, cols], ones], axis=1))
            o_ref[:, cols] = (acc[:, :ATT_HEAD_DIM] / acc[:, ATT_HEAD_DIM:]).astype(BF16)

    @pl.when(b < 2)
    def _():
        heads(True)

    @pl.when(b >= 2)
    def _():
        heads(False)


def _attention(qkv, profile, n_heads):
    s = qkv.shape[0]
    qb = ATT_QBLOCK
    nb = s // qb
    hps = ATT_HEADS_PER_STEP
    width = hps * ATT_HEAD_DIM
    ngroups = n_heads // hps

    def kv_spec(offset, back):
        return pl.BlockSpec((qb, width), lambda h, b: (jnp.maximum(b - back, 0), offset + h))

    return pl.pallas_call(
        _attn_kernel,
        grid=(ngroups, nb),
        in_specs=[
            pl.BlockSpec((qb, width), lambda h, b: (b, h)),
            kv_spec(ngroups, 2), kv_spec(ngroups, 1), kv_spec(ngroups, 0),
            kv_spec(2 * ngroups, 2), kv_spec(2 * ngroups, 1), kv_spec(2 * ngroups, 0),
            pl.BlockSpec((hps, 1, 4 * qb), lambda h, b: (h, 0, 0)),
        ],
        out_specs=pl.BlockSpec((qb, width), lambda h, b: (b, h)),
        out_shape=jax.ShapeDtypeStruct((s, n_heads * ATT_HEAD_DIM), BF16),
        scratch_shapes=[pltpu.VMEM((hps, qb, 3 * qb), F32)],
        compiler_params=_params("arbitrary", "arbitrary"),
        name="chunk_attention",
    )(qkv, qkv, qkv, qkv, qkv, qkv, qkv, profile)


def _attention_profile(rel_bias):
    qb = ATT_QBLOCK
    n_heads = rel_bias.shape[0]
    assert REL_CLIP <= 2 * qb and qb <= REL_CLIP + 1 and 2 * REL_CLIP + 1 == rel_bias.shape[1]
    n_flat = 2 * qb - REL_CLIP + 1
    top = rel_bias[:, 2 * REL_CLIP:]
    profile = jnp.concatenate([
        jnp.broadcast_to(top, (n_heads, n_flat)),
        jnp.flip(rel_bias[:, REL_CLIP - qb + 1:2 * REL_CLIP], axis=1),
        jnp.broadcast_to(top, (n_heads, qb)),
    ], axis=1)
    return profile.reshape(n_heads, 1, 4 * qb).astype(F32)


def _ssm_prep_kernel(are_ref, aim_ref, ldt_ref, btr_ref, bti_ref,
                     abr_ref, abi_ref, apr_ref, api_ref, bbr_ref, bbi_ref, *, n_squarings):
    lam_r, lam_i = are_ref[...], aim_ref[...]
    dt = jnp.exp(ldt_ref[...])
    mag = jnp.exp(lam_r * dt)
    ab_r, ab_i = mag * jnp.cos(lam_i * dt), mag * jnp.sin(lam_i * dt)
    abr_ref[...] = ab_r
    abi_ref[...] = ab_i
    p_r, p_i = ab_r, ab_i
    for _ in range(n_squarings):
        p_r, p_i = p_r * p_r - p_i * p_i, 2.0 * p_r * p_i
    apr_ref[...] = p_r
    api_ref[...] = p_i
    num_r, num_i = ab_r - 1.0, ab_i
    den = lam_r * lam_r + lam_i * lam_i
    co_r = (num_r * lam_r + num_i * lam_i) / den
    co_i = (num_i * lam_r - num_r * lam_i) / den
    for c in range(btr_ref.shape[0]):
        b_r, b_i = btr_ref[c], bti_ref[c]
        bbr_ref[c] = co_r * b_r - co_i * b_i
        bbi_ref[c] = co_r * b_i + co_i * b_r


def _ssm_prep(a_re, a_im, log_dt, bt_re, bt_im, n_squarings):
    g, n = a_re.shape
    gn = jax.ShapeDtypeStruct((g, n), F32)
    cgn = jax.ShapeDtypeStruct(bt_re.shape, F32)
    return pl.pallas_call(
        functools.partial(_ssm_prep_kernel, n_squarings=n_squarings),
        out_shape=(gn, gn, gn, gn, cgn, cgn),
        name="ssm_prep",
    )(a_re, a_im, log_dt, bt_re, bt_im)


def _ssm_kernel(x_ref, g_ref, d_ref, wbr_ref, wbi_ref, wcr_ref, wci_ref,
                ar_ref, ai_ref, pr_ref, pi_ref, o_ref,
                hp_ref, y_ref, sra_ref, sia_ref, srb_ref, sib_ref, cr_ref, ci_ref):
    i = pl.program_id(0)
    t = x_ref.shape[0]
    seg = t // SUBLANES
    nblk, cb, ns = wbr_ref.shape
    assert cb == LANES and hp_ref.shape[0] == nblk

    h = _rms_norm(x_ref[...], g_ref[...])
    for c in range(nblk):
        for s_idx in range(SUBLANES):
            hp_ref[c, pl.ds(s_idx, seg, stride=SUBLANES), :] = (
                h[s_idx * seg:(s_idx + 1) * seg, c * LANES:(c + 1) * LANES])

    @pl.when(i == 0)
    def _():
        cr_ref[...] = jnp.zeros(cr_ref.shape, F32)
        ci_ref[...] = jnp.zeros(ci_ref.shape, F32)

    zeros = jnp.zeros((SUBLANES, ns), F32)
    sub = lax.broadcasted_iota(jnp.int32, (SUBLANES, ns), 0)

    def project_in(gb, sr_ref, si_ref):
        hb = hp_ref[gb].astype(BF16)
        sr_ref[...] = _dot(hb, wbr_ref[gb])
        si_ref[...] = _dot(hb, wbi_ref[gb])

    def group_block(gb, sr_ref, si_ref, nxt_r_ref, nxt_i_ref):
        a_r = jnp.broadcast_to(ar_ref[gb], (SUBLANES, ns))
        a_i = jnp.broadcast_to(ai_ref[gb], (SUBLANES, ns))

        def step(j, carry, store):
            x_r, x_i = carry
            rows = pl.ds(pl.multiple_of(j * SUBLANES, SUBLANES), SUBLANES)
            n_r = a_r * x_r - a_i * x_i + sr_ref[rows, :]
            n_i = a_r * x_i + a_i * x_r + si_ref[rows, :]
            if store:
                sr_ref[rows, :] = n_r
                si_ref[rows, :] = n_i
            return n_r, n_i

        f_r, f_i = lax.fori_loop(0, seg, functools.partial(step, store=False), (zeros, zeros),
                                 unroll=4)
        p_r, p_i = pr_ref[gb], pi_ref[gb]
        row_r, row_i = cr_ref[gb], ci_ref[gb]
        init_r, init_i = zeros, zeros
        for s_idx in range(SUBLANES):
            init_r = jnp.where(sub == s_idx, jnp.broadcast_to(row_r, (SUBLANES, ns)), init_r)
            init_i = jnp.where(sub == s_idx, jnp.broadcast_to(row_i, (SUBLANES, ns)), init_i)
            if s_idx + 1 < SUBLANES:
                row_r, row_i = (p_r * row_r - p_i * row_i + f_r[s_idx:s_idx + 1, :],
                                p_r * row_i + p_i * row_r + f_i[s_idx:s_idx + 1, :])
        l_r, l_i = lax.fori_loop(0, seg, functools.partial(step, store=True), (init_r, init_i),
                                 unroll=4)
        cr_ref[gb] = l_r[SUBLANES - 1:SUBLANES, :]
        ci_ref[gb] = l_i[SUBLANES - 1:SUBLANES, :]
        y_ref[gb] = (_dot(sr_ref[...].astype(BF16), wcr_ref[gb])
                     + _dot(si_ref[...].astype(BF16), wci_ref[gb]))
        project_in(jnp.minimum(gb + 1, nblk - 1), nxt_r_ref, nxt_i_ref)

    def block_pair(p, _):
        group_block(2 * p, sra_ref, sia_ref, srb_ref, sib_ref)
        group_block(2 * p + 1, srb_ref, sib_ref, sra_ref, sia_ref)
        return 0

    assert nblk % 2 == 0
    project_in(0, sra_ref, sia_ref)
    lax.fori_loop(0, nblk // 2, block_pair, 0)

    for c in range(nblk):
        lanes = slice(c * LANES, (c + 1) * LANES)
        y_ref[c] = jax.nn.gelu(y_ref[c] + d_ref[:, lanes] * hp_ref[c])
        for s_idx in range(SUBLANES):
            o_ref[s_idx * seg:(s_idx + 1) * seg, lanes] = (
                y_ref[c, pl.ds(s_idx, seg, stride=SUBLANES), :].astype(BF16))


def _ssm_core(x, gain, d_skip, wb_r, wb_i, wc_r, wc_i, ab_r, ab_i, ap_r, ap_i):
    s, d = x.shape
    t = min(SSM_ROW_TILE, s)
    nblk, cb, ns = wb_r.shape

    def whole(a):
        return pl.BlockSpec(a.shape, lambda i: (0,) * a.ndim)

    return pl.pallas_call(
        _ssm_kernel,
        grid=(s // t,),
        in_specs=[
            pl.BlockSpec((t, d), lambda i: (i, 0)),
            whole(gain), whole(d_skip),
            whole(wb_r), whole(wb_i), whole(wc_r), whole(wc_i),
            whole(ab_r), whole(ab_i), whole(ap_r), whole(ap_i),
        ],
        out_specs=pl.BlockSpec((t, d), lambda i: (i, 0)),
        out_shape=jax.ShapeDtypeStruct((s, d), BF16),
        scratch_shapes=[
            pltpu.VMEM((nblk, t, LANES), F32),
            pltpu.VMEM((nblk, t, LANES), F32),
            pltpu.VMEM((t, ns), F32),
            pltpu.VMEM((t, ns), F32),
            pltpu.VMEM((t, ns), F32),
            pltpu.VMEM((t, ns), F32),
            pltpu.VMEM((nblk, 1, ns), F32),
            pltpu.VMEM((nblk, 1, ns), F32),
        ],
        compiler_params=_params("arbitrary"),
        name="ssm_core",
    )(x, gain, d_skip, wb_r, wb_i, wc_r, wc_i, ab_r, ab_i, ap_r, ap_i)


def _block_diag(w, groups_per_block):
    g, r, c = w.shape
    nblk = g // groups_per_block
    tiled = jnp.tile(w.reshape(nblk, groups_per_block * r, c), (1, 1, groups_per_block))
    row_group = lax.broadcasted_iota(jnp.int32, tiled.shape, 1) // r
    col_group = lax.broadcasted_iota(jnp.int32, tiled.shape, 2) // c
    return jnp.where(row_group == col_group, tiled, jnp.zeros_like(tiled))


def _s5_mixer(x, gain, a_re, a_im, log_dt, b_re, b_im, c_re, c_im, d_skip, w_glu, layer):
    s, d = x.shape
    g, n = a_re.shape
    gpb = SSM_GROUPS_PER_BLOCK
    seg = min(SSM_ROW_TILE, s) // SUBLANES
    n_squarings = int(math.log2(seg))
    assert 2 ** n_squarings == seg
    bt_re, bt_im = b_re.transpose(2, 0, 1), b_im.transpose(2, 0, 1)
    ab_r, ab_i, ap_r, ap_i, bb_r, bb_i = _ssm_prep(a_re, a_im, log_dt.reshape(g, 1), bt_re, bt_im,
                                                   n_squarings)
    wb_r = _block_diag(bb_r.transpose(1, 0, 2), gpb).astype(BF16)
    wb_i = _block_diag(bb_i.transpose(1, 0, 2), gpb).astype(BF16)
    wc_r = _block_diag(c_re.transpose(0, 2, 1), gpb).astype(BF16)
    wc_i = _block_diag(-c_im.transpose(0, 2, 1), gpb).astype(BF16)
    flat = lambda a: a.reshape(g // gpb, 1, gpb * n)
    z = _ssm_core(x, gain, d_skip.reshape(1, d), wb_r, wb_i, wc_r, wc_i,
                  flat(ab_r), flat(ab_i), flat(ap_r), flat(ap_i))
    return _glu_res(z, w_glu, x, layer)


def kernel(x, norm_mix, norm_mlp, mlp_w1, mlp_w2, conv_w_in, conv_w, conv_w_out, pool_w_in, pool_w_group, pool_scale, att_w_qkv, att_q_norm, att_k_norm, att_rel_bias, att_w_out, ssm_a_re, ssm_a_im, ssm_log_dt, ssm_b_re, ssm_b_im, ssm_c_re, ssm_c_im, ssm_d, ssm_w_glu):
    b, s, d = x.shape
    depth = norm_mix.shape[0]
    n_mixers = 4
    outs = []
    for bi in range(b):
        xs = x[bi]
        for i in range(depth):
            kind, j = i % n_mixers, i // n_mixers
            gain = norm_mix[i].reshape(1, d)
            if kind == 0:
                gated = _conv_front(xs, gain, conv_w_in, conv_w[j], j)
                xs = _proj_res(gated, conv_w_out, xs, j)
            elif kind == 1:
                xs = _pool_mixer(xs, gain, pool_w_in, pool_w_group, pool_scale[j].reshape(1, d), j)
            elif kind == 2:
                n_heads = d // ATT_HEAD_DIM
                head_gains = jnp.concatenate([jnp.tile(att_q_norm[j], n_heads),
                                              jnp.tile(att_k_norm[j], n_heads)]).reshape(1, 2 * d)
                qkv = _qkv_proj(xs, gain, att_w_qkv, head_gains, j)
                att = _attention(qkv, _attention_profile(att_rel_bias[j]), n_heads)
                xs = _proj_res(att, att_w_out, xs, j)
            else:
                xs = _s5_mixer(xs, gain, ssm_a_re[j], ssm_a_im[j], ssm_log_dt[j], ssm_b_re[j],
                               ssm_b_im[j], ssm_c_re[j], ssm_c_im[j], ssm_d[j], ssm_w_glu, j)
            xs = _mlp(xs, norm_mlp[i].reshape(1, d), mlp_w1, mlp_w2, i)
        outs.append(xs)
    return outs[0][None] if b == 1 else jnp.stack(outs)
```

```python
import functools
import math

import jax
import jax.numpy as jnp
from jax import lax
from jax.experimental import pallas as pl
from jax.experimental.pallas import tpu as pltpu

F32 = jnp.float32
BF16 = jnp.bfloat16

RMS_EPS = 1e-6
CHUNK = 64
ATT_HEAD_DIM = 128
ATT_LEFT_CHUNKS = 8
REL_CLIP = 256
MASK_VALUE = -1e30
POOL_WINDOWS = (2, 4, 8, 16)
CONV_WIDTH = 3
SSM_GROUP = 16
SSM_STATE = 64

SUBLANES = 8
LANES = 128
VMEM_LIMIT_BYTES = 56 * 1024 * 1024

ROW_TILE = 1024
COL_TILE = 512
QKV_COL_TILE = 1024
PROJ_ROW_TILE = 512
PROJ_COL_TILE = 2048
GLU_COL_TILE = 1024
FF_TILE = 512
MLP_OUT_TILE = 512
MLP_K_TILE = 2048
ATT_QBLOCK = 256
ATT_HEADS_PER_STEP = 8
SSM_ROW_TILE = 256
SSM_PAIRS_PER_BATCH = 4
SSM_GROUPS_PER_BLOCK = 8
HALO = 16
POOL_HALO = 32


def _params(*semantics):
    return pltpu.CompilerParams(dimension_semantics=semantics, vmem_limit_bytes=VMEM_LIMIT_BYTES)


def _rms_norm(x, gain):
    return x * lax.rsqrt(jnp.mean(x * x, axis=-1, keepdims=True) + RMS_EPS) * gain


def _dot(a, b):
    return jnp.dot(a, b, preferred_element_type=F32)


def _wdot(a, w_ref):
    return _dot(a, w_ref[...].astype(BF16))


def _mlp_kernel(x_hbm, g_ref, w1_hbm, w2_hbm, xr_ref, o_ref, x_buf, h_ref, a_ref, w1_buf, w2_buf,
                semx, sem1, sem2, *, layer, n_rows):
    i, t = pl.program_id(0), pl.program_id(1)
    n_out = pl.num_programs(1) - 1
    tf, tk, tn = w1_buf.shape[2], w2_buf.shape[1], w2_buf.shape[2]
    n_up, n_k = a_ref.shape[1] // tf, a_ref.shape[1] // tk
    assert n_up % 2 == 0 and n_k % 2 == 0

    def w1_copy(f, slot):
        return pltpu.make_async_copy(w1_hbm.at[layer, :, pl.ds(f * tf, tf)], w1_buf.at[slot],
                                     sem1.at[slot])

    def w2_copy(n, kh):
        return pltpu.make_async_copy(w2_hbm.at[layer, pl.ds(kh * tk, tk), pl.ds(n * tn, tn)],
                                     w2_buf.at[kh % 2], sem2.at[kh % 2])

    def x_copy(row):
        tm = x_buf.shape[0]
        return pltpu.make_async_copy(x_hbm.at[pl.ds(row * tm, tm), :], x_buf, semx.at[0])

    @pl.when(t == 0)
    def _():
        @pl.when(i == 0)
        def _():
            x_copy(0).start()
            w1_copy(0, 0).start()

        w2_copy(0, 0).start()
        w2_copy(0, 1).start()
        x_copy(i).wait()
        h_ref[...] = _rms_norm(x_buf[...], g_ref[...]).astype(BF16)

        @pl.when(i + 1 < n_rows)
        def _():
            x_copy(i + 1).start()

        def up_pair(p, _):
            for slot in (0, 1):
                f = 2 * p + slot
                w1_copy(f, slot).wait()

                @pl.when(f + 1 < n_up)
                def _():
                    w1_copy(f + 1, 1 - slot).start()

                a = jnp.maximum(_dot(h_ref[...], w1_buf[slot].astype(BF16)), 0.0)
                a_ref[:, pl.ds(pl.multiple_of(f * tf, tf), tf)] = (a * a).astype(BF16)
            return 0

        lax.fori_loop(0, n_up // 2, up_pair, 0)

    @pl.when(t > 0)
    def _():
        n = t - 1
        for kh in range(n_k):
            w2_copy(n, kh).wait()
            part = _dot(a_ref[:, kh * tk:(kh + 1) * tk], w2_buf[kh % 2].astype(BF16))
            if kh == 0:
                o_ref[...] = xr_ref[...] + part
            else:
                o_ref[...] += part
            if kh + 2 < n_k:
                w2_copy(n, kh + 2).start()
            else:
                @pl.when(n + 1 < n_out)
                def _(kh=kh):
                    w2_copy(n + 1, kh + 2 - n_k).start()

        @pl.when((n + 1 == n_out) & (i + 1 < n_rows))
        def _():
            w1_copy(0, 0).start()


def _mlp(x, gain, w1, w2, layer):
    s, d = x.shape
    dff = w1.shape[2]
    tm, tf, tn, tk = min(ROW_TILE, s), FF_TILE, MLP_OUT_TILE, MLP_K_TILE
    n_rows, n_out = s // tm, d // tn

    def out_tile(t):
        return jnp.maximum(t - 1, 0)

    return pl.pallas_call(
        functools.partial(_mlp_kernel, layer=layer, n_rows=n_rows),
        grid=(n_rows, 1 + n_out),
        in_specs=[
            pl.BlockSpec(memory_space=pl.ANY),
            pl.BlockSpec((1, d), lambda i, t: (0, 0)),
            pl.BlockSpec(memory_space=pl.ANY),
            pl.BlockSpec(memory_space=pl.ANY),
            pl.BlockSpec((tm, tn), lambda i, t: (i, out_tile(t))),
        ],
        out_specs=pl.BlockSpec((tm, tn), lambda i, t: (i, out_tile(t))),
        out_shape=jax.ShapeDtypeStruct((s, d), F32),
        scratch_shapes=[
            pltpu.VMEM((tm, d), F32),
            pltpu.VMEM((tm, d), BF16),
            pltpu.VMEM((tm, dff), BF16),
            pltpu.VMEM((2, d, tf), F32),
            pltpu.VMEM((2, tk, tn), F32),
            pltpu.SemaphoreType.DMA((1,)),
            pltpu.SemaphoreType.DMA((2,)),
            pltpu.SemaphoreType.DMA((2,)),
        ],
        compiler_params=_params("arbitrary", "arbitrary"),
        name="mlp",
    )(x, gain, w1, w2, x)


def _proj_res_kernel(a_ref, w_ref, x_ref, o_ref, wb_ref):
    @pl.when(pl.program_id(1) == 0)
    def _():
        wb_ref[...] = w_ref[...].astype(BF16)

    o_ref[...] = x_ref[...] + _dot(a_ref[...], wb_ref[...])


def _proj_res(a, w, x, layer):
    s, k = a.shape
    n = w.shape[2]
    tm, tn = min(PROJ_ROW_TILE, s), min(PROJ_COL_TILE, n)
    return pl.pallas_call(
        _proj_res_kernel,
        grid=(n // tn, s // tm),
        in_specs=[
            pl.BlockSpec((tm, k), lambda j, i: (i, 0)),
            pl.BlockSpec((None, k, tn), lambda j, i: (layer, 0, j), pipeline_mode=pl.Buffered(1)),
            pl.BlockSpec((tm, tn), lambda j, i: (i, j)),
        ],
        out_specs=pl.BlockSpec((tm, tn), lambda j, i: (i, j)),
        out_shape=jax.ShapeDtypeStruct((s, n), F32),
        scratch_shapes=[pltpu.VMEM((k, tn), BF16)],
        compiler_params=_params("arbitrary", "arbitrary"),
        name="proj_res",
    )(a, w, x)


def _glu_res_kernel(a_ref, wv_ref, wg_ref, x_ref, o_ref, wvb_ref, wgb_ref):
    @pl.when(pl.program_id(1) == 0)
    def _():
        wvb_ref[...] = wv_ref[...].astype(BF16)
        wgb_ref[...] = wg_ref[...].astype(BF16)

    a = a_ref[...]
    val = _dot(a, wvb_ref[...])
    gate = _dot(a, wgb_ref[...])
    o_ref[...] = x_ref[...] + val * jax.nn.sigmoid(gate)


def _glu_res(a, w_glu, x, layer):
    s, k = a.shape
    n = w_glu.shape[2] // 2
    tm, tn = min(PROJ_ROW_TILE, s), min(GLU_COL_TILE, n)
    nj = n // tn
    return pl.pallas_call(
        _glu_res_kernel,
        grid=(nj, s // tm),
        in_specs=[
            pl.BlockSpec((tm, k), lambda j, i: (i, 0)),
            pl.BlockSpec((None, k, tn), lambda j, i: (layer, 0, j), pipeline_mode=pl.Buffered(1)),
            pl.BlockSpec((None, k, tn), lambda j, i: (layer, 0, j + nj), pipeline_mode=pl.Buffered(1)),
            pl.BlockSpec((tm, tn), lambda j, i: (i, j)),
        ],
        out_specs=pl.BlockSpec((tm, tn), lambda j, i: (i, j)),
        out_shape=jax.ShapeDtypeStruct((s, n), F32),
        scratch_shapes=[pltpu.VMEM((k, tn), BF16), pltpu.VMEM((k, tn), BF16)],
        compiler_params=_params("arbitrary", "arbitrary"),
        name="glu_res",
    )(a, w_glu, w_glu, x)


def _conv_front_kernel(x_ref, g_ref, wb_ref, wc_ref, wv_ref, cw_ref, o_ref, h_ref, ext_ref, carry_ref):
    i, j = pl.program_id(0), pl.program_id(1)
    tm = o_ref.shape[0]

    @pl.when(j == 0)
    def _():
        h_ref[...] = _rms_norm(x_ref[...], g_ref[...]).astype(BF16)

    @pl.when(i == 0)
    def _():
        carry_ref[j] = jnp.zeros(carry_ref.shape[1:], F32)

    h = h_ref[...]
    u = _wdot(h, wc_ref) * _wdot(h, wv_ref)
    ext_ref[0:HALO, :] = carry_ref[j]
    ext_ref[HALO:, :] = u
    carry_ref[j] = u[tm - HALO:, :]
    conv = cw_ref[CONV_WIDTH - 1:CONV_WIDTH, :] * u
    for lag in range(1, CONV_WIDTH):
        tap = CONV_WIDTH - 1 - lag
        conv += cw_ref[tap:tap + 1, :] * ext_ref[pl.ds(HALO - lag, tm), :]
    o_ref[...] = (_wdot(h, wb_ref) * conv).astype(BF16)


def _conv_front(x, gain, w_in, conv_w, layer):
    s, d = x.shape
    tm, tn = min(ROW_TILE, s), COL_TILE
    nj = d // tn
    return pl.pallas_call(
        _conv_front_kernel,
        grid=(s // tm, nj),
        in_specs=[
            pl.BlockSpec((tm, d), lambda i, j: (i, 0)),
            pl.BlockSpec((1, d), lambda i, j: (0, 0)),
            pl.BlockSpec((None, d, tn), lambda i, j: (layer, 0, j)),
            pl.BlockSpec((None, d, tn), lambda i, j: (layer, 0, j + nj)),
            pl.BlockSpec((None, d, tn), lambda i, j: (layer, 0, j + 2 * nj)),
            pl.BlockSpec((CONV_WIDTH, tn), lambda i, j: (0, j)),
        ],
        out_specs=pl.BlockSpec((tm, tn), lambda i, j: (i, j)),
        out_shape=jax.ShapeDtypeStruct((s, d), BF16),
        scratch_shapes=[
            pltpu.VMEM((tm, d), BF16),
            pltpu.VMEM((tm + HALO, tn), F32),
            pltpu.VMEM((nj, HALO, tn), F32),
        ],
        compiler_params=_params("arbitrary", "arbitrary"),
        name="conv_front",
    )(x, gain, w_in, w_in, w_in, conv_w)


def _pool_kernel(x_ref, g_ref, win_ref, wg_ref, sc_ref, xr_ref, o_ref, h_ref, ext_ref, la_ref, lb_ref,
                 carry_ref, winb_ref, wgb_ref):
    i, grp = pl.program_id(0), pl.program_id(1)
    tm = o_ref.shape[0]
    halo = POOL_HALO
    n_ext = tm + halo

    @pl.when(grp == 0)
    def _():
        h_ref[...] = _rms_norm(x_ref[...], g_ref[...]).astype(BF16)

    @pl.when(i == 0)
    def _():
        carry_ref[grp] = jnp.zeros(carry_ref.shape[1:], F32)
        winb_ref[grp] = win_ref[...].astype(BF16)
        wgb_ref[grp] = wg_ref[0].astype(BF16)

    u = _dot(h_ref[...], winb_ref[grp])
    ext_ref[0:halo, :] = carry_ref[grp]
    ext_ref[halo:, :] = u
    carry_ref[grp] = u[tm - halo:, :]
    pos = (i * tm + 1 + lax.broadcasted_iota(jnp.int32, (tm, 1), 0)).astype(F32)

    for gi, w in enumerate(POOL_WINDOWS):
        @pl.when(grp == gi)
        def _(w=w):
            levels = int(math.log2(w))
            assert 2 ** levels == w and SUBLANES * levels <= halo
            src = ext_ref
            for lv in range(1, levels):
                dst = la_ref if lv % 2 else lb_ref
                lo, shift = SUBLANES * lv, 2 ** (lv - 1)
                dst[lo:, :] = src[lo:, :] + src[pl.ds(lo - shift, n_ext - lo), :]
                src = dst
            acc = src[halo:, :] + src[pl.ds(halo - w // 2, tm), :]
            inv_count = 1.0 / jnp.minimum(pos, float(w))
            pooled = (acc * inv_count - u).astype(BF16)
            o_ref[...] = xr_ref[...] + _dot(pooled, wgb_ref[gi]) * sc_ref[...]


def _pool_mixer(x, gain, w_in, w_group, scale, layer):
    s, d = x.shape
    ng, pg = w_group.shape[1], w_group.shape[2]
    tm = min(ROW_TILE, s)

    def first_tile_only(i, g):
        return jnp.where(i == 0, g, ng - 1)

    return pl.pallas_call(
        _pool_kernel,
        grid=(s // tm, ng),
        in_specs=[
            pl.BlockSpec((tm, d), lambda i, g: (i, 0)),
            pl.BlockSpec((1, d), lambda i, g: (0, 0)),
            pl.BlockSpec((None, d, pg), lambda i, g: (layer, 0, first_tile_only(i, g)),
                         pipeline_mode=pl.Buffered(1)),
            pl.BlockSpec((None, 1, pg, pg), lambda i, g: (layer, first_tile_only(i, g), 0, 0)),
            pl.BlockSpec((1, pg), lambda i, g: (0, g)),
            pl.BlockSpec((tm, pg), lambda i, g: (i, g)),
        ],
        out_specs=pl.BlockSpec((tm, pg), lambda i, g: (i, g)),
        out_shape=jax.ShapeDtypeStruct((s, d), F32),
        scratch_shapes=[
            pltpu.VMEM((tm, d), BF16),
            pltpu.VMEM((tm + POOL_HALO, pg), F32),
            pltpu.VMEM((tm + POOL_HALO, pg), F32),
            pltpu.VMEM((tm + POOL_HALO, pg), F32),
            pltpu.VMEM((ng, POOL_HALO, pg), F32),
            pltpu.VMEM((ng, d, pg), BF16),
            pltpu.VMEM((ng, pg, pg), BF16),
        ],
        compiler_params=_params("arbitrary", "arbitrary"),
        name="pool_mixer",
    )(x, gain, w_in, w_group, scale, x)


def _qkv_kernel(x_ref, g_ref, w_ref, hg_ref, o_ref, h_ref, *, n_norm_tiles):
    j = pl.program_id(1)

    @pl.when(j == 0)
    def _():
        h_ref[...] = _rms_norm(x_ref[...], g_ref[...]).astype(BF16)

    y = _wdot(h_ref[...], w_ref)

    @pl.when(j < n_norm_tiles)
    def _():
        for hd in range(y.shape[1] // ATT_HEAD_DIM):
            sl = slice(hd * ATT_HEAD_DIM, (hd + 1) * ATT_HEAD_DIM)
            o_ref[:, sl] = _rms_norm(y[:, sl], hg_ref[:, sl]).astype(BF16)

    @pl.when(j >= n_norm_tiles)
    def _():
        o_ref[...] = y.astype(BF16)


def _qkv_proj(x, gain, w_qkv, head_gains, layer):
    s, d = x.shape
    n = w_qkv.shape[2]
    tm, tn = min(ROW_TILE, s), QKV_COL_TILE
    n_norm_tiles = head_gains.shape[1] // tn
    return pl.pallas_call(
        functools.partial(_qkv_kernel, n_norm_tiles=n_norm_tiles),
        grid=(s // tm, n // tn),
        in_specs=[
            pl.BlockSpec((tm, d), lambda i, j: (i, 0)),
            pl.BlockSpec((1, d), lambda i, j: (0, 0)),
            pl.BlockSpec((None, d, tn), lambda i, j: (layer, 0, j)),
            pl.BlockSpec((1, tn), lambda i, j: (0, jnp.minimum(j, n_norm_tiles - 1))),
        ],
        out_specs=pl.BlockSpec((tm, tn), lambda i, j: (i, j)),
        out_shape=jax.ShapeDtypeStruct((s, n), BF16),
        scratch_shapes=[pltpu.VMEM((tm, d), BF16)],
        compiler_params=_params("parallel", "arbitrary"),
        name="qkv_proj",
    )(x, gain, w_qkv, head_gains)


def _attn_kernel(q_ref, k0_ref, k1_ref, k2_ref, v0_ref, v1_ref, v2_ref, prof_ref, o_ref, bias_ref):
    b = pl.program_id(1)
    qb = q_ref.shape[0]
    n_local = q_ref.shape[1] // ATT_HEAD_DIM
    log2e = math.log2(math.e)
    scale = ATT_HEAD_DIM ** -0.5 * log2e

    @pl.when(b == 0)
    def _():
        q_idx = lax.broadcasted_iota(jnp.int32, (qb, 3 * qb), 0) + 2 * qb
        k_idx = lax.broadcasted_iota(jnp.int32, (qb, 3 * qb), 1)
        chunk_start = (q_idx // CHUNK) * CHUNK
        in_band = (k_idx >= chunk_start - ATT_LEFT_CHUNKS * CHUNK) & (k_idx < chunk_start + CHUNK)
        for hh in range(n_local):
            profile = jnp.broadcast_to(prof_ref[hh], (qb, prof_ref.shape[2]))
            table = pltpu.roll(profile, 0, 1, stride=1, stride_axis=0)[:, :3 * qb]
            bias_ref[hh] = jnp.where(in_band, table * log2e, MASK_VALUE)

    def lane_slabs(a):
        return [a[:, k * LANES:(k + 1) * LANES] for k in range(a.shape[1] // LANES)]

    def heads(first_blocks):
        for hh in range(n_local):
            cols = slice(hh * ATT_HEAD_DIM, (hh + 1) * ATT_HEAD_DIM)
            q = q_ref[:, cols]
            scores = []
            for c, k_ref in enumerate((k0_ref, k1_ref, k2_ref)):
                s_c = lax.dot_general(q, k_ref[:, cols], (((1,), (1,)), ((), ())),
                                      preferred_element_type=F32)
                s_c = s_c * scale + bias_ref[hh, :, c * qb:(c + 1) * qb]
                if first_blocks and c < 2:
                    s_c = jnp.where(b - 2 + c >= 0, s_c, MASK_VALUE)
                scores.append(s_c)
            m = jnp.max(functools.reduce(jnp.maximum,
                                         [sl for s_c in scores for sl in lane_slabs(s_c)]),
                        axis=-1, keepdims=True)
            acc = jnp.zeros((qb, 2 * ATT_HEAD_DIM), F32)
            ones = jnp.ones((qb, ATT_HEAD_DIM), BF16)
            for s_c, v_ref in zip(scores, (v0_ref, v1_ref, v2_ref)):
                p = jnp.exp2((s_c - m).astype(BF16))
                acc += _dot(p, jnp.concatenate([v_ref[:, cols], ones], axis=1))
            o_ref[:, cols] = (acc[:, :ATT_HEAD_DIM] / acc[:, ATT_HEAD_DIM:]).astype(BF16)

    @pl.when(b < 2)
    def _():
        heads(True)

    @pl.when(b >= 2)
    def _():
        heads(False)


def _attention(qkv, profile, n_heads):
    s = qkv.shape[0]
    qb = ATT_QBLOCK
    nb = s // qb
    hps = ATT_HEADS_PER_STEP
    width = hps * ATT_HEAD_DIM
    ngroups = n_heads // hps

    def kv_spec(offset, back):
        return pl.BlockSpec((qb, width), lambda h, b: (jnp.maximum(b - back, 0), offset + h))

    return pl.pallas_call(
        _attn_kernel,
        grid=(ngroups, nb),
        in_specs=[
            pl.BlockSpec((qb, width), lambda h, b: (b, h)),
            kv_spec(ngroups, 2), kv_spec(ngroups, 1), kv_spec(ngroups, 0),
            kv_spec(2 * ngroups, 2), kv_spec(2 * ngroups, 1), kv_spec(2 * ngroups, 0),
            pl.BlockSpec((hps, 1, 4 * qb), lambda h, b: (h, 0, 0)),
        ],
        out_specs=pl.BlockSpec((qb, width), lambda h, b: (b, h)),
        out_shape=jax.ShapeDtypeStruct((s, n_heads * ATT_HEAD_DIM), BF16),
        scratch_shapes=[pltpu.VMEM((hps, qb, 3 * qb), F32)],
        compiler_params=_params("arbitrary", "arbitrary"),
        name="chunk_attention",
    )(qkv, qkv, qkv, qkv, qkv, qkv, qkv, profile)


def _attention_profile(rel_bias):
    qb = ATT_QBLOCK
    n_heads = rel_bias.shape[0]
    assert REL_CLIP <= 2 * qb and qb <= REL_CLIP + 1 and 2 * REL_CLIP + 1 == rel_bias.shape[1]
    n_flat = 2 * qb - REL_CLIP + 1
    top = rel_bias[:, 2 * REL_CLIP:]
    profile = jnp.concatenate([
        jnp.broadcast_to(top, (n_heads, n_flat)),
        jnp.flip(rel_bias[:, REL_CLIP - qb + 1:2 * REL_CLIP], axis=1),
        jnp.broadcast_to(top, (n_heads, qb)),
    ], axis=1)
    return profile.reshape(n_heads, 1, 4 * qb).astype(F32)


def _ssm_prep_kernel(are_ref, aim_ref, ldt_ref, btr_ref, bti_ref, abr_ref, abi_ref, bbr_ref, bbi_ref):
    lam_r, lam_i = are_ref[...], aim_ref[...]
    dt = jnp.exp(ldt_ref[...])
    mag = jnp.exp(lam_r * dt)
    ab_r, ab_i = mag * jnp.cos(lam_i * dt), mag * jnp.sin(lam_i * dt)
    abr_ref[...] = ab_r
    abi_ref[...] = ab_i
    num_r, num_i = ab_r - 1.0, ab_i
    den = lam_r * lam_r + lam_i * lam_i
    co_r = (num_r * lam_r + num_i * lam_i) / den
    co_i = (num_i * lam_r - num_r * lam_i) / den
    for c in range(btr_ref.shape[0]):
        b_r, b_i = btr_ref[c], bti_ref[c]
        bbr_ref[c] = co_r * b_r - co_i * b_i
        bbi_ref[c] = co_r * b_i + co_i * b_r


def _ssm_prep(a_re, a_im, log_dt, bt_re, bt_im):
    g, n = a_re.shape
    gn = jax.ShapeDtypeStruct((g, n), F32)
    cgn = jax.ShapeDtypeStruct(bt_re.shape, F32)
    return pl.pallas_call(
        _ssm_prep_kernel,
        out_shape=(gn, gn, cgn, cgn),
        name="ssm_prep",
    )(a_re, a_im, log_dt, bt_re, bt_im)


def _ssm_kernel(x_ref, g_ref, d_ref, wbr_ref, wbi_ref, wcr_ref, wci_ref, ar_ref, ai_ref, o_ref,
                hn_ref, y_ref, xr_ref, xi_ref, cr_ref, ci_ref):
    i = pl.program_id(0)
    t = x_ref.shape[0]
    nblk, cb, ns = wbr_ref.shape
    n_batches, ppb = xr_ref.shape[0], xr_ref.shape[1]
    nsub = ns // LANES
    assert cb == LANES and 2 * nsub == SUBLANES and n_batches * ppb * 2 == nblk

    h = _rms_norm(x_ref[...], g_ref[...])
    for c in range(nblk):
        hn_ref[c] = h[:, c * LANES:(c + 1) * LANES]

    @pl.when(i == 0)
    def _():
        cr_ref[...] = jnp.zeros(cr_ref.shape, F32)
        ci_ref[...] = jnp.zeros(ci_ref.shape, F32)

    def slab_rows(half, k):
        return pl.ds(half * nsub + k, t, stride=SUBLANES)

    def blocks_of(batch):
        for g8 in range(2 * ppb):
            yield batch * 2 * ppb + g8, g8 // 2, g8 % 2

    def project_in(batch):
        for gb, pair, half in blocks_of(batch):
            hb = hn_ref[gb].astype(BF16)
            for w_ref, dst_ref in ((wbr_ref, xr_ref), (wbi_ref, xi_ref)):
                bu = _dot(hb, w_ref[gb])
                for k in range(nsub):
                    dst_ref[batch, pair, slab_rows(half, k), :] = bu[:, k * LANES:(k + 1) * LANES]

    def scan(batch):
        coef = [(ar_ref[batch * ppb + p], ai_ref[batch * ppb + p]) for p in range(ppb)]

        def step(tt, carry):
            rows = pl.ds(pl.multiple_of(tt * SUBLANES, SUBLANES), SUBLANES)
            out = []
            for p in range(ppb):
                (a_r, a_i), x_r, x_i = coef[p], carry[2 * p], carry[2 * p + 1]
                n_r = a_r * x_r - a_i * x_i + xr_ref[batch, p, rows, :]
                n_i = a_r * x_i + a_i * x_r + xi_ref[batch, p, rows, :]
                xr_ref[batch, p, rows, :] = n_r
                xi_ref[batch, p, rows, :] = n_i
                out += [n_r, n_i]
            return tuple(out)

        init = tuple(ref[batch * ppb + p] for p in range(ppb) for ref in (cr_ref, ci_ref))
        last = lax.fori_loop(0, t, step, init, unroll=8)
        for p in range(ppb):
            cr_ref[batch * ppb + p] = last[2 * p]
            ci_ref[batch * ppb + p] = last[2 * p + 1]

    def project_out(batch):
        for gb, pair, half in blocks_of(batch):
            s_r, s_i = (jnp.concatenate([ref[batch, pair, slab_rows(half, k), :] for k in range(nsub)],
                                        axis=1).astype(BF16) for ref in (xr_ref, xi_ref))
            y_ref[gb] = _dot(s_r, wcr_ref[gb]) + _dot(s_i, wci_ref[gb])

    project_in(0)
    for batch in range(n_batches):
        scan(batch)
        project_out(batch)
        if batch + 1 < n_batches:
            project_in(batch + 1)

    for c in range(nblk):
        lanes = slice(c * LANES, (c + 1) * LANES)
        o_ref[:, lanes] = jax.nn.gelu(y_ref[c] + d_ref[:, lanes] * hn_ref[c]).astype(BF16)


def _ssm_core(x, gain, d_skip, wb_r, wb_i, wc_r, wc_i, ab_r, ab_i):
    s, d = x.shape
    t = min(SSM_ROW_TILE, s)
    nblk, cb, ns = wb_r.shape
    ppb = SSM_PAIRS_PER_BATCH
    n_batches = nblk // (2 * ppb)

    def whole(a):
        return pl.BlockSpec(a.shape, lambda i: (0,) * a.ndim)

    return pl.pallas_call(
        _ssm_kernel,
        grid=(s // t,),
        in_specs=[
            pl.BlockSpec((t, d), lambda i: (i, 0)),
            whole(gain), whole(d_skip),
            whole(wb_r), whole(wb_i), whole(wc_r), whole(wc_i),
            whole(ab_r), whole(ab_i),
        ],
        out_specs=pl.BlockSpec((t, d), lambda i: (i, 0)),
        out_shape=jax.ShapeDtypeStruct((s, d), BF16),
        scratch_shapes=[
            pltpu.VMEM((nblk, t, LANES), F32),
            pltpu.VMEM((nblk, t, LANES), F32),
            pltpu.VMEM((n_batches, ppb, t * SUBLANES, LANES), F32),
            pltpu.VMEM((n_batches, ppb, t * SUBLANES, LANES), F32),
            pltpu.VMEM((nblk // 2, SUBLANES, LANES), F32),
            pltpu.VMEM((nblk // 2, SUBLANES, LANES), F32),
        ],
        compiler_params=_params("arbitrary"),
        name="ssm_core",
    )(x, gain, d_skip, wb_r, wb_i, wc_r, wc_i, ab_r, ab_i)


def _block_diag(w, groups_per_block):
    g, r, c = w.shape
    nblk = g // groups_per_block
    tiled = jnp.tile(w.reshape(nblk, groups_per_block * r, c), (1, 1, groups_per_block))
    row_group = lax.broadcasted_iota(jnp.int32, tiled.shape, 1) // r
    col_group = lax.broadcasted_iota(jnp.int32, tiled.shape, 2) // c
    return jnp.where(row_group == col_group, tiled, jnp.zeros_like(tiled))


def _s5_mixer(x, gain, a_re, a_im, log_dt, b_re, b_im, c_re, c_im, d_skip, w_glu, layer):
    s, d = x.shape
    g, n = a_re.shape
    gpb = SSM_GROUPS_PER_BLOCK
    bt_re, bt_im = b_re.transpose(2, 0, 1), b_im.transpose(2, 0, 1)
    ab_r, ab_i, bb_r, bb_i = _ssm_prep(a_re, a_im, log_dt.reshape(g, 1), bt_re, bt_im)
    wb_r = _block_diag(bb_r.transpose(1, 0, 2), gpb).astype(BF16)
    wb_i = _block_diag(bb_i.transpose(1, 0, 2), gpb).astype(BF16)
    wc_r = _block_diag(c_re.transpose(0, 2, 1), gpb).astype(BF16)
    wc_i = _block_diag(-c_im.transpose(0, 2, 1), gpb).astype(BF16)
    slabs = lambda a: a.reshape(g // (2 * gpb), 2 * gpb * n // LANES, LANES)
    z = _ssm_core(x, gain, d_skip.reshape(1, d), wb_r, wb_i, wc_r, wc_i, slabs(ab_r), slabs(ab_i))
    return _glu_res(z, w_glu, x, layer)


def kernel(x, norm_mix, norm_mlp, mlp_w1, mlp_w2, conv_w_in, conv_w, conv_w_out, pool_w_in, pool_w_group, pool_scale, att_w_qkv, att_q_norm, att_k_norm, att_rel_bias, att_w_out, ssm_a_re, ssm_a_im, ssm_log_dt, ssm_b_re, ssm_b_im, ssm_c_re, ssm_c_im, ssm_d, ssm_w_glu):
    b, s, d = x.shape
    depth = norm_mix.shape[0]
    n_mixers = 4
    outs = []
    for bi in range(b):
        xs = x[bi]
        for i in range(depth):
            kind, j = i % n_mixers, i // n_mixers
            gain = norm_mix[i].reshape(1, d)
            if kind == 0:
                gated = _conv_front(xs, gain, conv_w_in, conv_w[j], j)
                xs = _proj_res(gated, conv_w_out, xs, j)
            elif kind == 1:
                xs = _pool_mixer(xs, gain, pool_w_in, pool_w_group, pool_scale[j].reshape(1, d), j)
            elif kind == 2:
                n_heads = d // ATT_HEAD_DIM
                head_gains = jnp.concatenate([jnp.tile(att_q_norm[j], n_heads),
                                              jnp.tile(att_k_norm[j], n_heads)]).reshape(1, 2 * d)
                qkv = _qkv_proj(xs, gain, att_w_qkv, head_gains, j)
                att = _attention(qkv, _attention_profile(att_rel_bias[j]), n_heads)
                xs = _proj_res(att, att_w_out, xs, j)
            else:
                xs = _s5_mixer(xs, gain, ssm_a_re[j], ssm_a_im[j], ssm_log_dt[j], ssm_b_re[j],
                               ssm_b_im[j], ssm_c_re[j], ssm_c_im[j], ssm_d[j], ssm_w_glu, j)
            xs = _mlp(xs, norm_mlp[i].reshape(1, d), mlp_w1, mlp_w2, i)
        outs.append(xs)
    return outs[0][None] if b == 1 else jnp.stack(outs)
```

```python
import functools
import math

import jax
import jax.numpy as jnp
from jax import lax
from jax.experimental import pallas as pl
from jax.experimental.pallas import tpu as pltpu

F32 = jnp.float32
BF16 = jnp.bfloat16

RMS_EPS = 1e-6
CHUNK = 64
ATT_HEAD_DIM = 128
ATT_LEFT_CHUNKS = 8
REL_CLIP = 256
MASK_VALUE = -1e30
POOL_WINDOWS = (2, 4, 8, 16)
CONV_WIDTH = 3
SSM_GROUP = 16
SSM_STATE = 64

SUBLANES = 8
LANES = 128
VMEM_LIMIT_BYTES = 56 * 1024 * 1024

ROW_TILE = 1024
COL_TILE = 512
QKV_COL_TILE = 1024
PROJ_ROW_TILE = 512
PROJ_COL_TILE = 2048
GLU_COL_TILE = 1024
FF_TILE = 512
MLP_OUT_TILE = 512
MLP_K_TILE = 2048
ATT_QBLOCK = 256
ATT_HEADS_PER_STEP = 8
SSM_ROW_TILE = 256
SSM_PAIRS_PER_BATCH = 4
SSM_GROUPS_PER_BLOCK = 8
HALO = 16
POOL_HALO = 32


def _params(*semantics):
    return pltpu.CompilerParams(dimension_semantics=semantics, vmem_limit_bytes=VMEM_LIMIT_BYTES)


def _rms_norm(x, gain):
    return x * lax.rsqrt(jnp.mean(x * x, axis=-1, keepdims=True) + RMS_EPS) * gain


def _dot(a, b):
    return jnp.dot(a, b, preferred_element_type=F32)


def _wdot(a, w_ref):
    return _dot(a, w_ref[...].astype(BF16))


def _mlp_kernel(x_hbm, g_ref, w1_hbm, w2_hbm, xr_ref, o_ref, x_buf, h_ref, a_ref, w1_buf, w2_buf,
                semx, sem1, sem2, *, layer, n_rows):
    i, t = pl.program_id(0), pl.program_id(1)
    n_out = pl.num_programs(1) - 1
    tf, tk, tn = w1_buf.shape[2], w2_buf.shape[1], w2_buf.shape[2]
    n_up, n_k = a_ref.shape[1] // tf, a_ref.shape[1] // tk
    assert n_up % 2 == 0 and n_k % 2 == 0

    def w1_copy(f, slot):
        return pltpu.make_async_copy(w1_hbm.at[layer, :, pl.ds(f * tf, tf)], w1_buf.at[slot],
                                     sem1.at[slot])

    def w2_copy(n, kh):
        return pltpu.make_async_copy(w2_hbm.at[layer, pl.ds(kh * tk, tk), pl.ds(n * tn, tn)],
                                     w2_buf.at[kh % 2], sem2.at[kh % 2])

    def x_copy(row):
        tm = x_buf.shape[0]
        return pltpu.make_async_copy(x_hbm.at[pl.ds(row * tm, tm), :], x_buf, semx.at[0])

    @pl.when(t == 0)
    def _():
        @pl.when(i == 0)
        def _():
            x_copy(0).start()
            w1_copy(0, 0).start()

        w2_copy(0, 0).start()
        w2_copy(0, 1).start()
        x_copy(i).wait()
        h_ref[...] = _rms_norm(x_buf[...], g_ref[...]).astype(BF16)

        @pl.when(i + 1 < n_rows)
        def _():
            x_copy(i + 1).start()

        def up_pair(p, _):
            for slot in (0, 1):
                f = 2 * p + slot
                w1_copy(f, slot).wait()

                @pl.when(f + 1 < n_up)
                def _():
                    w1_copy(f + 1, 1 - slot).start()

                a = jnp.maximum(_dot(h_ref[...], w1_buf[slot].astype(BF16)), 0.0)
                a_ref[:, pl.ds(pl.multiple_of(f * tf, tf), tf)] = (a * a).astype(BF16)
            return 0

        lax.fori_loop(0, n_up // 2, up_pair, 0)

    @pl.when(t > 0)
    def _():
        n = t - 1
        for kh in range(n_k):
            w2_copy(n, kh).wait()
            part = _dot(a_ref[:, kh * tk:(kh + 1) * tk], w2_buf[kh % 2].astype(BF16))
            if kh == 0:
                o_ref[...] = xr_ref[...] + part
            else:
                o_ref[...] += part
            if kh + 2 < n_k:
                w2_copy(n, kh + 2).start()
            else:
                @pl.when(n + 1 < n_out)
                def _(kh=kh):
                    w2_copy(n + 1, kh + 2 - n_k).start()

        @pl.when((n + 1 == n_out) & (i + 1 < n_rows))
        def _():
            w1_copy(0, 0).start()


def _mlp(x, gain, w1, w2, layer):
    s, d = x.shape
    dff = w1.shape[2]
    tm, tf, tn, tk = min(ROW_TILE, s), FF_TILE, MLP_OUT_TILE, MLP_K_TILE
    n_rows, n_out = s // tm, d // tn

    def out_tile(t):
        return jnp.maximum(t - 1, 0)

    return pl.pallas_call(
        functools.partial(_mlp_kernel, layer=layer, n_rows=n_rows),
        grid=(n_rows, 1 + n_out),
        in_specs=[
            pl.BlockSpec(memory_space=pl.ANY),
            pl.BlockSpec((1, d), lambda i, t: (0, 0)),
            pl.BlockSpec(memory_space=pl.ANY),
            pl.BlockSpec(memory_space=pl.ANY),
            pl.BlockSpec((tm, tn), lambda i, t: (i, out_tile(t))),
        ],
        out_specs=pl.BlockSpec((tm, tn), lambda i, t: (i, out_tile(t))),
        out_shape=jax.ShapeDtypeStruct((s, d), F32),
        scratch_shapes=[
            pltpu.VMEM((tm, d), F32),
            pltpu.VMEM((tm, d), BF16),
            pltpu.VMEM((tm, dff), BF16),
            pltpu.VMEM((2, d, tf), F32),
            pltpu.VMEM((2, tk, tn), F32),
            pltpu.SemaphoreType.DMA((1,)),
            pltpu.SemaphoreType.DMA((2,)),
            pltpu.SemaphoreType.DMA((2,)),
        ],
        compiler_params=_params("arbitrary", "arbitrary"),
        name="mlp",
    )(x, gain, w1, w2, x)


def _proj_res_kernel(a_ref, w_ref, x_ref, o_ref, wb_ref):
    @pl.when(pl.program_id(1) == 0)
    def _():
        wb_ref[...] = w_ref[...].astype(BF16)

    o_ref[...] = x_ref[...] + _dot(a_ref[...], wb_ref[...])


def _proj_res(a, w, x, layer):
    s, k = a.shape
    n = w.shape[2]
    tm, tn = min(PROJ_ROW_TILE, s), min(PROJ_COL_TILE, n)
    return pl.pallas_call(
        _proj_res_kernel,
        grid=(n // tn, s // tm),
        in_specs=[
            pl.BlockSpec((tm, k), lambda j, i: (i, 0)),
            pl.BlockSpec((None, k, tn), lambda j, i: (layer, 0, j), pipeline_mode=pl.Buffered(1)),
            pl.BlockSpec((tm, tn), lambda j, i: (i, j)),
        ],
        out_specs=pl.BlockSpec((tm, tn), lambda j, i: (i, j)),
        out_shape=jax.ShapeDtypeStruct((s, n), F32),
        scratch_shapes=[pltpu.VMEM((k, tn), BF16)],
        compiler_params=_params("arbitrary", "arbitrary"),
        name="proj_res",
    )(a, w, x)


def _glu_res_kernel(a_ref, wv_ref, wg_ref, x_ref, o_ref, wvb_ref, wgb_ref):
    @pl.when(pl.program_id(1) == 0)
    def _():
        wvb_ref[...] = wv_ref[...].astype(BF16)
        wgb_ref[...] = wg_ref[...].astype(BF16)

    a = a_ref[...]
    val = _dot(a, wvb_ref[...])
    gate = _dot(a, wgb_ref[...])
    o_ref[...] = x_ref[...] + val * jax.nn.sigmoid(gate)


def _glu_res(a, w_glu, x, layer):
    s, k = a.shape
    n = w_glu.shape[2] // 2
    tm, tn = min(PROJ_ROW_TILE, s), min(GLU_COL_TILE, n)
    nj = n // tn
    return pl.pallas_call(
        _glu_res_kernel,
        grid=(nj, s // tm),
        in_specs=[
            pl.BlockSpec((tm, k), lambda j, i: (i, 0)),
            pl.BlockSpec((None, k, tn), lambda j, i: (layer, 0, j), pipeline_mode=pl.Buffered(1)),
            pl.BlockSpec((None, k, tn), lambda j, i: (layer, 0, j + nj), pipeline_mode=pl.Buffered(1)),
            pl.BlockSpec((tm, tn), lambda j, i: (i, j)),
        ],
        out_specs=pl.BlockSpec((tm, tn), lambda j, i: (i, j)),
        out_shape=jax.ShapeDtypeStruct((s, n), F32),
        scratch_shapes=[pltpu.VMEM((k, tn), BF16), pltpu.VMEM((k, tn), BF16)],
        compiler_params=_params("arbitrary", "arbitrary"),
        name="glu_res",
    )(a, w_glu, w_glu, x)


def _conv_front_kernel(x_ref, g_ref, wb_ref, wc_ref, wv_ref, cw_ref, o_ref, h_ref, ext_ref, carry_ref):
    i, j = pl.program_id(0), pl.program_id(1)
    tm = o_ref.shape[0]

    @pl.when(j == 0)
    def _():
        h_ref[...] = _rms_norm(x_ref[...], g_ref[...]).astype(BF16)

    @pl.when(i == 0)
    def _():
        carry_ref[j] = jnp.zeros(carry_ref.shape[1:], F32)

    h = h_ref[...]
    u = _wdot(h, wc_ref) * _wdot(h, wv_ref)
    ext_ref[0:HALO, :] = carry_ref[j]
    ext_ref[HALO:, :] = u
    carry_ref[j] = u[tm - HALO:, :]
    conv = cw_ref[CONV_WIDTH - 1:CONV_WIDTH, :] * u
    for lag in range(1, CONV_WIDTH):
        tap = CONV_WIDTH - 1 - lag
        conv += cw_ref[tap:tap + 1, :] * ext_ref[pl.ds(HALO - lag, tm), :]
    o_ref[...] = (_wdot(h, wb_ref) * conv).astype(BF16)


def _conv_front(x, gain, w_in, conv_w, layer):
    s, d = x.shape
    tm, tn = min(ROW_TILE, s), COL_TILE
    nj = d // tn
    return pl.pallas_call(
        _conv_front_kernel,
        grid=(s // tm, nj),
        in_specs=[
            pl.BlockSpec((tm, d), lambda i, j: (i, 0)),
            pl.BlockSpec((1, d), lambda i, j: (0, 0)),
            pl.BlockSpec((None, d, tn), lambda i, j: (layer, 0, j)),
            pl.BlockSpec((None, d, tn), lambda i, j: (layer, 0, j + nj)),
            pl.BlockSpec((None, d, tn), lambda i, j: (layer, 0, j + 2 * nj)),
            pl.BlockSpec((CONV_WIDTH, tn), lambda i, j: (0, j)),
        ],
        out_specs=pl.BlockSpec((tm, tn), lambda i, j: (i, j)),
        out_shape=jax.ShapeDtypeStruct((s, d), BF16),
        scratch_shapes=[
            pltpu.VMEM((tm, d), BF16),
            pltpu.VMEM((tm + HALO, tn), F32),
            pltpu.VMEM((nj, HALO, tn), F32),
        ],
        compiler_params=_params("arbitrary", "arbitrary"),
        name="conv_front",
    )(x, gain, w_in, w_in, w_in, conv_w)


def _pool_kernel(x_ref, g_ref, win_ref, wg_ref, sc_ref, xr_ref, o_ref, h_ref, ext_ref, la_ref, lb_ref,
                 carry_ref, winb_ref, wgb_ref):
    i, grp = pl.program_id(0), pl.program_id(1)
    tm = o_ref.shape[0]
    halo = POOL_HALO
    n_ext = tm + halo

    @pl.when(grp == 0)
    def _():
        h_ref[...] = _rms_norm(x_ref[...], g_ref[...]).astype(BF16)

    @pl.when(i == 0)
    def _():
        carry_ref[grp] = jnp.zeros(carry_ref.shape[1:], F32)
        winb_ref[grp] = win_ref[...].astype(BF16)
        wgb_ref[grp] = wg_ref[0].astype(BF16)

    u = _dot(h_ref[...], winb_ref[grp])
    ext_ref[0:halo, :] = carry_ref[grp]
    ext_ref[halo:, :] = u
    carry_ref[grp] = u[tm - halo:, :]
    pos = (i * tm + 1 + lax.broadcasted_iota(jnp.int32, (tm, 1), 0)).astype(F32)

    for gi, w in enumerate(POOL_WINDOWS):
        @pl.when(grp == gi)
        def _(w=w):
            levels = int(math.log2(w))
            assert 2 ** levels == w and SUBLANES * levels <= halo
            src = ext_ref
            for lv in range(1, levels):
                dst = la_ref if lv % 2 else lb_ref
                lo, shift = SUBLANES * lv, 2 ** (lv - 1)
                dst[lo:, :] = src[lo:, :] + src[pl.ds(lo - shift, n_ext - lo), :]
                src = dst
            acc = src[halo:, :] + src[pl.ds(halo - w // 2, tm), :]
            inv_count = 1.0 / jnp.minimum(pos, float(w))
            pooled = (acc * inv_count - u).astype(BF16)
            o_ref[...] = xr_ref[...] + _dot(pooled, wgb_ref[gi]) * sc_ref[...]


def _pool_mixer(x, gain, w_in, w_group, scale, layer):
    s, d = x.shape
    ng, pg = w_group.shape[1], w_group.shape[2]
    tm = min(ROW_TILE, s)

    def first_tile_only(i, g):
        return jnp.where(i == 0, g, ng - 1)

    return pl.pallas_call(
        _pool_kernel,
        grid=(s // tm, ng),
        in_specs=[
            pl.BlockSpec((tm, d), lambda i, g: (i, 0)),
            pl.BlockSpec((1, d), lambda i, g: (0, 0)),
            pl.BlockSpec((None, d, pg), lambda i, g: (layer, 0, first_tile_only(i, g)),
                         pipeline_mode=pl.Buffered(1)),
            pl.BlockSpec((None, 1, pg, pg), lambda i, g: (layer, first_tile_only(i, g), 0, 0)),
            pl.BlockSpec((1, pg), lambda i, g: (0, g)),
            pl.BlockSpec((tm, pg), lambda i, g: (i, g)),
        ],
        out_specs=pl.BlockSpec((tm, pg), lambda i, g: (i, g)),
        out_shape=jax.ShapeDtypeStruct((s, d), F32),
        scratch_shapes=[
            pltpu.VMEM((tm, d), BF16),
            pltpu.VMEM((tm + POOL_HALO, pg), F32),
            pltpu.VMEM((tm + POOL_HALO, pg), F32),
            pltpu.VMEM((tm + POOL_HALO, pg), F32),
            pltpu.VMEM((ng, POOL_HALO, pg), F32),
            pltpu.VMEM((ng, d, pg), BF16),
            pltpu.VMEM((ng, pg, pg), BF16),
        ],
        compiler_params=_params("arbitrary", "arbitrary"),
        name="pool_mixer",
    )(x, gain, w_in, w_group, scale, x)


def _qkv_kernel(x_hbm, g_ref, w_ref, hg_ref, o_ref, x_buf, h_ref, wb_ref, semx, *, n_norm_tiles, n_rows):
    i, j = pl.program_id(0), pl.program_id(1)

    def x_copy(row):
        tm = x_buf.shape[0]
        return pltpu.make_async_copy(x_hbm.at[pl.ds(row * tm, tm), :], x_buf, semx.at[0])

    @pl.when(j == 0)
    def _():
        @pl.when(i == 0)
        def _():
            x_copy(0).start()

        x_copy(i).wait()
        h_ref[...] = _rms_norm(x_buf[...], g_ref[...]).astype(BF16)

        @pl.when(i + 1 < n_rows)
        def _():
            x_copy(i + 1).start()

    @pl.when(i == 0)
    def _():
        wb_ref[j] = w_ref[...].astype(BF16)

    y = _dot(h_ref[...], wb_ref[j])

    @pl.when(j < n_norm_tiles)
    def _():
        for hd in range(y.shape[1] // ATT_HEAD_DIM):
            sl = slice(hd * ATT_HEAD_DIM, (hd + 1) * ATT_HEAD_DIM)
            o_ref[:, sl] = _rms_norm(y[:, sl], hg_ref[:, sl]).astype(BF16)

    @pl.when(j >= n_norm_tiles)
    def _():
        o_ref[...] = y.astype(BF16)


def _qkv_proj(x, gain, w_qkv, head_gains, layer):
    s, d = x.shape
    n = w_qkv.shape[2]
    tm, tn = min(ROW_TILE, s), QKV_COL_TILE
    n_norm_tiles = head_gains.shape[1] // tn
    n_rows, n_cols = s // tm, n // tn
    return pl.pallas_call(
        functools.partial(_qkv_kernel, n_norm_tiles=n_norm_tiles, n_rows=n_rows),
        grid=(n_rows, n_cols),
        in_specs=[
            pl.BlockSpec(memory_space=pl.ANY),
            pl.BlockSpec((1, d), lambda i, j: (0, 0)),
            pl.BlockSpec((None, d, tn), lambda i, j: (layer, 0, jnp.where(i == 0, j, n_cols - 1)),
                         pipeline_mode=pl.Buffered(1)),
            pl.BlockSpec((1, tn), lambda i, j: (0, jnp.minimum(j, n_norm_tiles - 1))),
        ],
        out_specs=pl.BlockSpec((tm, tn), lambda i, j: (i, j)),
        out_shape=jax.ShapeDtypeStruct((s, n), BF16),
        scratch_shapes=[
            pltpu.VMEM((tm, d), F32),
            pltpu.VMEM((tm, d), BF16),
            pltpu.VMEM((n_cols, d, tn), BF16),
            pltpu.SemaphoreType.DMA((1,)),
        ],
        compiler_params=_params("arbitrary", "arbitrary"),
        name="qkv_proj",
    )(x, gain, w_qkv, head_gains)


def _attn_kernel(q_ref, k0_ref, k1_ref, k2_ref, v0_ref, v1_ref, v2_ref, prof_ref, o_ref, bias_ref):
    b = pl.program_id(1)
    qb = q_ref.shape[0]
    n_local = q_ref.shape[1] // ATT_HEAD_DIM
    log2e = math.log2(math.e)
    scale = ATT_HEAD_DIM ** -0.5 * log2e

    @pl.when(b == 0)
    def _():
        q_idx = lax.broadcasted_iota(jnp.int32, (qb, 3 * qb), 0) + 2 * qb
        k_idx = lax.broadcasted_iota(jnp.int32, (qb, 3 * qb), 1)
        chunk_start = (q_idx // CHUNK) * CHUNK
        in_band = (k_idx >= chunk_start - ATT_LEFT_CHUNKS * CHUNK) & (k_idx < chunk_start + CHUNK)
        for hh in range(n_local):
            profile = jnp.broadcast_to(prof_ref[hh], (qb, prof_ref.shape[2]))
            table = pltpu.roll(profile, 0, 1, stride=1, stride_axis=0)[:, :3 * qb]
            bias_ref[hh] = jnp.where(in_band, table * log2e, MASK_VALUE)

    def lane_slabs(a):
        return [a[:, k * LANES:(k + 1) * LANES] for k in range(a.shape[1] // LANES)]

    def heads(first_blocks):
        for hh in range(n_local):
            cols = slice(hh * ATT_HEAD_DIM, (hh + 1) * ATT_HEAD_DIM)
            q = q_ref[:, cols]
            scores = []
            for c, k_ref in enumerate((k0_ref, k1_ref, k2_ref)):
                s_c = lax.dot_general(q, k_ref[:, cols], (((1,), (1,)), ((), ())),
                                      preferred_element_type=F32)
                s_c = s_c * scale + bias_ref[hh, :, c * qb:(c + 1) * qb]
                if first_blocks and c < 2:
                    s_c = jnp.where(b - 2 + c >= 0, s_c, MASK_VALUE)
                scores.append(s_c)
            m = jnp.max(functools.reduce(jnp.maximum,
                                         [sl for s_c in scores for sl in lane_slabs(s_c)]),
                        axis=-1, keepdims=True)
            acc = jnp.zeros((qb, 2 * ATT_HEAD_DIM), F32)
            ones = jnp.ones((qb, ATT_HEAD_DIM), BF16)
            for s_c, v_ref in zip(scores, (v0_ref, v1_ref, v2_ref)):
                p = jnp.exp2((s_c - m).astype(BF16))
                acc += _dot(p, jnp.concatenate([v_ref[:, cols], ones], axis=1))
            o_ref[:, cols] = (acc[:, :ATT_HEAD_DIM] / acc[:, ATT_HEAD_DIM:]).astype(BF16)

    @pl.when(b < 2)
    def _():
        heads(True)

    @pl.when(b >= 2)
    def _():
        heads(False)


def _attention(qkv, profile, n_heads):
    s = qkv.shape[0]
    qb = ATT_QBLOCK
    nb = s // qb
    hps = ATT_HEADS_PER_STEP
    width = hps * ATT_HEAD_DIM
    ngroups = n_heads // hps

    def kv_spec(offset, back):
        return pl.BlockSpec((qb, width), lambda h, b: (jnp.maximum(b - back, 0), offset + h))

    return pl.pallas_call(
        _attn_kernel,
        grid=(ngroups, nb),
        in_specs=[
            pl.BlockSpec((qb, width), lambda h, b: (b, h)),
            kv_spec(ngroups, 2), kv_spec(ngroups, 1), kv_spec(ngroups, 0),
            kv_spec(2 * ngroups, 2), kv_spec(2 * ngroups, 1), kv_spec(2 * ngroups, 0),
            pl.BlockSpec((hps, 1, 4 * qb), lambda h, b: (h, 0, 0)),
        ],
        out_specs=pl.BlockSpec((qb, width), lambda h, b: (b, h)),
        out_shape=jax.ShapeDtypeStruct((s, n_heads * ATT_HEAD_DIM), BF16),
        scratch_shapes=[pltpu.VMEM((hps, qb, 3 * qb), F32)],
        compiler_params=_params("arbitrary", "arbitrary"),
        name="chunk_attention",
    )(qkv, qkv, qkv, qkv, qkv, qkv, qkv, profile)


def _attention_profile(rel_bias):
    qb = ATT_QBLOCK
    n_heads = rel_bias.shape[0]
    assert REL_CLIP <= 2 * qb and qb <= REL_CLIP + 1 and 2 * REL_CLIP + 1 == rel_bias.shape[1]
    n_flat = 2 * qb - REL_CLIP + 1
    top = rel_bias[:, 2 * REL_CLIP:]
    profile = jnp.concatenate([
        jnp.broadcast_to(top, (n_heads, n_flat)),
        jnp.flip(rel_bias[:, REL_CLIP - qb + 1:2 * REL_CLIP], axis=1),
        jnp.broadcast_to(top, (n_heads, qb)),
    ], axis=1)
    return profile.reshape(n_heads, 1, 4 * qb).astype(F32)


def _ssm_prep_kernel(are_ref, aim_ref, ldt_ref, btr_ref, bti_ref, abr_ref, abi_ref, bbr_ref, bbi_ref):
    lam_r, lam_i = are_ref[...], aim_ref[...]
    dt = jnp.exp(ldt_ref[...])
    mag = jnp.exp(lam_r * dt)
    ab_r, ab_i = mag * jnp.cos(lam_i * dt), mag * jnp.sin(lam_i * dt)
    abr_ref[...] = ab_r
    abi_ref[...] = ab_i
    num_r, num_i = ab_r - 1.0, ab_i
    den = lam_r * lam_r + lam_i * lam_i
    co_r = (num_r * lam_r + num_i * lam_i) / den
    co_i = (num_i * lam_r - num_r * lam_i) / den
    for c in range(btr_ref.shape[0]):
        b_r, b_i = btr_ref[c], bti_ref[c]
        bbr_ref[c] = co_r * b_r - co_i * b_i
        bbi_ref[c] = co_r * b_i + co_i * b_r


def _ssm_prep(a_re, a_im, log_dt, bt_re, bt_im):
    g, n = a_re.shape
    gn = jax.ShapeDtypeStruct((g, n), F32)
    cgn = jax.ShapeDtypeStruct(bt_re.shape, F32)
    return pl.pallas_call(
        _ssm_prep_kernel,
        out_shape=(gn, gn, cgn, cgn),
        name="ssm_prep",
    )(a_re, a_im, log_dt, bt_re, bt_im)


def _ssm_kernel(x_ref, g_ref, d_ref, wbr_ref, wbi_ref, wcr_ref, wci_ref, ar_ref, ai_ref, o_ref,
                hn_ref, y_ref, xr_ref, xi_ref, cr_ref, ci_ref):
    i = pl.program_id(0)
    t = x_ref.shape[0]
    nblk, cb, ns = wbr_ref.shape
    n_batches, ppb = xr_ref.shape[0], xr_ref.shape[1]
    nsub = ns // LANES
    assert cb == LANES and 2 * nsub == SUBLANES and n_batches * ppb * 2 == nblk

    h = _rms_norm(x_ref[...], g_ref[...])
    for c in range(nblk):
        hn_ref[c] = h[:, c * LANES:(c + 1) * LANES]

    @pl.when(i == 0)
    def _():
        cr_ref[...] = jnp.zeros(cr_ref.shape, F32)
        ci_ref[...] = jnp.zeros(ci_ref.shape, F32)

    def slab_rows(half, k):
        return pl.ds(half * nsub + k, t, stride=SUBLANES)

    def blocks_of(batch):
        for g8 in range(2 * ppb):
            yield batch * 2 * ppb + g8, g8 // 2, g8 % 2

    def project_in(batch):
        for gb, pair, half in blocks_of(batch):
            hb = hn_ref[gb].astype(BF16)
            for w_ref, dst_ref in ((wbr_ref, xr_ref), (wbi_ref, xi_ref)):
                bu = _dot(hb, w_ref[gb])
                for k in range(nsub):
                    dst_ref[batch, pair, slab_rows(half, k), :] = bu[:, k * LANES:(k + 1) * LANES]

    def scan(batch):
        coef = [(ar_ref[batch * ppb + p], ai_ref[batch * ppb + p]) for p in range(ppb)]

        def step(tt, carry):
            rows = pl.ds(pl.multiple_of(tt * SUBLANES, SUBLANES), SUBLANES)
            out = []
            for p in range(ppb):
                (a_r, a_i), x_r, x_i = coef[p], carry[2 * p], carry[2 * p + 1]
                n_r = a_r * x_r - a_i * x_i + xr_ref[batch, p, rows, :]
                n_i = a_r * x_i + a_i * x_r + xi_ref[batch, p, rows, :]
                xr_ref[batch, p, rows, :] = n_r
                xi_ref[batch, p, rows, :] = n_i
                out += [n_r, n_i]
            return tuple(out)

        init = tuple(ref[batch * ppb + p] for p in range(ppb) for ref in (cr_ref, ci_ref))
        last = lax.fori_loop(0, t, step, init, unroll=8)
        for p in range(ppb):
            cr_ref[batch * ppb + p] = last[2 * p]
            ci_ref[batch * ppb + p] = last[2 * p + 1]

    def project_out(batch):
        for gb, pair, half in blocks_of(batch):
            s_r, s_i = (jnp.concatenate([ref[batch, pair, slab_rows(half, k), :] for k in range(nsub)],
                                        axis=1).astype(BF16) for ref in (xr_ref, xi_ref))
            y_ref[gb] = _dot(s_r, wcr_ref[gb]) + _dot(s_i, wci_ref[gb])

    project_in(0)
    for batch in range(n_batches):
        scan(batch)
        project_out(batch)
        if batch + 1 < n_batches:
            project_in(batch + 1)

    for c in range(nblk):
        lanes = slice(c * LANES, (c + 1) * LANES)
        o_ref[:, lanes] = jax.nn.gelu(y_ref[c] + d_ref[:, lanes] * hn_ref[c]).astype(BF16)


def _ssm_core(x, gain, d_skip, wb_r, wb_i, wc_r, wc_i, ab_r, ab_i):
    s, d = x.shape
    t = min(SSM_ROW_TILE, s)
    nblk, cb, ns = wb_r.shape
    ppb = SSM_PAIRS_PER_BATCH
    n_batches = nblk // (2 * ppb)

    def whole(a):
        return pl.BlockSpec(a.shape, lambda i: (0,) * a.ndim)

    return pl.pallas_call(
        _ssm_kernel,
        grid=(s // t,),
        in_specs=[
            pl.BlockSpec((t, d), lambda i: (i, 0)),
            whole(gain), whole(d_skip),
            whole(wb_r), whole(wb_i), whole(wc_r), whole(wc_i),
            whole(ab_r), whole(ab_i),
        ],
        out_specs=pl.BlockSpec((t, d), lambda i: (i, 0)),
        out_shape=jax.ShapeDtypeStruct((s, d), BF16),
        scratch_shapes=[
            pltpu.VMEM((nblk, t, LANES), F32),
            pltpu.VMEM((nblk, t, LANES), F32),
            pltpu.VMEM((n_batches, ppb, t * SUBLANES, LANES), F32),
            pltpu.VMEM((n_batches, ppb, t * SUBLANES, LANES), F32),
            pltpu.VMEM((nblk // 2, SUBLANES, LANES), F32),
            pltpu.VMEM((nblk // 2, SUBLANES, LANES), F32),
        ],
        compiler_params=_params("arbitrary"),
        name="ssm_core",
    )(x, gain, d_skip, wb_r, wb_i, wc_r, wc_i, ab_r, ab_i)


def _block_diag(w, groups_per_block):
    g, r, c = w.shape
    nblk = g // groups_per_block
    tiled = jnp.tile(w.reshape(nblk, groups_per_block * r, c), (1, 1, groups_per_block))
    row_group = lax.broadcasted_iota(jnp.int32, tiled.shape, 1) // r
    col_group = lax.broadcasted_iota(jnp.int32, tiled.shape, 2) // c
    return jnp.where(row_group == col_group, tiled, jnp.zeros_like(tiled))


def _s5_mixer(x, gain, a_re, a_im, log_dt, b_re, b_im, c_re, c_im, d_skip, w_glu, layer):
    s, d = x.shape
    g, n = a_re.shape
    gpb = SSM_GROUPS_PER_BLOCK
    bt_re, bt_im = b_re.transpose(2, 0, 1), b_im.transpose(2, 0, 1)
    ab_r, ab_i, bb_r, bb_i = _ssm_prep(a_re, a_im, log_dt.reshape(g, 1), bt_re, bt_im)
    wb_r = _block_diag(bb_r.transpose(1, 0, 2), gpb).astype(BF16)
    wb_i = _block_diag(bb_i.transpose(1, 0, 2), gpb).astype(BF16)
    wc_r = _block_diag(c_re.transpose(0, 2, 1), gpb).astype(BF16)
    wc_i = _block_diag(-c_im.transpose(0, 2, 1), gpb).astype(BF16)
    slabs = lambda a: a.reshape(g // (2 * gpb), 2 * gpb * n // LANES, LANES)
    z = _ssm_core(x, gain, d_skip.reshape(1, d), wb_r, wb_i, wc_r, wc_i, slabs(ab_r), slabs(ab_i))
    return _glu_res(z, w_glu, x, layer)


def kernel(x, norm_mix, norm_mlp, mlp_w1, mlp_w2, conv_w_in, conv_w, conv_w_out, pool_w_in, pool_w_group, pool_scale, att_w_qkv, att_q_norm, att_k_norm, att_rel_bias, att_w_out, ssm_a_re, ssm_a_im, ssm_log_dt, ssm_b_re, ssm_b_im, ssm_c_re, ssm_c_im, ssm_d, ssm_w_glu):
    b, s, d = x.shape
    depth = norm_mix.shape[0]
    n_mixers = 4
    outs = []
    for bi in range(b):
        xs = x[bi]
        for i in range(depth):
            kind, j = i % n_mixers, i // n_mixers
            gain = norm_mix[i].reshape(1, d)
            if kind == 0:
                gated = _conv_front(xs, gain, conv_w_in, conv_w[j], j)
                xs = _proj_res(gated, conv_w_out, xs, j)
            elif kind == 1:
                xs = _pool_mixer(xs, gain, pool_w_in, pool_w_group, pool_scale[j].reshape(1, d), j)
            elif kind == 2:
                n_heads = d // ATT_HEAD_DIM
                head_gains = jnp.concatenate([jnp.tile(att_q_norm[j], n_heads),
                                              jnp.tile(att_k_norm[j], n_heads)]).reshape(1, 2 * d)
                qkv = _qkv_proj(xs, gain, att_w_qkv, head_gains, j)
                att = _attention(qkv, _attention_profile(att_rel_bias[j]), n_heads)
                xs = _proj_res(att, att_w_out, xs, j)
            else:
                xs = _s5_mixer(xs, gain, ssm_a_re[j], ssm_a_im[j], ssm_log_dt[j], ssm_b_re[j],
                               ssm_b_im[j], ssm_c_re[j], ssm_c_im[j], ssm_d[j], ssm_w_glu, j)
            xs = _mlp(xs, norm_mlp[i].reshape(1, d), mlp_w1, mlp_w2, i)
        outs.append(xs)
    return outs[0][None] if b == 1 else jnp.stack(outs)
```

```python
import functools
import math

import jax
import jax.numpy as jnp
from jax import lax
from jax.experimental import pallas as pl
from jax.experimental.pallas import tpu as pltpu

F32 = jnp.float32
BF16 = jnp.bfloat16

RMS_EPS = 1e-6
CHUNK = 64
ATT_HEAD_DIM = 128
ATT_LEFT_CHUNKS = 8
REL_CLIP = 256
MASK_VALUE = -1e30
POOL_WINDOWS = (2, 4, 8, 16)
CONV_WIDTH = 3
SSM_GROUP = 16
SSM_STATE = 64

SUBLANES = 8
LANES = 128
VMEM_LIMIT_BYTES = 56 * 1024 * 1024

ROW_TILE = 1024
COL_TILE = 512
QKV_COL_TILE = 1024
PROJ_ROW_TILE = 512
PROJ_COL_TILE = 2048
GLU_COL_TILE = 1024
FF_TILE = 512
MLP_OUT_TILE = 512
MLP_K_TILE = 2048
ATT_QBLOCK = 256
ATT_HEADS_PER_STEP = 8
SSM_ROW_TILE = 256
SSM_PAIRS_PER_BATCH = 4
SSM_GROUPS_PER_BLOCK = 8
HALO = 16
POOL_HALO = 32


def _params(*semantics):
    return pltpu.CompilerParams(dimension_semantics=semantics, vmem_limit_bytes=VMEM_LIMIT_BYTES)


def _rms_norm(x, gain):
    return x * lax.rsqrt(jnp.mean(x * x, axis=-1, keepdims=True) + RMS_EPS) * gain


def _dot(a, b):
    return jnp.dot(a, b, preferred_element_type=F32)


def _wdot(a, w_ref):
    return _dot(a, w_ref[...].astype(BF16))


def _mlp_kernel(x_hbm, g_ref, w1_hbm, w2_hbm, xr_ref, o_ref, x_buf, h_ref, a_ref, w1_buf, w2_buf,
                semx, sem1, sem2, *, layer, n_rows):
    i, t = pl.program_id(0), pl.program_id(1)
    n_out = pl.num_programs(1) - 1
    tf, tk, tn = w1_buf.shape[2], w2_buf.shape[1], w2_buf.shape[2]
    n_up, n_k = a_ref.shape[1] // tf, a_ref.shape[1] // tk
    assert n_up % 2 == 0 and n_k % 2 == 0

    def w1_copy(f, slot):
        return pltpu.make_async_copy(w1_hbm.at[layer, :, pl.ds(f * tf, tf)], w1_buf.at[slot],
                                     sem1.at[slot])

    def w2_copy(n, kh):
        return pltpu.make_async_copy(w2_hbm.at[layer, pl.ds(kh * tk, tk), pl.ds(n * tn, tn)],
                                     w2_buf.at[kh % 2], sem2.at[kh % 2])

    def x_copy(row):
        tm = x_buf.shape[0]
        return pltpu.make_async_copy(x_hbm.at[pl.ds(row * tm, tm), :], x_buf, semx.at[0])

    @pl.when(t == 0)
    def _():
        @pl.when(i == 0)
        def _():
            x_copy(0).start()
            w1_copy(0, 0).start()

        w2_copy(0, 0).start()
        w2_copy(0, 1).start()
        x_copy(i).wait()
        h_ref[...] = _rms_norm(x_buf[...], g_ref[...]).astype(BF16)

        @pl.when(i + 1 < n_rows)
        def _():
            x_copy(i + 1).start()

        def up_pair(p, _):
            for slot in (0, 1):
                f = 2 * p + slot
                w1_copy(f, slot).wait()

                @pl.when(f + 1 < n_up)
                def _():
                    w1_copy(f + 1, 1 - slot).start()

                a = jnp.maximum(_dot(h_ref[...], w1_buf[slot].astype(BF16)), 0.0)
                a_ref[:, pl.ds(pl.multiple_of(f * tf, tf), tf)] = (a * a).astype(BF16)
            return 0

        lax.fori_loop(0, n_up // 2, up_pair, 0)

    @pl.when(t > 0)
    def _():
        n = t - 1
        for kh in range(n_k):
            w2_copy(n, kh).wait()
            part = _dot(a_ref[:, kh * tk:(kh + 1) * tk], w2_buf[kh % 2].astype(BF16))
            if kh == 0:
                o_ref[...] = xr_ref[...] + part
            else:
                o_ref[...] += part
            if kh + 2 < n_k:
                w2_copy(n, kh + 2).start()
            else:
                @pl.when(n + 1 < n_out)
                def _(kh=kh):
                    w2_copy(n + 1, kh + 2 - n_k).start()

        @pl.when((n + 1 == n_out) & (i + 1 < n_rows))
        def _():
            w1_copy(0, 0).start()


def _mlp(x, gain, w1, w2, layer):
    s, d = x.shape
    dff = w1.shape[2]
    tm, tf, tn, tk = min(ROW_TILE, s), FF_TILE, MLP_OUT_TILE, MLP_K_TILE
    n_rows, n_out = s // tm, d // tn

    def out_tile(t):
        return jnp.maximum(t - 1, 0)

    return pl.pallas_call(
        functools.partial(_mlp_kernel, layer=layer, n_rows=n_rows),
        grid=(n_rows, 1 + n_out),
        in_specs=[
            pl.BlockSpec(memory_space=pl.ANY),
            pl.BlockSpec((1, d), lambda i, t: (0, 0)),
            pl.BlockSpec(memory_space=pl.ANY),
            pl.BlockSpec(memory_space=pl.ANY),
            pl.BlockSpec((tm, tn), lambda i, t: (i, out_tile(t))),
        ],
        out_specs=pl.BlockSpec((tm, tn), lambda i, t: (i, out_tile(t))),
        out_shape=jax.ShapeDtypeStruct((s, d), F32),
        scratch_shapes=[
            pltpu.VMEM((tm, d), F32),
            pltpu.VMEM((tm, d), BF16),
            pltpu.VMEM((tm, dff), BF16),
            pltpu.VMEM((2, d, tf), F32),
            pltpu.VMEM((2, tk, tn), F32),
            pltpu.SemaphoreType.DMA((1,)),
            pltpu.SemaphoreType.DMA((2,)),
            pltpu.SemaphoreType.DMA((2,)),
        ],
        compiler_params=_params("arbitrary", "arbitrary"),
        name="mlp",
    )(x, gain, w1, w2, x)


def _proj_res_kernel(a_ref, w_ref, x_ref, o_ref, wb_ref):
    @pl.when(pl.program_id(1) == 0)
    def _():
        wb_ref[...] = w_ref[...].astype(BF16)

    o_ref[...] = x_ref[...] + _dot(a_ref[...], wb_ref[...])


def _proj_res(a, w, x, layer):
    s, k = a.shape
    n = w.shape[2]
    tm, tn = min(PROJ_ROW_TILE, s), min(PROJ_COL_TILE, n)
    return pl.pallas_call(
        _proj_res_kernel,
        grid=(n // tn, s // tm),
        in_specs=[
            pl.BlockSpec((tm, k), lambda j, i: (i, 0)),
            pl.BlockSpec((None, k, tn), lambda j, i: (layer, 0, j), pipeline_mode=pl.Buffered(1)),
            pl.BlockSpec((tm, tn), lambda j, i: (i, j)),
        ],
        out_specs=pl.BlockSpec((tm, tn), lambda j, i: (i, j)),
        out_shape=jax.ShapeDtypeStruct((s, n), F32),
        scratch_shapes=[pltpu.VMEM((k, tn), BF16)],
        compiler_params=_params("arbitrary", "arbitrary"),
        name="proj_res",
    )(a, w, x)


def _glu_res_kernel(a_ref, wv_ref, wg_ref, x_ref, o_ref, wvb_ref, wgb_ref):
    @pl.when(pl.program_id(1) == 0)
    def _():
        wvb_ref[...] = wv_ref[...].astype(BF16)
        wgb_ref[...] = wg_ref[...].astype(BF16)

    a = a_ref[...]
    val = _dot(a, wvb_ref[...])
    gate = _dot(a, wgb_ref[...])
    o_ref[...] = x_ref[...] + val * jax.nn.sigmoid(gate)


def _glu_res(a, w_glu, x, layer):
    s, k = a.shape
    n = w_glu.shape[2] // 2
    tm, tn = min(PROJ_ROW_TILE, s), min(GLU_COL_TILE, n)
    nj = n // tn
    return pl.pallas_call(
        _glu_res_kernel,
        grid=(nj, s // tm),
        in_specs=[
            pl.BlockSpec((tm, k), lambda j, i: (i, 0)),
            pl.BlockSpec((None, k, tn), lambda j, i: (layer, 0, j), pipeline_mode=pl.Buffered(1)),
            pl.BlockSpec((None, k, tn), lambda j, i: (layer, 0, j + nj), pipeline_mode=pl.Buffered(1)),
            pl.BlockSpec((tm, tn), lambda j, i: (i, j)),
        ],
        out_specs=pl.BlockSpec((tm, tn), lambda j, i: (i, j)),
        out_shape=jax.ShapeDtypeStruct((s, n), F32),
        scratch_shapes=[pltpu.VMEM((k, tn), BF16), pltpu.VMEM((k, tn), BF16)],
        compiler_params=_params("arbitrary", "arbitrary"),
        name="glu_res",
    )(a, w_glu, w_glu, x)


def _conv_front_kernel(x_ref, g_ref, wb_ref, wc_ref, wv_ref, cw_ref, o_ref, h_ref, ext_ref, carry_ref):
    i, j = pl.program_id(0), pl.program_id(1)
    tm = o_ref.shape[0]

    @pl.when(j == 0)
    def _():
        h_ref[...] = _rms_norm(x_ref[...], g_ref[...]).astype(BF16)

    @pl.when(i == 0)
    def _():
        carry_ref[j] = jnp.zeros(carry_ref.shape[1:], F32)

    h = h_ref[...]
    u = _wdot(h, wc_ref) * _wdot(h, wv_ref)
    ext_ref[0:HALO, :] = carry_ref[j]
    ext_ref[HALO:, :] = u
    carry_ref[j] = u[tm - HALO:, :]
    conv = cw_ref[CONV_WIDTH - 1:CONV_WIDTH, :] * u
    for lag in range(1, CONV_WIDTH):
        tap = CONV_WIDTH - 1 - lag
        conv += cw_ref[tap:tap + 1, :] * ext_ref[pl.ds(HALO - lag, tm), :]
    o_ref[...] = (_wdot(h, wb_ref) * conv).astype(BF16)


def _conv_front(x, gain, w_in, conv_w, layer):
    s, d = x.shape
    tm, tn = min(ROW_TILE, s), COL_TILE
    nj = d // tn
    return pl.pallas_call(
        _conv_front_kernel,
        grid=(s // tm, nj),
        in_specs=[
            pl.BlockSpec((tm, d), lambda i, j: (i, 0)),
            pl.BlockSpec((1, d), lambda i, j: (0, 0)),
            pl.BlockSpec((None, d, tn), lambda i, j: (layer, 0, j)),
            pl.BlockSpec((None, d, tn), lambda i, j: (layer, 0, j + nj)),
            pl.BlockSpec((None, d, tn), lambda i, j: (layer, 0, j + 2 * nj)),
            pl.BlockSpec((CONV_WIDTH, tn), lambda i, j: (0, j)),
        ],
        out_specs=pl.BlockSpec((tm, tn), lambda i, j: (i, j)),
        out_shape=jax.ShapeDtypeStruct((s, d), BF16),
        scratch_shapes=[
            pltpu.VMEM((tm, d), BF16),
            pltpu.VMEM((tm + HALO, tn), F32),
            pltpu.VMEM((nj, HALO, tn), F32),
        ],
        compiler_params=_params("arbitrary", "arbitrary"),
        name="conv_front",
    )(x, gain, w_in, w_in, w_in, conv_w)


def _pool_kernel(x_ref, g_ref, win_ref, wg_ref, sc_ref, xr_ref, o_ref, h_ref, ext_ref, la_ref, lb_ref,
                 carry_ref, winb_ref, wgb_ref):
    i, grp = pl.program_id(0), pl.program_id(1)
    tm = o_ref.shape[0]
    halo = POOL_HALO
    n_ext = tm + halo

    @pl.when(grp == 0)
    def _():
        h_ref[...] = _rms_norm(x_ref[...], g_ref[...]).astype(BF16)

    @pl.when(i == 0)
    def _():
        carry_ref[grp] = jnp.zeros(carry_ref.shape[1:], F32)
        winb_ref[grp] = win_ref[...].astype(BF16)
        wgb_ref[grp] = wg_ref[0].astype(BF16)

    u = _dot(h_ref[...], winb_ref[grp])
    ext_ref[0:halo, :] = carry_ref[grp]
    ext_ref[halo:, :] = u
    carry_ref[grp] = u[tm - halo:, :]
    pos = (i * tm + 1 + lax.broadcasted_iota(jnp.int32, (tm, 1), 0)).astype(F32)

    for gi, w in enumerate(POOL_WINDOWS):
        @pl.when(grp == gi)
        def _(w=w):
            levels = int(math.log2(w))
            assert 2 ** levels == w and SUBLANES * levels <= halo
            src = ext_ref
            for lv in range(1, levels):
                dst = la_ref if lv % 2 else lb_ref
                lo, shift = SUBLANES * lv, 2 ** (lv - 1)
                dst[lo:, :] = src[lo:, :] + src[pl.ds(lo - shift, n_ext - lo), :]
                src = dst
            acc = src[halo:, :] + src[pl.ds(halo - w // 2, tm), :]
            inv_count = 1.0 / jnp.minimum(pos, float(w))
            pooled = (acc * inv_count - u).astype(BF16)
            o_ref[...] = xr_ref[...] + _dot(pooled, wgb_ref[gi]) * sc_ref[...]


def _pool_mixer(x, gain, w_in, w_group, scale, layer):
    s, d = x.shape
    ng, pg = w_group.shape[1], w_group.shape[2]
    tm = min(ROW_TILE, s)

    def first_tile_only(i, g):
        return jnp.where(i == 0, g, ng - 1)

    return pl.pallas_call(
        _pool_kernel,
        grid=(s // tm, ng),
        in_specs=[
            pl.BlockSpec((tm, d), lambda i, g: (i, 0)),
            pl.BlockSpec((1, d), lambda i, g: (0, 0)),
            pl.BlockSpec((None, d, pg), lambda i, g: (layer, 0, first_tile_only(i, g)),
                         pipeline_mode=pl.Buffered(1)),
            pl.BlockSpec((None, 1, pg, pg), lambda i, g: (layer, first_tile_only(i, g), 0, 0)),
            pl.BlockSpec((1, pg), lambda i, g: (0, g)),
            pl.BlockSpec((tm, pg), lambda i, g: (i, g)),
        ],
        out_specs=pl.BlockSpec((tm, pg), lambda i, g: (i, g)),
        out_shape=jax.ShapeDtypeStruct((s, d), F32),
        scratch_shapes=[
            pltpu.VMEM((tm, d), BF16),
            pltpu.VMEM((tm + POOL_HALO, pg), F32),
            pltpu.VMEM((tm + POOL_HALO, pg), F32),
            pltpu.VMEM((tm + POOL_HALO, pg), F32),
            pltpu.VMEM((ng, POOL_HALO, pg), F32),
            pltpu.VMEM((ng, d, pg), BF16),
            pltpu.VMEM((ng, pg, pg), BF16),
        ],
        compiler_params=_params("arbitrary", "arbitrary"),
        name="pool_mixer",
    )(x, gain, w_in, w_group, scale, x)


def _qkv_kernel(x_ref, g_ref, w_ref, hg_ref, o_ref, h_ref, *, n_norm_tiles):
    j = pl.program_id(1)

    @pl.when(j == 0)
    def _():
        h_ref[...] = _rms_norm(x_ref[...], g_ref[...]).astype(BF16)

    y = _wdot(h_ref[...], w_ref)

    @pl.when(j < n_norm_tiles)
    def _():
        for hd in range(y.shape[1] // ATT_HEAD_DIM):
            sl = slice(hd * ATT_HEAD_DIM, (hd + 1) * ATT_HEAD_DIM)
            o_ref[:, sl] = _rms_norm(y[:, sl], hg_ref[:, sl]).astype(BF16)

    @pl.when(j >= n_norm_tiles)
    def _():
        o_ref[...] = y.astype(BF16)


def _qkv_proj(x, gain, w_qkv, head_gains, layer):
    s, d = x.shape
    n = w_qkv.shape[2]
    tm, tn = min(ROW_TILE, s), QKV_COL_TILE
    n_norm_tiles = head_gains.shape[1] // tn
    return pl.pallas_call(
        functools.partial(_qkv_kernel, n_norm_tiles=n_norm_tiles),
        grid=(s // tm, n // tn),
        in_specs=[
            pl.BlockSpec((tm, d), lambda i, j: (i, 0)),
            pl.BlockSpec((1, d), lambda i, j: (0, 0)),
            pl.BlockSpec((None, d, tn), lambda i, j: (layer, 0, j)),
            pl.BlockSpec((1, tn), lambda i, j: (0, jnp.minimum(j, n_norm_tiles - 1))),
        ],
        out_specs=pl.BlockSpec((tm, tn), lambda i, j: (i, j)),
        out_shape=jax.ShapeDtypeStruct((s, n), BF16),
        scratch_shapes=[pltpu.VMEM((tm, d), BF16)],
        compiler_params=_params("parallel", "arbitrary"),
        name="qkv_proj",
    )(x, gain, w_qkv, head_gains)


def _attn_kernel(q_ref, k0_ref, k1_ref, k2_ref, v0_ref, v1_ref, v2_ref, prof_ref, o_ref, bias_ref):
    b = pl.program_id(1)
    qb = q_ref.shape[0]
    n_local = q_ref.shape[1] // ATT_HEAD_DIM
    log2e = math.log2(math.e)
    scale = ATT_HEAD_DIM ** -0.5 * log2e

    @pl.when(b == 0)
    def _():
        q_idx = lax.broadcasted_iota(jnp.int32, (qb, 3 * qb), 0) + 2 * qb
        k_idx = lax.broadcasted_iota(jnp.int32, (qb, 3 * qb), 1)
        chunk_start = (q_idx // CHUNK) * CHUNK
        in_band = (k_idx >= chunk_start - ATT_LEFT_CHUNKS * CHUNK) & (k_idx < chunk_start + CHUNK)
        for hh in range(n_local):
            profile = jnp.broadcast_to(prof_ref[hh], (qb, prof_ref.shape[2]))
            table = pltpu.roll(profile, 0, 1, stride=1, stride_axis=0)[:, :3 * qb]
            bias_ref[hh] = jnp.where(in_band, table * log2e, MASK_VALUE)

    def lane_slabs(a):
        return [a[:, k * LANES:(k + 1) * LANES] for k in range(a.shape[1] // LANES)]

    def heads(first_blocks):
        for hh in range(n_local):
            cols = slice(hh * ATT_HEAD_DIM, (hh + 1) * ATT_HEAD_DIM)
            q = q_ref[:, cols]
            scores = []
            for c, k_ref in enumerate((k0_ref, k1_ref, k2_ref)):
                s_c = lax.dot_general(q, k_ref[:, cols], (((1,), (1,)), ((), ())),
                                      preferred_element_type=F32)
                s_c = s_c * scale + bias_ref[hh, :, c * qb:(c + 1) * qb]
                if first_blocks and c < 2:
                    s_c = jnp.where(b - 2 + c >= 0, s_c, MASK_VALUE)
                scores.append(s_c)
            m = jnp.max(functools.reduce(jnp.maximum,
                                         [sl for s_c in scores for sl in lane_slabs(s_c)]),
                        axis=-1, keepdims=True)
            acc = jnp.zeros((qb, 2 * ATT_HEAD_DIM), F32)
            ones = jnp.ones((qb, ATT_HEAD_DIM), BF16)
            for s_c, v_ref in zip(scores, (v0_ref, v1_ref, v2_ref)):
                p = jnp.exp2((s_c - m).astype(BF16))
                acc += _dot(p, jnp.concatenate([v_ref[:, cols], ones], axis=1))
            o_ref[:, cols] = (acc[:, :ATT_HEAD_DIM] / acc[:, ATT_HEAD_DIM:]).astype(BF16)

    @pl.when(b < 2)
    def _():
        heads(True)

    @pl.when(b >= 2)
    def _():
        heads(False)


def _attention(qkv, profile, n_heads):
    s = qkv.shape[0]
    qb = ATT_QBLOCK
    nb = s // qb
    hps = ATT_HEADS_PER_STEP
    width = hps * ATT_HEAD_DIM
    ngroups = n_heads // hps

    def kv_spec(offset, back):
        return pl.BlockSpec((qb, width), lambda h, b: (jnp.maximum(b - back, 0), offset + h))

    return pl.pallas_call(
        _attn_kernel,
        grid=(ngroups, nb),
        in_specs=[
            pl.BlockSpec((qb, width), lambda h, b: (b, h)),
            kv_spec(ngroups, 2), kv_spec(ngroups, 1), kv_spec(ngroups, 0),
            kv_spec(2 * ngroups, 2), kv_spec(2 * ngroups, 1), kv_spec(2 * ngroups, 0),
            pl.BlockSpec((hps, 1, 4 * qb), lambda h, b: (h, 0, 0)),
        ],
        out_specs=pl.BlockSpec((qb, width), lambda h, b: (b, h)),
        out_shape=jax.ShapeDtypeStruct((s, n_heads * ATT_HEAD_DIM), BF16),
        scratch_shapes=[pltpu.VMEM((hps, qb, 3 * qb), F32)],
        compiler_params=_params("arbitrary", "arbitrary"),
        name="chunk_attention",
    )(qkv, qkv, qkv, qkv, qkv, qkv, qkv, profile)


def _attention_profile(rel_bias):
    qb = ATT_QBLOCK
    n_heads = rel_bias.shape[0]
    assert REL_CLIP <= 2 * qb and qb <= REL_CLIP + 1 and 2 * REL_CLIP + 1 == rel_bias.shape[1]
    n_flat = 2 * qb - REL_CLIP + 1
    top = rel_bias[:, 2 * REL_CLIP:]
    profile = jnp.concatenate([
        jnp.broadcast_to(top, (n_heads, n_flat)),
        jnp.flip(rel_bias[:, REL_CLIP - qb + 1:2 * REL_CLIP], axis=1),
        jnp.broadcast_to(top, (n_heads, qb)),
    ], axis=1)
    return profile.reshape(n_heads, 1, 4 * qb).astype(F32)


def _ssm_prep_kernel(are_ref, aim_ref, ldt_ref, btr_ref, bti_ref, abr_ref, abi_ref, bbr_ref, bbi_ref):
    lam_r, lam_i = are_ref[...], aim_ref[...]
    dt = jnp.exp(ldt_ref[...])
    mag = jnp.exp(lam_r * dt)
    ab_r, ab_i = mag * jnp.cos(lam_i * dt), mag * jnp.sin(lam_i * dt)
    abr_ref[...] = ab_r
    abi_ref[...] = ab_i
    num_r, num_i = ab_r - 1.0, ab_i
    den = lam_r * lam_r + lam_i * lam_i
    co_r = (num_r * lam_r + num_i * lam_i) / den
    co_i = (num_i * lam_r - num_r * lam_i) / den
    for c in range(btr_ref.shape[0]):
        b_r, b_i = btr_ref[c], bti_ref[c]
        bbr_ref[c] = co_r * b_r - co_i * b_i
        bbi_ref[c] = co_r * b_i + co_i * b_r


def _ssm_prep(a_re, a_im, log_dt, bt_re, bt_im):
    g, n = a_re.shape
    gn = jax.ShapeDtypeStruct((g, n), F32)
    cgn = jax.ShapeDtypeStruct(bt_re.shape, F32)
    return pl.pallas_call(
        _ssm_prep_kernel,
        out_shape=(gn, gn, cgn, cgn),
        name="ssm_prep",
    )(a_re, a_im, log_dt, bt_re, bt_im)


def _ssm_kernel(x_ref, g_ref, d_ref, wbr_ref, wbi_ref, wcr_ref, wci_ref, ar_ref, ai_ref, o_ref,
                hn_ref, y_ref, xr_ref, xi_ref, cr_ref, ci_ref):
    i = pl.program_id(0)
    t = x_ref.shape[0]
    nblk, cb, ns = wbr_ref.shape
    n_batches, ppb = xr_ref.shape[0], xr_ref.shape[1]
    nsub = ns // LANES
    assert cb == LANES and 2 * nsub == SUBLANES and n_batches * ppb * 2 == nblk

    h = _rms_norm(x_ref[...], g_ref[...])
    for c in range(nblk):
        hn_ref[c] = h[:, c * LANES:(c + 1) * LANES]

    @pl.when(i == 0)
    def _():
        cr_ref[...] = jnp.zeros(cr_ref.shape, F32)
        ci_ref[...] = jnp.zeros(ci_ref.shape, F32)

    def slab_rows(half, k):
        return pl.ds(half * nsub + k, t, stride=SUBLANES)

    def blocks_of(batch):
        for g8 in range(2 * ppb):
            yield batch * 2 * ppb + g8, g8 // 2, g8 % 2

    def run(*stages):
        stages = list(stages)
        while stages:
            for stage in list(stages):
                if next(stage, "done") == "done":
                    stages.remove(stage)

    def project_in(batch):
        for gb, pair, half in blocks_of(batch):
            hb = hn_ref[gb].astype(BF16)
            for w_ref, dst_ref in ((wbr_ref, xr_ref), (wbi_ref, xi_ref)):
                bu = _dot(hb, w_ref[gb])
                for k in range(nsub):
                    dst_ref[batch, pair, slab_rows(half, k), :] = bu[:, k * LANES:(k + 1) * LANES]
            yield

    def scan(batch):
        coef = [(ar_ref[batch * ppb + p], ai_ref[batch * ppb + p]) for p in range(ppb)]
        steps_per_turn = t // (2 * ppb)

        def step(tt, carry):
            rows = pl.ds(tt * SUBLANES, SUBLANES)
            out = []
            for p in range(ppb):
                (a_r, a_i), x_r, x_i = coef[p], carry[2 * p], carry[2 * p + 1]
                n_r = a_r * x_r - a_i * x_i + xr_ref[batch, p, rows, :]
                n_i = a_r * x_i + a_i * x_r + xi_ref[batch, p, rows, :]
                xr_ref[batch, p, rows, :] = n_r
                xi_ref[batch, p, rows, :] = n_i
                out += [n_r, n_i]
            return tuple(out)

        last = tuple(ref[batch * ppb + p] for p in range(ppb) for ref in (cr_ref, ci_ref))
        for tt in range(t):
            last = step(tt, last)
            if (tt + 1) % steps_per_turn == 0 and tt + 1 < t:
                yield
        for p in range(ppb):
            cr_ref[batch * ppb + p] = last[2 * p]
            ci_ref[batch * ppb + p] = last[2 * p + 1]
        yield

    def project_out(batch):
        for gb, pair, half in blocks_of(batch):
            s_r, s_i = (jnp.concatenate([ref[batch, pair, slab_rows(half, k), :] for k in range(nsub)],
                                        axis=1).astype(BF16) for ref in (xr_ref, xi_ref))
            y_ref[gb] = _dot(s_r, wcr_ref[gb]) + _dot(s_i, wci_ref[gb])
            yield

    def finish(batch):
        for gb, _, _ in blocks_of(batch):
            lanes = slice(gb * LANES, (gb + 1) * LANES)
            o_ref[:, lanes] = jax.nn.gelu(y_ref[gb] + d_ref[:, lanes] * hn_ref[gb]).astype(BF16)
            yield

    run(project_in(0))
    for batch in range(n_batches):
        beside = [project_in(batch + 1)] if batch + 1 < n_batches else []
        beside += [project_out(batch - 1)] if batch > 0 else []
        beside += [finish(batch - 2)] if batch > 1 else []
        run(scan(batch), *beside)
    run(project_out(n_batches - 1), *([finish(n_batches - 2)] if n_batches > 1 else []))
    run(finish(n_batches - 1))


def _ssm_core(x, gain, d_skip, wb_r, wb_i, wc_r, wc_i, ab_r, ab_i):
    s, d = x.shape
    t = min(SSM_ROW_TILE, s)
    nblk, cb, ns = wb_r.shape
    ppb = SSM_PAIRS_PER_BATCH
    n_batches = nblk // (2 * ppb)

    def whole(a):
        return pl.BlockSpec(a.shape, lambda i: (0,) * a.ndim)

    return pl.pallas_call(
        _ssm_kernel,
        grid=(s // t,),
        in_specs=[
            pl.BlockSpec((t, d), lambda i: (i, 0)),
            whole(gain), whole(d_skip),
            whole(wb_r), whole(wb_i), whole(wc_r), whole(wc_i),
            whole(ab_r), whole(ab_i),
        ],
        out_specs=pl.BlockSpec((t, d), lambda i: (i, 0)),
        out_shape=jax.ShapeDtypeStruct((s, d), BF16),
        scratch_shapes=[
            pltpu.VMEM((nblk, t, LANES), F32),
            pltpu.VMEM((nblk, t, LANES), F32),
            pltpu.VMEM((n_batches, ppb, t * SUBLANES, LANES), F32),
            pltpu.VMEM((n_batches, ppb, t * SUBLANES, LANES), F32),
            pltpu.VMEM((nblk // 2, SUBLANES, LANES), F32),
            pltpu.VMEM((nblk // 2, SUBLANES, LANES), F32),
        ],
        compiler_params=_params("arbitrary"),
        name="ssm_core",
    )(x, gain, d_skip, wb_r, wb_i, wc_r, wc_i, ab_r, ab_i)


def _block_diag(w, groups_per_block):
    g, r, c = w.shape
    nblk = g // groups_per_block
    tiled = jnp.tile(w.reshape(nblk, groups_per_block * r, c), (1, 1, groups_per_block))
    row_group = lax.broadcasted_iota(jnp.int32, tiled.shape, 1) // r
    col_group = lax.broadcasted_iota(jnp.int32, tiled.shape, 2) // c
    return jnp.where(row_group == col_group, tiled, jnp.zeros_like(tiled))


def _s5_mixer(x, gain, a_re, a_im, log_dt, b_re, b_im, c_re, c_im, d_skip, w_glu, layer):
    s, d = x.shape
    g, n = a_re.shape
    gpb = SSM_GROUPS_PER_BLOCK
    bt_re, bt_im = b_re.transpose(2, 0, 1), b_im.transpose(2, 0, 1)
    ab_r, ab_i, bb_r, bb_i = _ssm_prep(a_re, a_im, log_dt.reshape(g, 1), bt_re, bt_im)
    wb_r = _block_diag(bb_r.transpose(1, 0, 2), gpb).astype(BF16)
    wb_i = _block_diag(bb_i.transpose(1, 0, 2), gpb).astype(BF16)
    wc_r = _block_diag(c_re.transpose(0, 2, 1), gpb).astype(BF16)
    wc_i = _block_diag(-c_im.transpose(0, 2, 1), gpb).astype(BF16)
    slabs = lambda a: a.reshape(g // (2 * gpb), 2 * gpb * n // LANES, LANES)
    z = _ssm_core(x, gain, d_skip.reshape(1, d), wb_r, wb_i, wc_r, wc_i, slabs(ab_r), slabs(ab_i))
    return _glu_res(z, w_glu, x, layer)


def kernel(x, norm_mix, norm_mlp, mlp_w1, mlp_w2, conv_w_in, conv_w, conv_w_out, pool_w_in, pool_w_group, pool_scale, att_w_qkv, att_q_norm, att_k_norm, att_rel_bias, att_w_out, ssm_a_re, ssm_a_im, ssm_log_dt, ssm_b_re, ssm_b_im, ssm_c_re, ssm_c_im, ssm_d, ssm_w_glu):
    b, s, d = x.shape
    depth = norm_mix.shape[0]
    n_mixers = 4
    outs = []
    for bi in range(b):
        xs = x[bi]
        for i in range(depth):
            kind, j = i % n_mixers, i // n_mixers
            gain = norm_mix[i].reshape(1, d)
            if kind == 0:
                gated = _conv_front(xs, gain, conv_w_in, conv_w[j], j)
                xs = _proj_res(gated, conv_w_out, xs, j)
            elif kind == 1:
                xs = _pool_mixer(xs, gain, pool_w_in, pool_w_group, pool_scale[j].reshape(1, d), j)
            elif kind == 2:
                n_heads = d // ATT_HEAD_DIM
                head_gains = jnp.concatenate([jnp.tile(att_q_norm[j], n_heads),
                                              jnp.tile(att_k_norm[j], n_heads)]).reshape(1, 2 * d)
                qkv = _qkv_proj(xs, gain, att_w_qkv, head_gains, j)
                att = _attention(qkv, _attention_profile(att_rel_bias[j]), n_heads)
                xs = _proj_res(att, att_w_out, xs, j)
            else:
                xs = _s5_mixer(xs, gain, ssm_a_re[j], ssm_a_im[j], ssm_log_dt[j], ssm_b_re[j],
                               ssm_b_im[j], ssm_c_re[j], ssm_c_im[j], ssm_d[j], ssm_w_glu, j)
            xs = _mlp(xs, norm_mlp[i].reshape(1, d), mlp_w1, mlp_w2, i)
        outs.append(xs)
    return outs[0][None] if b == 1 else jnp.stack(outs)
```

```python
import functools
import math

import jax
import jax.numpy as jnp
from jax import lax
from jax.experimental import pallas as pl
from jax.experimental.pallas import tpu as pltpu

F32 = jnp.float32
BF16 = jnp.bfloat16

RMS_EPS = 1e-6
CHUNK = 64
ATT_HEAD_DIM = 128
ATT_LEFT_CHUNKS = 8
REL_CLIP = 256
MASK_VALUE = -1e30
POOL_WINDOWS = (2, 4, 8, 16)
CONV_WIDTH = 3
SSM_GROUP = 16
SSM_STATE = 64

SUBLANES = 8
LANES = 128
VMEM_LIMIT_BYTES = 56 * 1024 * 1024

ROW_TILE = 1024
COL_TILE = 512
QKV_COL_TILE = 1024
PROJ_ROW_TILE = 512
PROJ_COL_TILE = 2048
GLU_COL_TILE = 1024
FF_TILE = 512
MLP_OUT_TILE = 512
MLP_K_TILE = 2048
ATT_QBLOCK = 256
ATT_HEADS_PER_STEP = 8
SSM_ROW_TILE = 256
SSM_PAIRS_PER_BATCH = 4
SSM_GROUPS_PER_BLOCK = 8
HALO = 16
POOL_HALO = 32
POOL_ROW_TILE = 512


def _params(*semantics):
    return pltpu.CompilerParams(dimension_semantics=semantics, vmem_limit_bytes=VMEM_LIMIT_BYTES)


def _rms_norm(x, gain):
    return x * lax.rsqrt(jnp.mean(x * x, axis=-1, keepdims=True) + RMS_EPS) * gain


def _dot(a, b):
    return jnp.dot(a, b, preferred_element_type=F32)


def _wdot(a, w_ref):
    return _dot(a, w_ref[...].astype(BF16))


def _mlp_kernel(x_hbm, g_ref, w1_hbm, w2_hbm, xr_ref, o_ref, x_buf, h_ref, a_ref, w1_buf, w2_buf,
                semx, sem1, sem2, *, layer, n_rows):
    i, t = pl.program_id(0), pl.program_id(1)
    n_out = pl.num_programs(1) - 1
    tf, tk, tn = w1_buf.shape[2], w2_buf.shape[1], w2_buf.shape[2]
    n_up, n_k = a_ref.shape[1] // tf, a_ref.shape[1] // tk
    assert n_up % 2 == 0 and n_k % 2 == 0

    def w1_copy(f, slot):
        return pltpu.make_async_copy(w1_hbm.at[layer, :, pl.ds(f * tf, tf)], w1_buf.at[slot],
                                     sem1.at[slot])

    def w2_copy(n, kh):
        return pltpu.make_async_copy(w2_hbm.at[layer, pl.ds(kh * tk, tk), pl.ds(n * tn, tn)],
                                     w2_buf.at[kh % 2], sem2.at[kh % 2])

    def x_copy(row):
        tm = x_buf.shape[0]
        return pltpu.make_async_copy(x_hbm.at[pl.ds(row * tm, tm), :], x_buf, semx.at[0])

    @pl.when(t == 0)
    def _():
        @pl.when(i == 0)
        def _():
            x_copy(0).start()
            w1_copy(0, 0).start()

        w2_copy(0, 0).start()
        w2_copy(0, 1).start()
        x_copy(i).wait()
        h_ref[...] = _rms_norm(x_buf[...], g_ref[...]).astype(BF16)

        @pl.when(i + 1 < n_rows)
        def _():
            x_copy(i + 1).start()

        def up_pair(p, _):
            for slot in (0, 1):
                f = 2 * p + slot
                w1_copy(f, slot).wait()

                @pl.when(f + 1 < n_up)
                def _():
                    w1_copy(f + 1, 1 - slot).start()

                a = jnp.maximum(_dot(h_ref[...], w1_buf[slot].astype(BF16)), 0.0)
                a_ref[:, pl.ds(pl.multiple_of(f * tf, tf), tf)] = (a * a).astype(BF16)
            return 0

        lax.fori_loop(0, n_up // 2, up_pair, 0)

    @pl.when(t > 0)
    def _():
        n = t - 1
        for kh in range(n_k):
            w2_copy(n, kh).wait()
            part = _dot(a_ref[:, kh * tk:(kh + 1) * tk], w2_buf[kh % 2].astype(BF16))
            if kh == 0:
                o_ref[...] = xr_ref[...] + part
            else:
                o_ref[...] += part
            if kh + 2 < n_k:
                w2_copy(n, kh + 2).start()
            else:
                @pl.when(n + 1 < n_out)
                def _(kh=kh):
                    w2_copy(n + 1, kh + 2 - n_k).start()

        @pl.when((n + 1 == n_out) & (i + 1 < n_rows))
        def _():
            w1_copy(0, 0).start()


def _mlp(x, gain, w1, w2, layer):
    s, d = x.shape
    dff = w1.shape[2]
    tm, tf, tn, tk = min(ROW_TILE, s), FF_TILE, MLP_OUT_TILE, MLP_K_TILE
    n_rows, n_out = s // tm, d // tn

    def out_tile(t):
        return jnp.maximum(t - 1, 0)

    return pl.pallas_call(
        functools.partial(_mlp_kernel, layer=layer, n_rows=n_rows),
        grid=(n_rows, 1 + n_out),
        in_specs=[
            pl.BlockSpec(memory_space=pl.ANY),
            pl.BlockSpec((1, d), lambda i, t: (0, 0)),
            pl.BlockSpec(memory_space=pl.ANY),
            pl.BlockSpec(memory_space=pl.ANY),
            pl.BlockSpec((tm, tn), lambda i, t: (i, out_tile(t))),
        ],
        out_specs=pl.BlockSpec((tm, tn), lambda i, t: (i, out_tile(t))),
        out_shape=jax.ShapeDtypeStruct((s, d), F32),
        scratch_shapes=[
            pltpu.VMEM((tm, d), F32),
            pltpu.VMEM((tm, d), BF16),
            pltpu.VMEM((tm, dff), BF16),
            pltpu.VMEM((2, d, tf), F32),
            pltpu.VMEM((2, tk, tn), F32),
            pltpu.SemaphoreType.DMA((1,)),
            pltpu.SemaphoreType.DMA((2,)),
            pltpu.SemaphoreType.DMA((2,)),
        ],
        compiler_params=_params("arbitrary", "arbitrary"),
        name="mlp",
    )(x, gain, w1, w2, x)


def _proj_res_kernel(a_ref, w_ref, x_ref, o_ref, wb_ref):
    @pl.when(pl.program_id(1) == 0)
    def _():
        wb_ref[...] = w_ref[...].astype(BF16)

    o_ref[...] = x_ref[...] + _dot(a_ref[...], wb_ref[...])


def _proj_res(a, w, x, layer):
    s, k = a.shape
    n = w.shape[2]
    tm, tn = min(PROJ_ROW_TILE, s), min(PROJ_COL_TILE, n)
    return pl.pallas_call(
        _proj_res_kernel,
        grid=(n // tn, s // tm),
        in_specs=[
            pl.BlockSpec((tm, k), lambda j, i: (i, 0)),
            pl.BlockSpec((None, k, tn), lambda j, i: (layer, 0, j), pipeline_mode=pl.Buffered(1)),
            pl.BlockSpec((tm, tn), lambda j, i: (i, j)),
        ],
        out_specs=pl.BlockSpec((tm, tn), lambda j, i: (i, j)),
        out_shape=jax.ShapeDtypeStruct((s, n), F32),
        scratch_shapes=[pltpu.VMEM((k, tn), BF16)],
        compiler_params=_params("arbitrary", "arbitrary"),
        name="proj_res",
    )(a, w, x)


def _glu_res_kernel(a_ref, wv_ref, wg_ref, x_ref, o_ref, wvb_ref, wgb_ref):
    @pl.when(pl.program_id(1) == 0)
    def _():
        wvb_ref[...] = wv_ref[...].astype(BF16)
        wgb_ref[...] = wg_ref[...].astype(BF16)

    a = a_ref[...]
    val = _dot(a, wvb_ref[...])
    gate = _dot(a, wgb_ref[...])
    o_ref[...] = x_ref[...] + val * jax.nn.sigmoid(gate)


def _glu_res(a, w_glu, x, layer):
    s, k = a.shape
    n = w_glu.shape[2] // 2
    tm, tn = min(PROJ_ROW_TILE, s), min(GLU_COL_TILE, n)
    nj = n // tn
    return pl.pallas_call(
        _glu_res_kernel,
        grid=(nj, s // tm),
        in_specs=[
            pl.BlockSpec((tm, k), lambda j, i: (i, 0)),
            pl.BlockSpec((None, k, tn), lambda j, i: (layer, 0, j), pipeline_mode=pl.Buffered(1)),
            pl.BlockSpec((None, k, tn), lambda j, i: (layer, 0, j + nj), pipeline_mode=pl.Buffered(1)),
            pl.BlockSpec((tm, tn), lambda j, i: (i, j)),
        ],
        out_specs=pl.BlockSpec((tm, tn), lambda j, i: (i, j)),
        out_shape=jax.ShapeDtypeStruct((s, n), F32),
        scratch_shapes=[pltpu.VMEM((k, tn), BF16), pltpu.VMEM((k, tn), BF16)],
        compiler_params=_params("arbitrary", "arbitrary"),
        name="glu_res",
    )(a, w_glu, w_glu, x)


def _conv_front_kernel(x_ref, g_ref, wb_ref, wc_ref, wv_ref, cw_ref, o_ref, h_ref, ext_ref, carry_ref):
    i, j = pl.program_id(0), pl.program_id(1)
    tm = o_ref.shape[0]

    @pl.when(j == 0)
    def _():
        h_ref[...] = _rms_norm(x_ref[...], g_ref[...]).astype(BF16)

    @pl.when(i == 0)
    def _():
        carry_ref[j] = jnp.zeros(carry_ref.shape[1:], F32)

    h = h_ref[...]
    u = _wdot(h, wc_ref) * _wdot(h, wv_ref)
    ext_ref[0:HALO, :] = carry_ref[j]
    ext_ref[HALO:, :] = u
    carry_ref[j] = u[tm - HALO:, :]
    conv = cw_ref[CONV_WIDTH - 1:CONV_WIDTH, :] * u
    for lag in range(1, CONV_WIDTH):
        tap = CONV_WIDTH - 1 - lag
        conv += cw_ref[tap:tap + 1, :] * ext_ref[pl.ds(HALO - lag, tm), :]
    o_ref[...] = (_wdot(h, wb_ref) * conv).astype(BF16)


def _conv_front(x, gain, w_in, conv_w, layer):
    s, d = x.shape
    tm, tn = min(ROW_TILE, s), COL_TILE
    nj = d // tn
    return pl.pallas_call(
        _conv_front_kernel,
        grid=(s // tm, nj),
        in_specs=[
            pl.BlockSpec((tm, d), lambda i, j: (i, 0)),
            pl.BlockSpec((1, d), lambda i, j: (0, 0)),
            pl.BlockSpec((None, d, tn), lambda i, j: (layer, 0, j)),
            pl.BlockSpec((None, d, tn), lambda i, j: (layer, 0, j + nj)),
            pl.BlockSpec((None, d, tn), lambda i, j: (layer, 0, j + 2 * nj)),
            pl.BlockSpec((CONV_WIDTH, tn), lambda i, j: (0, j)),
        ],
        out_specs=pl.BlockSpec((tm, tn), lambda i, j: (i, j)),
        out_shape=jax.ShapeDtypeStruct((s, d), BF16),
        scratch_shapes=[
            pltpu.VMEM((tm, d), BF16),
            pltpu.VMEM((tm + HALO, tn), F32),
            pltpu.VMEM((nj, HALO, tn), F32),
        ],
        compiler_params=_params("arbitrary", "arbitrary"),
        name="conv_front",
    )(x, gain, w_in, w_in, w_in, conv_w)


def _round_kernel(w_ref, o_ref):
    o_ref[...] = w_ref[...].astype(BF16)


def _round_bf16(w, layer, block_rows):
    _, r, c = w.shape
    return pl.pallas_call(
        _round_kernel,
        grid=(r // block_rows,),
        in_specs=[pl.BlockSpec((None, block_rows, c), lambda i: (layer, i, 0))],
        out_specs=pl.BlockSpec((block_rows, c), lambda i: (i, 0)),
        out_shape=jax.ShapeDtypeStruct((r, c), BF16),
        compiler_params=_params("parallel"),
        name="round_bf16",
    )(w)


def _pool_kernel(x_ref, g_ref, win_ref, wg_ref, sc_ref, o_ref, ext_ref, la_ref, lb_ref, carry_ref):
    i = pl.program_id(0)
    tm = o_ref.shape[0]
    halo = POOL_HALO
    n_ext = tm + halo
    pg = wg_ref.shape[1]

    @pl.when(i == 0)
    def _():
        carry_ref[...] = jnp.zeros(carry_ref.shape, F32)

    x = x_ref[...]
    h = _rms_norm(x, g_ref[...]).astype(BF16)
    pos = (i * tm + 1 + lax.broadcasted_iota(jnp.int32, (tm, 1), 0)).astype(F32)

    for gi, w in enumerate(POOL_WINDOWS):
        cols = slice(gi * pg, (gi + 1) * pg)
        u = _dot(h, win_ref[:, cols])
        ext_ref[gi, 0:halo, :] = carry_ref[gi]
        ext_ref[gi, halo:, :] = u
        carry_ref[gi] = u[tm - halo:, :]
        levels = int(math.log2(w))
        assert 2 ** levels == w and SUBLANES * levels <= halo
        src = ext_ref
        for lv in range(1, levels):
            dst = la_ref if lv % 2 else lb_ref
            lo, shift = SUBLANES * lv, 2 ** (lv - 1)
            dst[gi, lo:, :] = src[gi, lo:, :] + src[gi, pl.ds(lo - shift, n_ext - lo), :]
            src = dst
        acc = src[gi, halo:, :] + src[gi, pl.ds(halo - w // 2, tm), :]
        inv_count = 1.0 / jnp.minimum(pos, float(w))
        pooled = (acc * inv_count - u).astype(BF16)
        o_ref[:, cols] = x[:, cols] + _dot(pooled, wg_ref[gi]) * sc_ref[:, cols]


def _pool_mixer(x, gain, w_in, w_group, scale, layer):
    s, d = x.shape
    ng, pg = w_group.shape[1], w_group.shape[2]
    tm = min(POOL_ROW_TILE, s)
    win_b = _round_bf16(w_in, layer, d // ng)
    wg_b = _round_bf16(w_group.reshape(w_group.shape[0], ng * pg, pg), layer, pg).reshape(ng, pg, pg)
    return pl.pallas_call(
        _pool_kernel,
        grid=(s // tm,),
        in_specs=[
            pl.BlockSpec((tm, d), lambda i: (i, 0)),
            pl.BlockSpec((1, d), lambda i: (0, 0)),
            pl.BlockSpec((d, d), lambda i: (0, 0)),
            pl.BlockSpec((ng, pg, pg), lambda i: (0, 0, 0)),
            pl.BlockSpec((1, d), lambda i: (0, 0)),
        ],
        out_specs=pl.BlockSpec((tm, d), lambda i: (i, 0)),
        out_shape=jax.ShapeDtypeStruct((s, d), F32),
        scratch_shapes=[
            pltpu.VMEM((ng, tm + POOL_HALO, pg), F32),
            pltpu.VMEM((ng, tm + POOL_HALO, pg), F32),
            pltpu.VMEM((ng, tm + POOL_HALO, pg), F32),
            pltpu.VMEM((ng, POOL_HALO, pg), F32),
        ],
        compiler_params=_params("arbitrary"),
        name="pool_mixer",
    )(x, gain, win_b, wg_b, scale)


def _qkv_kernel(x_ref, g_ref, w_ref, hg_ref, o_ref, h_ref, *, n_norm_tiles):
    j = pl.program_id(1)

    @pl.when(j == 0)
    def _():
        h_ref[...] = _rms_norm(x_ref[...], g_ref[...]).astype(BF16)

    @pl.when(j < n_norm_tiles)
    def _():
        y = _wdot(h_ref[...], w_ref)
        for hd in range(y.shape[1] // ATT_HEAD_DIM):
            sl = slice(hd * ATT_HEAD_DIM, (hd + 1) * ATT_HEAD_DIM)
            o_ref[:, sl] = _rms_norm(y[:, sl], hg_ref[:, sl]).astype(BF16)

    @pl.when(j >= n_norm_tiles)
    def _():
        o_ref[...] = _wdot(h_ref[...], w_ref).astype(BF16)


def _qkv_proj(x, gain, w_qkv, head_gains, layer):
    s, d = x.shape
    n = w_qkv.shape[2]
    tm, tn = min(ROW_TILE, s), QKV_COL_TILE
    n_norm_tiles = head_gains.shape[1] // tn
    return pl.pallas_call(
        functools.partial(_qkv_kernel, n_norm_tiles=n_norm_tiles),
        grid=(s // tm, n // tn),
        in_specs=[
            pl.BlockSpec((tm, d), lambda i, j: (i, 0)),
            pl.BlockSpec((1, d), lambda i, j: (0, 0)),
            pl.BlockSpec((None, d, tn), lambda i, j: (layer, 0, j)),
            pl.BlockSpec((1, tn), lambda i, j: (0, jnp.minimum(j, n_norm_tiles - 1))),
        ],
        out_specs=pl.BlockSpec((tm, tn), lambda i, j: (i, j)),
        out_shape=jax.ShapeDtypeStruct((s, n), BF16),
        scratch_shapes=[pltpu.VMEM((tm, d), BF16)],
        compiler_params=_params("parallel", "arbitrary"),
        name="qkv_proj",
    )(x, gain, w_qkv, head_gains)


def _attn_kernel(q_ref, k0_ref, k1_ref, k2_ref, v0_ref, v1_ref, v2_ref, prof_ref, o_ref, bias_ref):
    b = pl.program_id(1)
    qb = q_ref.shape[0]
    n_local = q_ref.shape[1] // ATT_HEAD_DIM
    log2e = math.log2(math.e)
    scale = ATT_HEAD_DIM ** -0.5 * log2e

    @pl.when(b == 0)
    def _():
        q_idx = lax.broadcasted_iota(jnp.int32, (qb, 3 * qb), 0) + 2 * qb
        k_idx = lax.broadcasted_iota(jnp.int32, (qb, 3 * qb), 1)
        chunk_start = (q_idx // CHUNK) * CHUNK
        in_band = (k_idx >= chunk_start - ATT_LEFT_CHUNKS * CHUNK) & (k_idx < chunk_start + CHUNK)
        for hh in range(n_local):
            profile = jnp.broadcast_to(prof_ref[hh], (qb, prof_ref.shape[2]))
            table = pltpu.roll(profile, 0, 1, stride=1, stride_axis=0)[:, :3 * qb]
            bias_ref[hh] = jnp.where(in_band, table * log2e, MASK_VALUE)

    def lane_slabs(a):
        return [a[:, k * LANES:(k + 1) * LANES] for k in range(a.shape[1] // LANES)]

    def heads(first_blocks):
        for hh in range(n_local):
            cols = slice(hh * ATT_HEAD_DIM, (hh + 1) * ATT_HEAD_DIM)
            q = q_ref[:, cols]
            scores = []
            for c, k_ref in enumerate((k0_ref, k1_ref, k2_ref)):
                s_c = lax.dot_general(q, k_ref[:, cols], (((1,), (1,)), ((), ())),
                                      preferred_element_type=F32)
                s_c = s_c * scale + bias_ref[hh, :, c * qb:(c + 1) * qb]
                if first_blocks and c < 2:
                    s_c = jnp.where(b - 2 + c >= 0, s_c, MASK_VALUE)
                scores.append(s_c)
            m = jnp.max(functools.reduce(jnp.maximum,
                                         [sl for s_c in scores for sl in lane_slabs(s_c)]),
                        axis=-1, keepdims=True)
            acc = jnp.zeros((qb, 2 * ATT_HEAD_DIM), F32)
            ones = jnp.ones((qb, ATT_HEAD_DIM), BF16)
            for s_c, v_ref in zip(scores, (v0_ref, v1_ref, v2_ref)):
                p = jnp.exp2((s_c - m).astype(BF16))
                acc += _dot(p, jnp.concatenate([v_ref[:, cols], ones], axis=1))
            o_ref[:, cols] = (acc[:, :ATT_HEAD_DIM] / acc[:, ATT_HEAD_DIM:]).astype(BF16)

    @pl.when(b < 2)
    def _():
        heads(True)

    @pl.when(b >= 2)
    def _():
        heads(False)


def _attention(qkv, profile, n_heads):
    s = qkv.shape[0]
    qb = ATT_QBLOCK
    nb = s // qb
    hps = ATT_HEADS_PER_STEP
    width = hps * ATT_HEAD_DIM
    ngroups = n_heads // hps

    def kv_spec(offset, back):
        return pl.BlockSpec((qb, width), lambda h, b: (jnp.maximum(b - back, 0), offset + h))

    return pl.pallas_call(
        _attn_kernel,
        grid=(ngroups, nb),
        in_specs=[
            pl.BlockSpec((qb, width), lambda h, b: (b, h)),
            kv_spec(ngroups, 2), kv_spec(ngroups, 1), kv_spec(ngroups, 0),
            kv_spec(2 * ngroups, 2), kv_spec(2 * ngroups, 1), kv_spec(2 * ngroups, 0),
            pl.BlockSpec((hps, 1, 4 * qb), lambda h, b: (h, 0, 0)),
        ],
        out_specs=pl.BlockSpec((qb, width), lambda h, b: (b, h)),
        out_shape=jax.ShapeDtypeStruct((s, n_heads * ATT_HEAD_DIM), BF16),
        scratch_shapes=[pltpu.VMEM((hps, qb, 3 * qb), F32)],
        compiler_params=_params("arbitrary", "arbitrary"),
        name="chunk_attention",
    )(qkv, qkv, qkv, qkv, qkv, qkv, qkv, profile)


def _attention_profile(rel_bias):
    qb = ATT_QBLOCK
    n_heads = rel_bias.shape[0]
    assert REL_CLIP <= 2 * qb and qb <= REL_CLIP + 1 and 2 * REL_CLIP + 1 == rel_bias.shape[1]
    n_flat = 2 * qb - REL_CLIP + 1
    top = rel_bias[:, 2 * REL_CLIP:]
    profile = jnp.concatenate([
        jnp.broadcast_to(top, (n_heads, n_flat)),
        jnp.flip(rel_bias[:, REL_CLIP - qb + 1:2 * REL_CLIP], axis=1),
        jnp.broadcast_to(top, (n_heads, qb)),
    ], axis=1)
    return profile.reshape(n_heads, 1, 4 * qb).astype(F32)


def _ssm_prep_kernel(are_ref, aim_ref, ldt_ref, btr_ref, bti_ref, abr_ref, abi_ref, bbr_ref, bbi_ref):
    lam_r, lam_i = are_ref[...], aim_ref[...]
    dt = jnp.exp(ldt_ref[...])
    mag = jnp.exp(lam_r * dt)
    ab_r, ab_i = mag * jnp.cos(lam_i * dt), mag * jnp.sin(lam_i * dt)
    abr_ref[...] = ab_r
    abi_ref[...] = ab_i
    num_r, num_i = ab_r - 1.0, ab_i
    den = lam_r * lam_r + lam_i * lam_i
    co_r = (num_r * lam_r + num_i * lam_i) / den
    co_i = (num_i * lam_r - num_r * lam_i) / den
    for c in range(btr_ref.shape[0]):
        b_r, b_i = btr_ref[c], bti_ref[c]
        bbr_ref[c] = co_r * b_r - co_i * b_i
        bbi_ref[c] = co_r * b_i + co_i * b_r


def _ssm_prep(a_re, a_im, log_dt, bt_re, bt_im):
    g, n = a_re.shape
    gn = jax.ShapeDtypeStruct((g, n), F32)
    cgn = jax.ShapeDtypeStruct(bt_re.shape, F32)
    return pl.pallas_call(
        _ssm_prep_kernel,
        out_shape=(gn, gn, cgn, cgn),
        name="ssm_prep",
    )(a_re, a_im, log_dt, bt_re, bt_im)


def _ssm_kernel(x_ref, g_ref, d_ref, wbr_ref, wbi_ref, wcr_ref, wci_ref, ar_ref, ai_ref, o_ref,
                hn_ref, y_ref, xr_ref, xi_ref, cr_ref, ci_ref):
    i = pl.program_id(0)
    t = x_ref.shape[0]
    nblk, cb, ns = wbr_ref.shape
    n_batches, ppb = xr_ref.shape[0], xr_ref.shape[1]
    nsub = ns // LANES
    assert cb == LANES and 2 * nsub == SUBLANES and n_batches * ppb * 2 == nblk

    h = _rms_norm(x_ref[...], g_ref[...])
    for c in range(nblk):
        hn_ref[c] = h[:, c * LANES:(c + 1) * LANES]

    @pl.when(i == 0)
    def _():
        cr_ref[...] = jnp.zeros(cr_ref.shape, F32)
        ci_ref[...] = jnp.zeros(ci_ref.shape, F32)

    def slab_rows(half, k):
        return pl.ds(half * nsub + k, t, stride=SUBLANES)

    def blocks_of(batch):
        for g8 in range(2 * ppb):
            yield batch * 2 * ppb + g8, g8 // 2, g8 % 2

    def run(*stages):
        stages = list(stages)
        while stages:
            for stage in list(stages):
                if next(stage, "done") == "done":
                    stages.remove(stage)

    def project_in(batch):
        for gb, pair, half in blocks_of(batch):
            hb = hn_ref[gb].astype(BF16)
            for w_ref, dst_ref in ((wbr_ref, xr_ref), (wbi_ref, xi_ref)):
                bu = _dot(hb, w_ref[gb])
                for k in range(nsub):
                    dst_ref[batch, pair, slab_rows(half, k), :] = bu[:, k * LANES:(k + 1) * LANES]
            yield

    def scan(batch):
        coef = [(ar_ref[batch * ppb + p], ai_ref[batch * ppb + p]) for p in range(ppb)]
        steps_per_turn = t // (2 * ppb)

        def step(tt, carry):
            rows = pl.ds(tt * SUBLANES, SUBLANES)
            out = []
            for p in range(ppb):
                (a_r, a_i), x_r, x_i = coef[p], carry[2 * p], carry[2 * p + 1]
                n_r = a_r * x_r - a_i * x_i + xr_ref[batch, p, rows, :]
                n_i = a_r * x_i + a_i * x_r + xi_ref[batch, p, rows, :]
                xr_ref[batch, p, rows, :] = n_r
                xi_ref[batch, p, rows, :] = n_i
                out += [n_r, n_i]
            return tuple(out)

        last = tuple(ref[batch * ppb + p] for p in range(ppb) for ref in (cr_ref, ci_ref))
        for tt in range(t):
            last = step(tt, last)
            if (tt + 1) % steps_per_turn == 0 and tt + 1 < t:
                yield
        for p in range(ppb):
            cr_ref[batch * ppb + p] = last[2 * p]
            ci_ref[batch * ppb + p] = last[2 * p + 1]
        yield

    def project_out(batch):
        for gb, pair, half in blocks_of(batch):
            s_r, s_i = (jnp.concatenate([ref[batch, pair, slab_rows(half, k), :] for k in range(nsub)],
                                        axis=1).astype(BF16) for ref in (xr_ref, xi_ref))
            y_ref[gb] = _dot(s_r, wcr_ref[gb]) + _dot(s_i, wci_ref[gb])
            yield

    def finish(batch):
        for gb, _, _ in blocks_of(batch):
            lanes = slice(gb * LANES, (gb + 1) * LANES)
            o_ref[:, lanes] = jax.nn.gelu(y_ref[gb] + d_ref[:, lanes] * hn_ref[gb]).astype(BF16)
            yield

    run(project_in(0))
    for batch in range(n_batches):
        beside = [project_in(batch + 1)] if batch + 1 < n_batches else []
        beside += [project_out(batch - 1)] if batch > 0 else []
        beside += [finish(batch - 2)] if batch > 1 else []
        run(scan(batch), *beside)
    run(project_out(n_batches - 1), *([finish(n_batches - 2)] if n_batches > 1 else []))
    run(finish(n_batches - 1))


def _ssm_core(x, gain, d_skip, wb_r, wb_i, wc_r, wc_i, ab_r, ab_i):
    s, d = x.shape
    t = min(SSM_ROW_TILE, s)
    nblk, cb, ns = wb_r.shape
    ppb = SSM_PAIRS_PER_BATCH
    n_batches = nblk // (2 * ppb)

    def whole(a):
        return pl.BlockSpec(a.shape, lambda i: (0,) * a.ndim)

    return pl.pallas_call(
        _ssm_kernel,
        grid=(s // t,),
        in_specs=[
            pl.BlockSpec((t, d), lambda i: (i, 0)),
            whole(gain), whole(d_skip),
            whole(wb_r), whole(wb_i), whole(wc_r), whole(wc_i),
            whole(ab_r), whole(ab_i),
        ],
        out_specs=pl.BlockSpec((t, d), lambda i: (i, 0)),
        out_shape=jax.ShapeDtypeStruct((s, d), BF16),
        scratch_shapes=[
            pltpu.VMEM((nblk, t, LANES), F32),
            pltpu.VMEM((nblk, t, LANES), F32),
            pltpu.VMEM((n_batches, ppb, t * SUBLANES, LANES), F32),
            pltpu.VMEM((n_batches, ppb, t * SUBLANES, LANES), F32),
            pltpu.VMEM((nblk // 2, SUBLANES, LANES), F32),
            pltpu.VMEM((nblk // 2, SUBLANES, LANES), F32),
        ],
        compiler_params=_params("arbitrary"),
        name="ssm_core",
    )(x, gain, d_skip, wb_r, wb_i, wc_r, wc_i, ab_r, ab_i)


def _block_diag(w, groups_per_block):
    g, r, c = w.shape
    nblk = g // groups_per_block
    tiled = jnp.tile(w.reshape(nblk, groups_per_block * r, c), (1, 1, groups_per_block))
    row_group = lax.broadcasted_iota(jnp.int32, tiled.shape, 1) // r
    col_group = lax.broadcasted_iota(jnp.int32, tiled.shape, 2) // c
    return jnp.where(row_group == col_group, tiled, jnp.zeros_like(tiled))


def _s5_mixer(x, gain, a_re, a_im, log_dt, b_re, b_im, c_re, c_im, d_skip, w_glu, layer):
    s, d = x.shape
    g, n = a_re.shape
    gpb = SSM_GROUPS_PER_BLOCK
    bt_re, bt_im = b_re.transpose(2, 0, 1), b_im.transpose(2, 0, 1)
    ab_r, ab_i, bb_r, bb_i = _ssm_prep(a_re, a_im, log_dt.reshape(g, 1), bt_re, bt_im)
    wb_r = _block_diag(bb_r.transpose(1, 0, 2), gpb).astype(BF16)
    wb_i = _block_diag(bb_i.transpose(1, 0, 2), gpb).astype(BF16)
    wc_r = _block_diag(c_re.transpose(0, 2, 1), gpb).astype(BF16)
    wc_i = _block_diag(-c_im.transpose(0, 2, 1), gpb).astype(BF16)
    slabs = lambda a: a.reshape(g // (2 * gpb), 2 * gpb * n // LANES, LANES)
    z = _ssm_core(x, gain, d_skip.reshape(1, d), wb_r, wb_i, wc_r, wc_i, slabs(ab_r), slabs(ab_i))
    return _glu_res(z, w_glu, x, layer)


def kernel(x, norm_mix, norm_mlp, mlp_w1, mlp_w2, conv_w_in, conv_w, conv_w_out, pool_w_in, pool_w_group, pool_scale, att_w_qkv, att_q_norm, att_k_norm, att_rel_bias, att_w_out, ssm_a_re, ssm_a_im, ssm_log_dt, ssm_b_re, ssm_b_im, ssm_c_re, ssm_c_im, ssm_d, ssm_w_glu):
    b, s, d = x.shape
    depth = norm_mix.shape[0]
    n_mixers = 4
    outs = []
    for bi in range(b):
        xs = x[bi]
        for i in range(depth):
            kind, j = i % n_mixers, i // n_mixers
            gain = norm_mix[i].reshape(1, d)
            if kind == 0:
                gated = _conv_front(xs, gain, conv_w_in, conv_w[j], j)
                xs = _proj_res(gated, conv_w_out, xs, j)
            elif kind == 1:
                xs = _pool_mixer(xs, gain, pool_w_in, pool_w_group, pool_scale[j].reshape(1, d), j)
            elif kind == 2:
                n_heads = d // ATT_HEAD_DIM
                head_gains = jnp.concatenate([jnp.tile(att_q_norm[j], n_heads),
                                              jnp.tile(att_k_norm[j], n_heads)]).reshape(1, 2 * d)
                qkv = _qkv_proj(xs, gain, att_w_qkv, head_gains, j)
                att = _attention(qkv, _attention_profile(att_rel_bias[j]), n_heads)
                xs = _proj_res(att, att_w_out, xs, j)
            else:
                xs = _s5_mixer(xs, gain, ssm_a_re[j], ssm_a_im[j], ssm_log_dt[j], ssm_b_re[j],
                               ssm_b_im[j], ssm_c_re[j], ssm_c_im[j], ssm_d[j], ssm_w_glu, j)
            xs = _mlp(xs, norm_mlp[i].reshape(1, d), mlp_w1, mlp_w2, i)
        outs.append(xs)
    return outs[0][None] if b == 1 else jnp.stack(outs)
```

```python
import functools
import math

import jax
import jax.numpy as jnp
from jax import lax
from jax.experimental import pallas as pl
from jax.experimental.pallas import tpu as pltpu

F32 = jnp.float32
BF16 = jnp.bfloat16

RMS_EPS = 1e-6
CHUNK = 64
ATT_HEAD_DIM = 128
ATT_LEFT_CHUNKS = 8
REL_CLIP = 256
MASK_VALUE = -1e30
POOL_WINDOWS = (2, 4, 8, 16)
CONV_WIDTH = 3
SSM_GROUP = 16
SSM_STATE = 64

SUBLANES = 8
LANES = 128
VMEM_LIMIT_BYTES = 56 * 1024 * 1024

ROW_TILE = 1024
COL_TILE = 512
QKV_COL_TILE = 1024
PROJ_ROW_TILE = 512
PROJ_COL_TILE = 2048
GLU_COL_TILE = 1024
FF_TILE = 512
MLP_OUT_TILE = 512
MLP_K_TILE = 2048
ATT_QBLOCK = 256
ATT_HEADS_PER_STEP = 8
SSM_ROW_TILE = 256
SSM_PAIRS_PER_BATCH = 4
SSM_GROUPS_PER_BLOCK = 8
HALO = 16
POOL_HALO = 32
POOL_ROW_TILE = 512


def _params(*semantics):
    return pltpu.CompilerParams(dimension_semantics=semantics, vmem_limit_bytes=VMEM_LIMIT_BYTES)


def _rms_norm(x, gain):
    return x * lax.rsqrt(jnp.mean(x * x, axis=-1, keepdims=True) + RMS_EPS) * gain


def _dot(a, b):
    return jnp.dot(a, b, preferred_element_type=F32)


def _wdot(a, w_ref):
    return _dot(a, w_ref[...].astype(BF16))


def _mlp_kernel(x_hbm, g_ref, w1_hbm, w2_hbm, xr_ref, o_ref, x_buf, h_ref, a_ref, w1_buf, w2_buf,
                semx, sem1, sem2, *, layer, n_rows):
    i, t = pl.program_id(0), pl.program_id(1)
    n_out = pl.num_programs(1) - 1
    tf, tk, tn = w1_buf.shape[2], w2_buf.shape[1], w2_buf.shape[2]
    n_up, n_k = a_ref.shape[1] // tf, a_ref.shape[1] // tk
    assert n_up % 2 == 0 and n_k % 2 == 0

    def w1_copy(f, slot):
        return pltpu.make_async_copy(w1_hbm.at[layer, :, pl.ds(f * tf, tf)], w1_buf.at[slot],
                                     sem1.at[slot])

    def w2_copy(n, kh):
        return pltpu.make_async_copy(w2_hbm.at[layer, pl.ds(kh * tk, tk), pl.ds(n * tn, tn)],
                                     w2_buf.at[kh % 2], sem2.at[kh % 2])

    def x_copy(row):
        tm = x_buf.shape[0]
        return pltpu.make_async_copy(x_hbm.at[pl.ds(row * tm, tm), :], x_buf, semx.at[0])

    @pl.when(t == 0)
    def _():
        @pl.when(i == 0)
        def _():
            x_copy(0).start()
            w1_copy(0, 0).start()

        w2_copy(0, 0).start()
        w2_copy(0, 1).start()

        def up_tile(f, slot, h):
            a = jnp.maximum(_dot(h, w1_buf[slot].astype(BF16)), 0.0)
            cols = pl.ds(f * tf, tf) if isinstance(f, int) else pl.ds(pl.multiple_of(f * tf, tf), tf)
            a_ref[:, cols] = (a * a).astype(BF16)

        x_copy(i).wait()
        w1_copy(0, 0).wait()
        w1_copy(1, 1).start()
        h = _rms_norm(x_buf[...], g_ref[...]).astype(BF16)
        h_ref[...] = h
        up_tile(0, 0, h)

        @pl.when(i + 1 < n_rows)
        def _():
            x_copy(i + 1).start()

        def up_pair(p, _):
            for f, slot in ((2 * p + 1, 1), (2 * p + 2, 0)):
                w1_copy(f, slot).wait()
                w1_copy(f + 1, 1 - slot).start()
                up_tile(f, slot, h_ref[...])
            return 0

        lax.fori_loop(0, (n_up - 2) // 2, up_pair, 0)
        w1_copy(n_up - 1, 1).wait()
        up_tile(n_up - 1, 1, h_ref[...])

    @pl.when(t > 0)
    def _():
        n = t - 1
        for kh in range(n_k):
            w2_copy(n, kh).wait()
            part = _dot(a_ref[:, kh * tk:(kh + 1) * tk], w2_buf[kh % 2].astype(BF16))
            if kh == 0:
                o_ref[...] = xr_ref[...] + part
            else:
                o_ref[...] += part
            if kh + 2 < n_k:
                w2_copy(n, kh + 2).start()
            else:
                @pl.when(n + 1 < n_out)
                def _(kh=kh):
                    w2_copy(n + 1, kh + 2 - n_k).start()

        @pl.when((n + 1 == n_out) & (i + 1 < n_rows))
        def _():
            w1_copy(0, 0).start()


def _mlp(x, gain, w1, w2, layer):
    s, d = x.shape
    dff = w1.shape[2]
    tm, tf, tn, tk = min(ROW_TILE, s), FF_TILE, MLP_OUT_TILE, MLP_K_TILE
    n_rows, n_out = s // tm, d // tn

    def out_tile(t):
        return jnp.maximum(t - 1, 0)

    return pl.pallas_call(
        functools.partial(_mlp_kernel, layer=layer, n_rows=n_rows),
        grid=(n_rows, 1 + n_out),
        in_specs=[
            pl.BlockSpec(memory_space=pl.ANY),
            pl.BlockSpec((1, d), lambda i, t: (0, 0)),
            pl.BlockSpec(memory_space=pl.ANY),
            pl.BlockSpec(memory_space=pl.ANY),
            pl.BlockSpec((tm, tn), lambda i, t: (i, out_tile(t))),
        ],
        out_specs=pl.BlockSpec((tm, tn), lambda i, t: (i, out_tile(t))),
        out_shape=jax.ShapeDtypeStruct((s, d), F32),
        scratch_shapes=[
            pltpu.VMEM((tm, d), F32),
            pltpu.VMEM((tm, d), BF16),
            pltpu.VMEM((tm, dff), BF16),
            pltpu.VMEM((2, d, tf), F32),
            pltpu.VMEM((2, tk, tn), F32),
            pltpu.SemaphoreType.DMA((1,)),
            pltpu.SemaphoreType.DMA((2,)),
            pltpu.SemaphoreType.DMA((2,)),
        ],
        compiler_params=_params("arbitrary", "arbitrary"),
        name="mlp",
    )(x, gain, w1, w2, x)


def _proj_res_kernel(a_ref, w_ref, x_ref, o_ref, wb_ref):
    @pl.when(pl.program_id(1) == 0)
    def _():
        wb_ref[...] = w_ref[...].astype(BF16)

    o_ref[...] = x_ref[...] + _dot(a_ref[...], wb_ref[...])


def _proj_res(a, w, x, layer):
    s, k = a.shape
    n = w.shape[2]
    tm, tn = min(PROJ_ROW_TILE, s), min(PROJ_COL_TILE, n)
    return pl.pallas_call(
        _proj_res_kernel,
        grid=(n // tn, s // tm),
        in_specs=[
            pl.BlockSpec((tm, k), lambda j, i: (i, 0)),
            pl.BlockSpec((None, k, tn), lambda j, i: (layer, 0, j), pipeline_mode=pl.Buffered(1)),
            pl.BlockSpec((tm, tn), lambda j, i: (i, j)),
        ],
        out_specs=pl.BlockSpec((tm, tn), lambda j, i: (i, j)),
        out_shape=jax.ShapeDtypeStruct((s, n), F32),
        scratch_shapes=[pltpu.VMEM((k, tn), BF16)],
        compiler_params=_params("arbitrary", "arbitrary"),
        name="proj_res",
    )(a, w, x)


def _glu_res_kernel(a_ref, wv_ref, wg_ref, x_ref, o_ref, wvb_ref, wgb_ref):
    @pl.when(pl.program_id(1) == 0)
    def _():
        wvb_ref[...] = wv_ref[...].astype(BF16)
        wgb_ref[...] = wg_ref[...].astype(BF16)

    a = a_ref[...]
    val = _dot(a, wvb_ref[...])
    gate = _dot(a, wgb_ref[...])
    o_ref[...] = x_ref[...] + val * jax.nn.sigmoid(gate)


def _glu_res(a, w_glu, x, layer):
    s, k = a.shape
    n = w_glu.shape[2] // 2
    tm, tn = min(PROJ_ROW_TILE, s), min(GLU_COL_TILE, n)
    nj = n // tn
    return pl.pallas_call(
        _glu_res_kernel,
        grid=(nj, s // tm),
        in_specs=[
            pl.BlockSpec((tm, k), lambda j, i: (i, 0)),
            pl.BlockSpec((None, k, tn), lambda j, i: (layer, 0, j), pipeline_mode=pl.Buffered(1)),
            pl.BlockSpec((None, k, tn), lambda j, i: (layer, 0, j + nj), pipeline_mode=pl.Buffered(1)),
            pl.BlockSpec((tm, tn), lambda j, i: (i, j)),
        ],
        out_specs=pl.BlockSpec((tm, tn), lambda j, i: (i, j)),
        out_shape=jax.ShapeDtypeStruct((s, n), F32),
        scratch_shapes=[pltpu.VMEM((k, tn), BF16), pltpu.VMEM((k, tn), BF16)],
        compiler_params=_params("arbitrary", "arbitrary"),
        name="glu_res",
    )(a, w_glu, w_glu, x)


def _conv_front_kernel(x_ref, g_ref, wb_ref, wc_ref, wv_ref, cw_ref, o_ref, h_ref, ext_ref, carry_ref):
    i, j = pl.program_id(0), pl.program_id(1)
    tm = o_ref.shape[0]

    @pl.when(i == 0)
    def _():
        carry_ref[j] = jnp.zeros(carry_ref.shape[1:], F32)

    def column_tile(h):
        u = _wdot(h, wc_ref) * _wdot(h, wv_ref)
        ext_ref[0:HALO, :] = carry_ref[j]
        ext_ref[HALO:, :] = u
        carry_ref[j] = u[tm - HALO:, :]
        conv = cw_ref[CONV_WIDTH - 1:CONV_WIDTH, :] * u
        for lag in range(1, CONV_WIDTH):
            tap = CONV_WIDTH - 1 - lag
            conv += cw_ref[tap:tap + 1, :] * ext_ref[pl.ds(HALO - lag, tm), :]
        o_ref[...] = (_wdot(h, wb_ref) * conv).astype(BF16)

    @pl.when(j == 0)
    def _():
        h = _rms_norm(x_ref[...], g_ref[...]).astype(BF16)
        h_ref[...] = h
        column_tile(h)

    @pl.when(j > 0)
    def _():
        column_tile(h_ref[...])


def _conv_front(x, gain, w_in, conv_w, layer):
    s, d = x.shape
    tm, tn = min(ROW_TILE, s), COL_TILE
    nj = d // tn
    return pl.pallas_call(
        _conv_front_kernel,
        grid=(s // tm, nj),
        in_specs=[
            pl.BlockSpec((tm, d), lambda i, j: (i, 0)),
            pl.BlockSpec((1, d), lambda i, j: (0, 0)),
            pl.BlockSpec((None, d, tn), lambda i, j: (layer, 0, j)),
            pl.BlockSpec((None, d, tn), lambda i, j: (layer, 0, j + nj)),
            pl.BlockSpec((None, d, tn), lambda i, j: (layer, 0, j + 2 * nj)),
            pl.BlockSpec((CONV_WIDTH, tn), lambda i, j: (0, j)),
        ],
        out_specs=pl.BlockSpec((tm, tn), lambda i, j: (i, j)),
        out_shape=jax.ShapeDtypeStruct((s, d), BF16),
        scratch_shapes=[
            pltpu.VMEM((tm, d), BF16),
            pltpu.VMEM((tm + HALO, tn), F32),
            pltpu.VMEM((nj, HALO, tn), F32),
        ],
        compiler_params=_params("arbitrary", "arbitrary"),
        name="conv_front",
    )(x, gain, w_in, w_in, w_in, conv_w)


def _round_kernel(w_ref, o_ref):
    o_ref[...] = w_ref[...].astype(BF16)


def _round_bf16(w, layer, block_rows):
    _, r, c = w.shape
    return pl.pallas_call(
        _round_kernel,
        grid=(r // block_rows,),
        in_specs=[pl.BlockSpec((None, block_rows, c), lambda i: (layer, i, 0))],
        out_specs=pl.BlockSpec((block_rows, c), lambda i: (i, 0)),
        out_shape=jax.ShapeDtypeStruct((r, c), BF16),
        compiler_params=_params("parallel"),
        name="round_bf16",
    )(w)


def _pool_kernel(x_ref, g_ref, win_ref, wg_ref, sc_ref, o_ref, ext_ref, la_ref, lb_ref, carry_ref):
    i = pl.program_id(0)
    tm = o_ref.shape[0]
    halo = POOL_HALO
    n_ext = tm + halo
    pg = wg_ref.shape[1]

    @pl.when(i == 0)
    def _():
        carry_ref[...] = jnp.zeros(carry_ref.shape, F32)

    x = x_ref[...]
    h = _rms_norm(x, g_ref[...]).astype(BF16)
    pos = (i * tm + 1 + lax.broadcasted_iota(jnp.int32, (tm, 1), 0)).astype(F32)

    for gi, w in enumerate(POOL_WINDOWS):
        cols = slice(gi * pg, (gi + 1) * pg)
        u = _dot(h, win_ref[:, cols])
        ext_ref[gi, 0:halo, :] = carry_ref[gi]
        ext_ref[gi, halo:, :] = u
        carry_ref[gi] = u[tm - halo:, :]
        levels = int(math.log2(w))
        assert 2 ** levels == w and SUBLANES * levels <= halo
        src = ext_ref
        for lv in range(1, levels):
            dst = la_ref if lv % 2 else lb_ref
            lo, shift = SUBLANES * lv, 2 ** (lv - 1)
            dst[gi, lo:, :] = src[gi, lo:, :] + src[gi, pl.ds(lo - shift, n_ext - lo), :]
            src = dst
        acc = src[gi, halo:, :] + src[gi, pl.ds(halo - w // 2, tm), :]
        inv_count = 1.0 / jnp.minimum(pos, float(w))
        pooled = (acc * inv_count - u).astype(BF16)
        o_ref[:, cols] = x[:, cols] + _dot(pooled, wg_ref[gi]) * sc_ref[:, cols]


def _pool_mixer(x, gain, w_in, w_group, scale, layer):
    s, d = x.shape
    ng, pg = w_group.shape[1], w_group.shape[2]
    tm = min(POOL_ROW_TILE, s)
    win_b = _round_bf16(w_in, layer, d // ng)
    wg_b = _round_bf16(w_group.reshape(w_group.shape[0], ng * pg, pg), layer, pg).reshape(ng, pg, pg)
    return pl.pallas_call(
        _pool_kernel,
        grid=(s // tm,),
        in_specs=[
            pl.BlockSpec((tm, d), lambda i: (i, 0)),
            pl.BlockSpec((1, d), lambda i: (0, 0)),
            pl.BlockSpec((d, d), lambda i: (0, 0)),
            pl.BlockSpec((ng, pg, pg), lambda i: (0, 0, 0)),
            pl.BlockSpec((1, d), lambda i: (0, 0)),
        ],
        out_specs=pl.BlockSpec((tm, d), lambda i: (i, 0)),
        out_shape=jax.ShapeDtypeStruct((s, d), F32),
        scratch_shapes=[
            pltpu.VMEM((ng, tm + POOL_HALO, pg), F32),
            pltpu.VMEM((ng, tm + POOL_HALO, pg), F32),
            pltpu.VMEM((ng, tm + POOL_HALO, pg), F32),
            pltpu.VMEM((ng, POOL_HALO, pg), F32),
        ],
        compiler_params=_params("arbitrary"),
        name="pool_mixer",
    )(x, gain, win_b, wg_b, scale)


def _qkv_kernel(x_ref, g_ref, w_ref, hg_ref, o_ref, h_ref, *, n_norm_tiles):
    j = pl.program_id(1)

    def normed_heads(h):
        y = _wdot(h, w_ref)
        for hd in range(y.shape[1] // ATT_HEAD_DIM):
            sl = slice(hd * ATT_HEAD_DIM, (hd + 1) * ATT_HEAD_DIM)
            o_ref[:, sl] = _rms_norm(y[:, sl], hg_ref[:, sl]).astype(BF16)

    @pl.when(j == 0)
    def _():
        h = _rms_norm(x_ref[...], g_ref[...]).astype(BF16)
        h_ref[...] = h
        normed_heads(h)

    @pl.when((j > 0) & (j < n_norm_tiles))
    def _():
        normed_heads(h_ref[...])

    @pl.when(j >= n_norm_tiles)
    def _():
        o_ref[...] = _wdot(h_ref[...], w_ref).astype(BF16)


def _qkv_proj(x, gain, w_qkv, head_gains, layer):
    s, d = x.shape
    n = w_qkv.shape[2]
    tm, tn = min(ROW_TILE, s), QKV_COL_TILE
    n_norm_tiles = head_gains.shape[1] // tn
    return pl.pallas_call(
        functools.partial(_qkv_kernel, n_norm_tiles=n_norm_tiles),
        grid=(s // tm, n // tn),
        in_specs=[
            pl.BlockSpec((tm, d), lambda i, j: (i, 0)),
            pl.BlockSpec((1, d), lambda i, j: (0, 0)),
            pl.BlockSpec((None, d, tn), lambda i, j: (layer, 0, j)),
            pl.BlockSpec((1, tn), lambda i, j: (0, jnp.minimum(j, n_norm_tiles - 1))),
        ],
        out_specs=pl.BlockSpec((tm, tn), lambda i, j: (i, j)),
        out_shape=jax.ShapeDtypeStruct((s, n), BF16),
        scratch_shapes=[pltpu.VMEM((tm, d), BF16)],
        compiler_params=_params("parallel", "arbitrary"),
        name="qkv_proj",
    )(x, gain, w_qkv, head_gains)


def _attn_kernel(q_ref, k0_ref, k1_ref, k2_ref, v0_ref, v1_ref, v2_ref, prof_ref, o_ref, bias_ref):
    b = pl.program_id(1)
    qb = q_ref.shape[0]
    n_local = q_ref.shape[1] // ATT_HEAD_DIM
    log2e = math.log2(math.e)
    scale = ATT_HEAD_DIM ** -0.5 * log2e

    @pl.when(b == 0)
    def _():
        q_idx = lax.broadcasted_iota(jnp.int32, (qb, 3 * qb), 0) + 2 * qb
        k_idx = lax.broadcasted_iota(jnp.int32, (qb, 3 * qb), 1)
        chunk_start = (q_idx // CHUNK) * CHUNK
        in_band = (k_idx >= chunk_start - ATT_LEFT_CHUNKS * CHUNK) & (k_idx < chunk_start + CHUNK)
        for hh in range(n_local):
            profile = jnp.broadcast_to(prof_ref[hh], (qb, prof_ref.shape[2]))
            table = pltpu.roll(profile, 0, 1, stride=1, stride_axis=0)[:, :3 * qb]
            bias_ref[hh] = jnp.where(in_band, table * log2e, MASK_VALUE)

    def lane_slabs(a):
        return [a[:, k * LANES:(k + 1) * LANES] for k in range(a.shape[1] // LANES)]

    def heads(first_blocks):
        for hh in range(n_local):
            cols = slice(hh * ATT_HEAD_DIM, (hh + 1) * ATT_HEAD_DIM)
            q = q_ref[:, cols]
            scores = []
            for c, k_ref in enumerate((k0_ref, k1_ref, k2_ref)):
                s_c = lax.dot_general(q, k_ref[:, cols], (((1,), (1,)), ((), ())),
                                      preferred_element_type=F32)
                s_c = s_c * scale + bias_ref[hh, :, c * qb:(c + 1) * qb]
                if first_blocks and c < 2:
                    s_c = jnp.where(b - 2 + c >= 0, s_c, MASK_VALUE)
                scores.append(s_c)
            m = jnp.max(functools.reduce(jnp.maximum,
                                         [sl for s_c in scores for sl in lane_slabs(s_c)]),
                        axis=-1, keepdims=True)
            acc = jnp.zeros((qb, 2 * ATT_HEAD_DIM), F32)
            ones = jnp.ones((qb, ATT_HEAD_DIM), BF16)
            for s_c, v_ref in zip(scores, (v0_ref, v1_ref, v2_ref)):
                p = jnp.exp2((s_c - m).astype(BF16))
                acc += _dot(p, jnp.concatenate([v_ref[:, cols], ones], axis=1))
            o_ref[:, cols] = (acc[:, :ATT_HEAD_DIM] / acc[:, ATT_HEAD_DIM:]).astype(BF16)

    @pl.when(b < 2)
    def _():
        heads(True)

    @pl.when(b >= 2)
    def _():
        heads(False)


def _attention(qkv, profile, n_heads):
    s = qkv.shape[0]
    qb = ATT_QBLOCK
    nb = s // qb
    hps = ATT_HEADS_PER_STEP
    width = hps * ATT_HEAD_DIM
    ngroups = n_heads // hps

    def kv_spec(offset, back):
        return pl.BlockSpec((qb, width), lambda h, b: (jnp.maximum(b - back, 0), offset + h))

    return pl.pallas_call(
        _attn_kernel,
        grid=(ngroups, nb),
        in_specs=[
            pl.BlockSpec((qb, width), lambda h, b: (b, h)),
            kv_spec(ngroups, 2), kv_spec(ngroups, 1), kv_spec(ngroups, 0),
            kv_spec(2 * ngroups, 2), kv_spec(2 * ngroups, 1), kv_spec(2 * ngroups, 0),
            pl.BlockSpec((hps, 1, 4 * qb), lambda h, b: (h, 0, 0)),
        ],
        out_specs=pl.BlockSpec((qb, width), lambda h, b: (b, h)),
        out_shape=jax.ShapeDtypeStruct((s, n_heads * ATT_HEAD_DIM), BF16),
        scratch_shapes=[pltpu.VMEM((hps, qb, 3 * qb), F32)],
        compiler_params=_params("arbitrary", "arbitrary"),
        name="chunk_attention",
    )(qkv, qkv, qkv, qkv, qkv, qkv, qkv, profile)


def _attention_profile(rel_bias):
    qb = ATT_QBLOCK
    n_heads = rel_bias.shape[0]
    assert REL_CLIP <= 2 * qb and qb <= REL_CLIP + 1 and 2 * REL_CLIP + 1 == rel_bias.shape[1]
    n_flat = 2 * qb - REL_CLIP + 1
    top = rel_bias[:, 2 * REL_CLIP:]
    profile = jnp.concatenate([
        jnp.broadcast_to(top, (n_heads, n_flat)),
        jnp.flip(rel_bias[:, REL_CLIP - qb + 1:2 * REL_CLIP], axis=1),
        jnp.broadcast_to(top, (n_heads, qb)),
    ], axis=1)
    return profile.reshape(n_heads, 1, 4 * qb).astype(F32)


def _ssm_prep_kernel(are_ref, aim_ref, ldt_ref, btr_ref, bti_ref, abr_ref, abi_ref, bbr_ref, bbi_ref):
    lam_r, lam_i = are_ref[...], aim_ref[...]
    dt = jnp.exp(ldt_ref[...])
    mag = jnp.exp(lam_r * dt)
    ab_r, ab_i = mag * jnp.cos(lam_i * dt), mag * jnp.sin(lam_i * dt)
    abr_ref[...] = ab_r
    abi_ref[...] = ab_i
    num_r, num_i = ab_r - 1.0, ab_i
    den = lam_r * lam_r + lam_i * lam_i
    co_r = (num_r * lam_r + num_i * lam_i) / den
    co_i = (num_i * lam_r - num_r * lam_i) / den
    for c in range(btr_ref.shape[0]):
        b_r, b_i = btr_ref[c], bti_ref[c]
        bbr_ref[c] = co_r * b_r - co_i * b_i
        bbi_ref[c] = co_r * b_i + co_i * b_r


def _ssm_prep(a_re, a_im, log_dt, bt_re, bt_im):
    g, n = a_re.shape
    gn = jax.ShapeDtypeStruct((g, n), F32)
    cgn = jax.ShapeDtypeStruct(bt_re.shape, F32)
    return pl.pallas_call(
        _ssm_prep_kernel,
        out_shape=(gn, gn, cgn, cgn),
        name="ssm_prep",
    )(a_re, a_im, log_dt, bt_re, bt_im)


def _ssm_kernel(x_ref, g_ref, d_ref, wbr_ref, wbi_ref, wcr_ref, wci_ref, ar_ref, ai_ref, o_ref,
                hn_ref, y_ref, xr_ref, xi_ref, cr_ref, ci_ref):
    i = pl.program_id(0)
    t = x_ref.shape[0]
    nblk, cb, ns = wbr_ref.shape
    n_batches, ppb = xr_ref.shape[0], xr_ref.shape[1]
    nsub = ns // LANES
    assert cb == LANES and 2 * nsub == SUBLANES and n_batches * ppb * 2 == nblk

    h = _rms_norm(x_ref[...], g_ref[...])
    for c in range(nblk):
        hn_ref[c] = h[:, c * LANES:(c + 1) * LANES]

    @pl.when(i == 0)
    def _():
        cr_ref[...] = jnp.zeros(cr_ref.shape, F32)
        ci_ref[...] = jnp.zeros(ci_ref.shape, F32)

    def slab_rows(half, k):
        return pl.ds(half * nsub + k, t, stride=SUBLANES)

    def blocks_of(batch):
        for g8 in range(2 * ppb):
            yield batch * 2 * ppb + g8, g8 // 2, g8 % 2

    def run(*stages):
        stages = list(stages)
        while stages:
            for stage in list(stages):
                if next(stage, "done") == "done":
                    stages.remove(stage)

    def project_in(batch):
        for gb, pair, half in blocks_of(batch):
            hb = hn_ref[gb].astype(BF16)
            for w_ref, dst_ref in ((wbr_ref, xr_ref), (wbi_ref, xi_ref)):
                bu = _dot(hb, w_ref[gb])
                for k in range(nsub):
                    dst_ref[batch, pair, slab_rows(half, k), :] = bu[:, k * LANES:(k + 1) * LANES]
            yield

    def scan(batch):
        coef = [(ar_ref[batch * ppb + p], ai_ref[batch * ppb + p]) for p in range(ppb)]
        steps_per_turn = t // (2 * ppb)

        def step(tt, carry):
            rows = pl.ds(tt * SUBLANES, SUBLANES)
            out = []
            for p in range(ppb):
                (a_r, a_i), x_r, x_i = coef[p], carry[2 * p], carry[2 * p + 1]
                n_r = a_r * x_r - a_i * x_i + xr_ref[batch, p, rows, :]
                n_i = a_r * x_i + a_i * x_r + xi_ref[batch, p, rows, :]
                xr_ref[batch, p, rows, :] = n_r
                xi_ref[batch, p, rows, :] = n_i
                out += [n_r, n_i]
            return tuple(out)

        last = tuple(ref[batch * ppb + p] for p in range(ppb) for ref in (cr_ref, ci_ref))
        for tt in range(t):
            last = step(tt, last)
            if (tt + 1) % steps_per_turn == 0 and tt + 1 < t:
                yield
        for p in range(ppb):
            cr_ref[batch * ppb + p] = last[2 * p]
            ci_ref[batch * ppb + p] = last[2 * p + 1]
        yield

    def project_out(batch):
        for gb, pair, half in blocks_of(batch):
            s_r, s_i = (jnp.concatenate([ref[batch, pair, slab_rows(half, k), :] for k in range(nsub)],
                                        axis=1).astype(BF16) for ref in (xr_ref, xi_ref))
            y_ref[gb] = _dot(s_r, wcr_ref[gb]) + _dot(s_i, wci_ref[gb])
            yield

    def finish(batch):
        for gb, _, _ in blocks_of(batch):
            lanes = slice(gb * LANES, (gb + 1) * LANES)
            o_ref[:, lanes] = jax.nn.gelu(y_ref[gb] + d_ref[:, lanes] * hn_ref[gb]).astype(BF16)
            yield

    run(project_in(0))
    for batch in range(n_batches):
        beside = [project_in(batch + 1)] if batch + 1 < n_batches else []
        beside += [project_out(batch - 1)] if batch > 0 else []
        beside += [finish(batch - 2)] if batch > 1 else []
        run(scan(batch), *beside)
    run(project_out(n_batches - 1), *([finish(n_batches - 2)] if n_batches > 1 else []))
    run(finish(n_batches - 1))


def _ssm_core(x, gain, d_skip, wb_r, wb_i, wc_r, wc_i, ab_r, ab_i):
    s, d = x.shape
    t = min(SSM_ROW_TILE, s)
    nblk, cb, ns = wb_r.shape
    ppb = SSM_PAIRS_PER_BATCH
    n_batches = nblk // (2 * ppb)

    def whole(a):
        return pl.BlockSpec(a.shape, lambda i: (0,) * a.ndim)

    return pl.pallas_call(
        _ssm_kernel,
        grid=(s // t,),
        in_specs=[
            pl.BlockSpec((t, d), lambda i: (i, 0)),
            whole(gain), whole(d_skip),
            whole(wb_r), whole(wb_i), whole(wc_r), whole(wc_i),
            whole(ab_r), whole(ab_i),
        ],
        out_specs=pl.BlockSpec((t, d), lambda i: (i, 0)),
        out_shape=jax.ShapeDtypeStruct((s, d), BF16),
        scratch_shapes=[
            pltpu.VMEM((nblk, t, LANES), F32),
            pltpu.VMEM((nblk, t, LANES), F32),
            pltpu.VMEM((n_batches, ppb, t * SUBLANES, LANES), F32),
            pltpu.VMEM((n_batches, ppb, t * SUBLANES, LANES), F32),
            pltpu.VMEM((nblk // 2, SUBLANES, LANES), F32),
            pltpu.VMEM((nblk // 2, SUBLANES, LANES), F32),
        ],
        compiler_params=_params("arbitrary"),
        name="ssm_core",
    )(x, gain, d_skip, wb_r, wb_i, wc_r, wc_i, ab_r, ab_i)


def _block_diag(w, groups_per_block):
    g, r, c = w.shape
    nblk = g // groups_per_block
    tiled = jnp.tile(w.reshape(nblk, groups_per_block * r, c), (1, 1, groups_per_block))
    row_group = lax.broadcasted_iota(jnp.int32, tiled.shape, 1) // r
    col_group = lax.broadcasted_iota(jnp.int32, tiled.shape, 2) // c
    return jnp.where(row_group == col_group, tiled, jnp.zeros_like(tiled))


def _s5_mixer(x, gain, a_re, a_im, log_dt, b_re, b_im, c_re, c_im, d_skip, w_glu, layer):
    s, d = x.shape
    g, n = a_re.shape
    gpb = SSM_GROUPS_PER_BLOCK
    bt_re, bt_im = b_re.transpose(2, 0, 1), b_im.transpose(2, 0, 1)
    ab_r, ab_i, bb_r, bb_i = _ssm_prep(a_re, a_im, log_dt.reshape(g, 1), bt_re, bt_im)
    wb_r = _block_diag(bb_r.transpose(1, 0, 2), gpb).astype(BF16)
    wb_i = _block_diag(bb_i.transpose(1, 0, 2), gpb).astype(BF16)
    wc_r = _block_diag(c_re.transpose(0, 2, 1), gpb).astype(BF16)
    wc_i = _block_diag(-c_im.transpose(0, 2, 1), gpb).astype(BF16)
    slabs = lambda a: a.reshape(g // (2 * gpb), 2 * gpb * n // LANES, LANES)
    z = _ssm_core(x, gain, d_skip.reshape(1, d), wb_r, wb_i, wc_r, wc_i, slabs(ab_r), slabs(ab_i))
    return _glu_res(z, w_glu, x, layer)


def kernel(x, norm_mix, norm_mlp, mlp_w1, mlp_w2, conv_w_in, conv_w, conv_w_out, pool_w_in, pool_w_group, pool_scale, att_w_qkv, att_q_norm, att_k_norm, att_rel_bias, att_w_out, ssm_a_re, ssm_a_im, ssm_log_dt, ssm_b_re, ssm_b_im, ssm_c_re, ssm_c_im, ssm_d, ssm_w_glu):
    b, s, d = x.shape
    depth = norm_mix.shape[0]
    n_mixers = 4
    outs = []
    for bi in range(b):
        xs = x[bi]
        for i in range(depth):
            kind, j = i % n_mixers, i // n_mixers
            gain = norm_mix[i].reshape(1, d)
            if kind == 0:
                gated = _conv_front(xs, gain, conv_w_in, conv_w[j], j)
                xs = _proj_res(gated, conv_w_out, xs, j)
            elif kind == 1:
                xs = _pool_mixer(xs, gain, pool_w_in, pool_w_group, pool_scale[j].reshape(1, d), j)
            elif kind == 2:
                n_heads = d // ATT_HEAD_DIM
                head_gains = jnp.concatenate([jnp.tile(att_q_norm[j], n_heads),
                                              jnp.tile(att_k_norm[j], n_heads)]).reshape(1, 2 * d)
                qkv = _qkv_proj(xs, gain, att_w_qkv, head_gains, j)
                att = _attention(qkv, _attention_profile(att_rel_bias[j]), n_heads)
                xs = _proj_res(att, att_w_out, xs, j)
            else:
                xs = _s5_mixer(xs, gain, ssm_a_re[j], ssm_a_im[j], ssm_log_dt[j], ssm_b_re[j],
                               ssm_b_im[j], ssm_c_re[j], ssm_c_im[j], ssm_d[j], ssm_w_glu, j)
            xs = _mlp(xs, norm_mlp[i].reshape(1, d), mlp_w1, mlp_w2, i)
        outs.append(xs)
    return outs[0][None] if b == 1 else jnp.stack(outs)
```

```python
import functools
import math

import jax
import jax.numpy as jnp
from jax import lax
from jax.experimental import pallas as pl
from jax.experimental.pallas import tpu as pltpu

F32 = jnp.float32
BF16 = jnp.bfloat16

RMS_EPS = 1e-6
CHUNK = 64
ATT_HEAD_DIM = 128
ATT_LEFT_CHUNKS = 8
REL_CLIP = 256
MASK_VALUE = -1e30
POOL_WINDOWS = (2, 4, 8, 16)
CONV_WIDTH = 3

SUBLANES = 8
LANES = 128
VMEM_LIMIT_BYTES = 56 * 1024 * 1024

ROW_TILE = 1024
CONV_COL_TILE = 512
QKV_COL_TILE = 1024
PROJ_ROW_TILE = 512
PROJ_COL_TILE = 2048
GLU_COL_TILE = 1024
FF_TILE = 512
MLP_OUT_TILE = 512
MLP_K_TILE = 2048
ATT_QBLOCK = 256
ATT_HEADS_PER_STEP = 16
SSM_ROW_TILE = 256
SSM_PAIRS_PER_BATCH = 4
SSM_GROUPS_PER_BLOCK = 8
HALO = 16
POOL_HALO = 32
POOL_ROW_TILE = 512


def _params(*semantics):
    return pltpu.CompilerParams(dimension_semantics=semantics, vmem_limit_bytes=VMEM_LIMIT_BYTES)


def _rms_norm(x, gain):
    return x * lax.rsqrt(jnp.mean(x * x, axis=-1, keepdims=True) + RMS_EPS) * gain


def _dot(a, b):
    return jnp.dot(a, b, preferred_element_type=F32)


def _wdot(a, w_ref):
    return _dot(a, w_ref[...].astype(BF16))


def _mlp_kernel(x_hbm, g_ref, w1_hbm, w2_hbm, xr_ref, o_ref, x_buf, h_ref, a_ref, w1_buf, w2_buf,
                semx, sem1, sem2, *, layer, n_rows):
    i, t = pl.program_id(0), pl.program_id(1)
    n_out = pl.num_programs(1) - 1
    tf, tk, tn = w1_buf.shape[2], w2_buf.shape[1], w2_buf.shape[2]
    n_up, n_k = a_ref.shape[1] // tf, a_ref.shape[1] // tk
    assert n_up % 2 == 0 and n_k % 2 == 0

    def w1_copy(f, slot):
        return pltpu.make_async_copy(w1_hbm.at[layer, :, pl.ds(f * tf, tf)], w1_buf.at[slot],
                                     sem1.at[slot])

    def w2_copy(n, kh):
        return pltpu.make_async_copy(w2_hbm.at[layer, pl.ds(kh * tk, tk), pl.ds(n * tn, tn)],
                                     w2_buf.at[kh % 2], sem2.at[kh % 2])

    def x_copy(row):
        tm = x_buf.shape[0]
        return pltpu.make_async_copy(x_hbm.at[pl.ds(row * tm, tm), :], x_buf, semx.at[0])

    @pl.when(t == 0)
    def _():
        @pl.when(i == 0)
        def _():
            x_copy(0).start()
            w1_copy(0, 0).start()

        w2_copy(0, 0).start()
        w2_copy(0, 1).start()

        def up_tile(f, slot, h):
            a = jnp.maximum(_dot(h, w1_buf[slot].astype(BF16)), 0.0)
            cols = pl.ds(f * tf, tf) if isinstance(f, int) else pl.ds(pl.multiple_of(f * tf, tf), tf)
            a_ref[:, cols] = (a * a).astype(BF16)

        x_copy(i).wait()
        w1_copy(0, 0).wait()
        w1_copy(1, 1).start()
        h = _rms_norm(x_buf[...], g_ref[...]).astype(BF16)
        h_ref[...] = h
        up_tile(0, 0, h)

        @pl.when(i + 1 < n_rows)
        def _():
            x_copy(i + 1).start()

        def up_pair(p, _):
            for f, slot in ((2 * p + 1, 1), (2 * p + 2, 0)):
                w1_copy(f, slot).wait()
                w1_copy(f + 1, 1 - slot).start()
                up_tile(f, slot, h_ref[...])
            return 0

        lax.fori_loop(0, (n_up - 2) // 2, up_pair, 0)
        w1_copy(n_up - 1, 1).wait()
        up_tile(n_up - 1, 1, h_ref[...])

    @pl.when(t > 0)
    def _():
        n = t - 1
        for kh in range(n_k):
            w2_copy(n, kh).wait()
            part = _dot(a_ref[:, kh * tk:(kh + 1) * tk], w2_buf[kh % 2].astype(BF16))
            if kh == 0:
                o_ref[...] = xr_ref[...] + part
            else:
                o_ref[...] += part
            if kh + 2 < n_k:
                w2_copy(n, kh + 2).start()
            else:
                @pl.when(n + 1 < n_out)
                def _(kh=kh):
                    w2_copy(n + 1, kh + 2 - n_k).start()

        @pl.when((n + 1 == n_out) & (i + 1 < n_rows))
        def _():
            w1_copy(0, 0).start()


def _mlp(x, gain, w1, w2, layer):
    s, d = x.shape
    dff = w1.shape[2]
    tm, tf, tn, tk = min(ROW_TILE, s), FF_TILE, MLP_OUT_TILE, MLP_K_TILE
    n_rows, n_out = s // tm, d // tn

    def out_tile(t):
        return jnp.maximum(t - 1, 0)

    return pl.pallas_call(
        functools.partial(_mlp_kernel, layer=layer, n_rows=n_rows),
        grid=(n_rows, 1 + n_out),
        in_specs=[
            pl.BlockSpec(memory_space=pl.ANY),
            pl.BlockSpec((1, d), lambda i, t: (0, 0)),
            pl.BlockSpec(memory_space=pl.ANY),
            pl.BlockSpec(memory_space=pl.ANY),
            pl.BlockSpec((tm, tn), lambda i, t: (i, out_tile(t))),
        ],
        out_specs=pl.BlockSpec((tm, tn), lambda i, t: (i, out_tile(t))),
        out_shape=jax.ShapeDtypeStruct((s, d), F32),
        scratch_shapes=[
            pltpu.VMEM((tm, d), F32),
            pltpu.VMEM((tm, d), BF16),
            pltpu.VMEM((tm, dff), BF16),
            pltpu.VMEM((2, d, tf), F32),
            pltpu.VMEM((2, tk, tn), F32),
            pltpu.SemaphoreType.DMA((1,)),
            pltpu.SemaphoreType.DMA((2,)),
            pltpu.SemaphoreType.DMA((2,)),
        ],
        compiler_params=_params("arbitrary", "arbitrary"),
        name="mlp",
    )(x, gain, w1, w2, x)


def _proj_res_kernel(a_ref, w_ref, x_ref, o_ref, wb_ref):
    @pl.when(pl.program_id(1) == 0)
    def _():
        wb_ref[...] = w_ref[...].astype(BF16)

    o_ref[...] = x_ref[...] + _dot(a_ref[...], wb_ref[...])


def _proj_res(a, w, x, layer):
    s, k = a.shape
    n = w.shape[2]
    tm, tn = min(PROJ_ROW_TILE, s), min(PROJ_COL_TILE, n)
    return pl.pallas_call(
        _proj_res_kernel,
        grid=(n // tn, s // tm),
        in_specs=[
            pl.BlockSpec((tm, k), lambda j, i: (i, 0)),
            pl.BlockSpec((None, k, tn), lambda j, i: (layer, 0, j), pipeline_mode=pl.Buffered(1)),
            pl.BlockSpec((tm, tn), lambda j, i: (i, j)),
        ],
        out_specs=pl.BlockSpec((tm, tn), lambda j, i: (i, j)),
        out_shape=jax.ShapeDtypeStruct((s, n), F32),
        scratch_shapes=[pltpu.VMEM((k, tn), BF16)],
        compiler_params=_params("arbitrary", "arbitrary"),
        name="proj_res",
    )(a, w, x)


def _glu_res_kernel(a_ref, wv_ref, wg_ref, x_ref, o_ref, wvb_ref, wgb_ref):
    @pl.when(pl.program_id(1) == 0)
    def _():
        wvb_ref[...] = wv_ref[...].astype(BF16)
        wgb_ref[...] = wg_ref[...].astype(BF16)

    a = a_ref[...]
    val = _dot(a, wvb_ref[...])
    gate = _dot(a, wgb_ref[...])
    o_ref[...] = x_ref[...] + val * jax.nn.sigmoid(gate)


def _glu_res(a, w_glu, x, layer):
    s, k = a.shape
    n = w_glu.shape[2] // 2
    tm, tn = min(PROJ_ROW_TILE, s), min(GLU_COL_TILE, n)
    nj = n // tn
    return pl.pallas_call(
        _glu_res_kernel,
        grid=(nj, s // tm),
        in_specs=[
            pl.BlockSpec((tm, k), lambda j, i: (i, 0)),
            pl.BlockSpec((None, k, tn), lambda j, i: (layer, 0, j), pipeline_mode=pl.Buffered(1)),
            pl.BlockSpec((None, k, tn), lambda j, i: (layer, 0, j + nj), pipeline_mode=pl.Buffered(1)),
            pl.BlockSpec((tm, tn), lambda j, i: (i, j)),
        ],
        out_specs=pl.BlockSpec((tm, tn), lambda j, i: (i, j)),
        out_shape=jax.ShapeDtypeStruct((s, n), F32),
        scratch_shapes=[pltpu.VMEM((k, tn), BF16), pltpu.VMEM((k, tn), BF16)],
        compiler_params=_params("arbitrary", "arbitrary"),
        name="glu_res",
    )(a, w_glu, w_glu, x)


def _conv_front_kernel(x_ref, g_ref, wb_ref, wc_ref, wv_ref, cw_ref, o_ref, h_ref, ext_ref, carry_ref):
    i, j = pl.program_id(0), pl.program_id(1)
    tm = o_ref.shape[0]

    @pl.when(i == 0)
    def _():
        carry_ref[j] = jnp.zeros(carry_ref.shape[1:], F32)

    def column_tile(h):
        u = _wdot(h, wc_ref) * _wdot(h, wv_ref)
        ext_ref[0:HALO, :] = carry_ref[j]
        ext_ref[HALO:, :] = u
        carry_ref[j] = u[tm - HALO:, :]
        conv = cw_ref[CONV_WIDTH - 1:CONV_WIDTH, :] * u
        for lag in range(1, CONV_WIDTH):
            tap = CONV_WIDTH - 1 - lag
            conv += cw_ref[tap:tap + 1, :] * ext_ref[pl.ds(HALO - lag, tm), :]
        o_ref[...] = (_wdot(h, wb_ref) * conv).astype(BF16)

    @pl.when(j == 0)
    def _():
        h = _rms_norm(x_ref[...], g_ref[...]).astype(BF16)
        h_ref[...] = h
        column_tile(h)

    @pl.when(j > 0)
    def _():
        column_tile(h_ref[...])


def _conv_front(x, gain, w_in, conv_w, layer):
    s, d = x.shape
    tm, tn = min(ROW_TILE, s), CONV_COL_TILE
    nj = d // tn
    return pl.pallas_call(
        _conv_front_kernel,
        grid=(s // tm, nj),
        in_specs=[
            pl.BlockSpec((tm, d), lambda i, j: (i, 0)),
            pl.BlockSpec((1, d), lambda i, j: (0, 0)),
            pl.BlockSpec((None, d, tn), lambda i, j: (layer, 0, j)),
            pl.BlockSpec((None, d, tn), lambda i, j: (layer, 0, j + nj)),
            pl.BlockSpec((None, d, tn), lambda i, j: (layer, 0, j + 2 * nj)),
            pl.BlockSpec((CONV_WIDTH, tn), lambda i, j: (0, j)),
        ],
        out_specs=pl.BlockSpec((tm, tn), lambda i, j: (i, j)),
        out_shape=jax.ShapeDtypeStruct((s, d), BF16),
        scratch_shapes=[
            pltpu.VMEM((tm, d), BF16),
            pltpu.VMEM((tm + HALO, tn), F32),
            pltpu.VMEM((nj, HALO, tn), F32),
        ],
        compiler_params=_params("arbitrary", "arbitrary"),
        name="conv_front",
    )(x, gain, w_in, w_in, w_in, conv_w)


def _round_kernel(w_ref, o_ref):
    o_ref[...] = w_ref[...].astype(BF16)


def _round_bf16(w, layer, block_rows):
    _, r, c = w.shape
    return pl.pallas_call(
        _round_kernel,
        grid=(r // block_rows,),
        in_specs=[pl.BlockSpec((None, block_rows, c), lambda i: (layer, i, 0))],
        out_specs=pl.BlockSpec((block_rows, c), lambda i: (i, 0)),
        out_shape=jax.ShapeDtypeStruct((r, c), BF16),
        compiler_params=_params("parallel"),
        name="round_bf16",
    )(w)


def _pool_kernel(x_ref, g_ref, win_ref, wg_ref, sc_ref, o_ref, ext_ref, la_ref, lb_ref, carry_ref):
    i = pl.program_id(0)
    tm = o_ref.shape[0]
    halo = POOL_HALO
    n_ext = tm + halo
    pg = wg_ref.shape[1]

    @pl.when(i == 0)
    def _():
        carry_ref[...] = jnp.zeros(carry_ref.shape, F32)

    x = x_ref[...]
    h = _rms_norm(x, g_ref[...]).astype(BF16)
    pos = (i * tm + 1 + lax.broadcasted_iota(jnp.int32, (tm, 1), 0)).astype(F32)

    for gi, w in enumerate(POOL_WINDOWS):
        cols = slice(gi * pg, (gi + 1) * pg)
        u = _dot(h, win_ref[:, cols])
        ext_ref[gi, 0:halo, :] = carry_ref[gi]
        ext_ref[gi, halo:, :] = u
        carry_ref[gi] = u[tm - halo:, :]
        levels = int(math.log2(w))
        assert 2 ** levels == w and SUBLANES * levels <= halo
        src = ext_ref
        for lv in range(1, levels):
            dst = la_ref if lv % 2 else lb_ref
            lo, shift = SUBLANES * lv, 2 ** (lv - 1)
            dst[gi, lo:, :] = src[gi, lo:, :] + src[gi, pl.ds(lo - shift, n_ext - lo), :]
            src = dst
        acc = src[gi, halo:, :] + src[gi, pl.ds(halo - w // 2, tm), :]
        inv_count = 1.0 / jnp.minimum(pos, float(w))
        pooled = (acc * inv_count - u).astype(BF16)
        o_ref[:, cols] = x[:, cols] + _dot(pooled, wg_ref[gi]) * sc_ref[:, cols]


def _pool_mixer(x, gain, w_in, w_group, scale, layer):
    s, d = x.shape
    ng, pg = w_group.shape[1], w_group.shape[2]
    tm = min(POOL_ROW_TILE, s)
    win_b = _round_bf16(w_in, layer, d // ng)
    wg_b = _round_bf16(w_group.reshape(w_group.shape[0], ng * pg, pg), layer, pg).reshape(ng, pg, pg)
    return pl.pallas_call(
        _pool_kernel,
        grid=(s // tm,),
        in_specs=[
            pl.BlockSpec((tm, d), lambda i: (i, 0)),
            pl.BlockSpec((1, d), lambda i: (0, 0)),
            pl.BlockSpec((d, d), lambda i: (0, 0)),
            pl.BlockSpec((ng, pg, pg), lambda i: (0, 0, 0)),
            pl.BlockSpec((1, d), lambda i: (0, 0)),
        ],
        out_specs=pl.BlockSpec((tm, d), lambda i: (i, 0)),
        out_shape=jax.ShapeDtypeStruct((s, d), F32),
        scratch_shapes=[
            pltpu.VMEM((ng, tm + POOL_HALO, pg), F32),
            pltpu.VMEM((ng, tm + POOL_HALO, pg), F32),
            pltpu.VMEM((ng, tm + POOL_HALO, pg), F32),
            pltpu.VMEM((ng, POOL_HALO, pg), F32),
        ],
        compiler_params=_params("arbitrary"),
        name="pool_mixer",
    )(x, gain, win_b, wg_b, scale)


def _qkv_kernel(x_ref, g_ref, w_ref, hg_ref, o_ref, h_ref, *, n_norm_tiles):
    j = pl.program_id(1)

    def normed_heads(h):
        y = _wdot(h, w_ref)
        for hd in range(y.shape[1] // ATT_HEAD_DIM):
            sl = slice(hd * ATT_HEAD_DIM, (hd + 1) * ATT_HEAD_DIM)
            o_ref[:, sl] = _rms_norm(y[:, sl], hg_ref[:, sl]).astype(BF16)

    @pl.when(j == 0)
    def _():
        h = _rms_norm(x_ref[...], g_ref[...]).astype(BF16)
        h_ref[...] = h
        normed_heads(h)

    @pl.when((j > 0) & (j < n_norm_tiles))
    def _():
        normed_heads(h_ref[...])

    @pl.when(j >= n_norm_tiles)
    def _():
        o_ref[...] = _wdot(h_ref[...], w_ref).astype(BF16)


def _qkv_proj(x, gain, w_qkv, head_gains, layer):
    s, d = x.shape
    n = w_qkv.shape[2]
    tm, tn = min(ROW_TILE, s), QKV_COL_TILE
    n_norm_tiles = head_gains.shape[1] // tn
    return pl.pallas_call(
        functools.partial(_qkv_kernel, n_norm_tiles=n_norm_tiles),
        grid=(s // tm, n // tn),
        in_specs=[
            pl.BlockSpec((tm, d), lambda i, j: (i, 0)),
            pl.BlockSpec((1, d), lambda i, j: (0, 0)),
            pl.BlockSpec((None, d, tn), lambda i, j: (layer, 0, j)),
            pl.BlockSpec((1, tn), lambda i, j: (0, jnp.minimum(j, n_norm_tiles - 1))),
        ],
        out_specs=pl.BlockSpec((tm, tn), lambda i, j: (i, j)),
        out_shape=jax.ShapeDtypeStruct((s, n), BF16),
        scratch_shapes=[pltpu.VMEM((tm, d), BF16)],
        compiler_params=_params("parallel", "arbitrary"),
        name="qkv_proj",
    )(x, gain, w_qkv, head_gains)


def _attn_kernel(q_ref, k0_ref, k1_ref, k2_ref, v0_ref, v1_ref, v2_ref, prof_ref, o_ref, bias_ref):
    b = pl.program_id(1)
    qb = q_ref.shape[0]
    n_local = q_ref.shape[1] // ATT_HEAD_DIM
    log2e = math.log2(math.e)
    scale = ATT_HEAD_DIM ** -0.5 * log2e

    @pl.when(b == 0)
    def _():
        q_idx = lax.broadcasted_iota(jnp.int32, (qb, 3 * qb), 0) + 2 * qb
        k_idx = lax.broadcasted_iota(jnp.int32, (qb, 3 * qb), 1)
        chunk_start = (q_idx // CHUNK) * CHUNK
        in_band = (k_idx >= chunk_start - ATT_LEFT_CHUNKS * CHUNK) & (k_idx < chunk_start + CHUNK)
        for hh in range(n_local):
            profile = jnp.broadcast_to(prof_ref[hh], (qb, prof_ref.shape[2]))
            table = pltpu.roll(profile, 0, 1, stride=1, stride_axis=0)[:, :3 * qb]
            bias_ref[hh] = jnp.where(in_band, table * log2e, MASK_VALUE)

    def lane_slabs(a):
        return [a[:, k * LANES:(k + 1) * LANES] for k in range(a.shape[1] // LANES)]

    def heads(first_blocks):
        for hh in range(n_local):
            cols = slice(hh * ATT_HEAD_DIM, (hh + 1) * ATT_HEAD_DIM)
            q = q_ref[:, cols]
            scores = []
            for c, k_ref in enumerate((k0_ref, k1_ref, k2_ref)):
                s_c = lax.dot_general(q, k_ref[:, cols], (((1,), (1,)), ((), ())),
                                      preferred_element_type=F32)
                s_c = s_c * scale + bias_ref[hh, :, c * qb:(c + 1) * qb]
                if first_blocks and c < 2:
                    s_c = jnp.where(b - 2 + c >= 0, s_c, MASK_VALUE)
                scores.append(s_c)
            m = jnp.max(functools.reduce(jnp.maximum,
                                         [sl for s_c in scores for sl in lane_slabs(s_c)]),
                        axis=-1, keepdims=True)
            acc = jnp.zeros((qb, 2 * ATT_HEAD_DIM), F32)
            ones = jnp.ones((qb, ATT_HEAD_DIM), BF16)
            for s_c, v_ref in zip(scores, (v0_ref, v1_ref, v2_ref)):
                p = jnp.exp2((s_c - m).astype(BF16))
                acc += _dot(p, jnp.concatenate([v_ref[:, cols], ones], axis=1))
            o_ref[:, cols] = (acc[:, :ATT_HEAD_DIM] / acc[:, ATT_HEAD_DIM:]).astype(BF16)

    @pl.when(b < 2)
    def _():
        heads(True)

    @pl.when(b >= 2)
    def _():
        heads(False)


def _attention(qkv, profile, n_heads):
    s = qkv.shape[0]
    qb = ATT_QBLOCK
    nb = s // qb
    hps = ATT_HEADS_PER_STEP
    width = hps * ATT_HEAD_DIM
    ngroups = n_heads // hps

    def kv_spec(offset, back):
        return pl.BlockSpec((qb, width), lambda h, b: (jnp.maximum(b - back, 0), offset + h))

    return pl.pallas_call(
        _attn_kernel,
        grid=(ngroups, nb),
        in_specs=[
            pl.BlockSpec((qb, width), lambda h, b: (b, h)),
            kv_spec(ngroups, 2), kv_spec(ngroups, 1), kv_spec(ngroups, 0),
            kv_spec(2 * ngroups, 2), kv_spec(2 * ngroups, 1), kv_spec(2 * ngroups, 0),
            pl.BlockSpec((hps, 1, 4 * qb), lambda h, b: (h, 0, 0)),
        ],
        out_specs=pl.BlockSpec((qb, width), lambda h, b: (b, h)),
        out_shape=jax.ShapeDtypeStruct((s, n_heads * ATT_HEAD_DIM), BF16),
        scratch_shapes=[pltpu.VMEM((hps, qb, 3 * qb), F32)],
        compiler_params=_params("arbitrary", "arbitrary"),
        name="chunk_attention",
    )(qkv, qkv, qkv, qkv, qkv, qkv, qkv, profile)


def _attention_profile(rel_bias):
    qb = ATT_QBLOCK
    n_heads = rel_bias.shape[0]
    assert REL_CLIP <= 2 * qb and qb <= REL_CLIP + 1 and 2 * REL_CLIP + 1 == rel_bias.shape[1]
    n_flat = 2 * qb - REL_CLIP + 1
    top = rel_bias[:, 2 * REL_CLIP:]
    profile = jnp.concatenate([
        jnp.broadcast_to(top, (n_heads, n_flat)),
        jnp.flip(rel_bias[:, REL_CLIP - qb + 1:2 * REL_CLIP], axis=1),
        jnp.broadcast_to(top, (n_heads, qb)),
    ], axis=1)
    return profile.reshape(n_heads, 1, 4 * qb).astype(F32)


def _ssm_prep_kernel(are_ref, aim_ref, ldt_ref, btr_ref, bti_ref, abr_ref, abi_ref, bbr_ref, bbi_ref):
    lam_r, lam_i = are_ref[...], aim_ref[...]
    dt = jnp.exp(ldt_ref[...])
    mag = jnp.exp(lam_r * dt)
    ab_r, ab_i = mag * jnp.cos(lam_i * dt), mag * jnp.sin(lam_i * dt)
    abr_ref[...] = ab_r
    abi_ref[...] = ab_i
    num_r, num_i = ab_r - 1.0, ab_i
    den = lam_r * lam_r + lam_i * lam_i
    co_r = (num_r * lam_r + num_i * lam_i) / den
    co_i = (num_i * lam_r - num_r * lam_i) / den
    for c in range(btr_ref.shape[0]):
        b_r, b_i = btr_ref[c], bti_ref[c]
        bbr_ref[c] = co_r * b_r - co_i * b_i
        bbi_ref[c] = co_r * b_i + co_i * b_r


def _ssm_prep(a_re, a_im, log_dt, bt_re, bt_im):
    g, n = a_re.shape
    gn = jax.ShapeDtypeStruct((g, n), F32)
    cgn = jax.ShapeDtypeStruct(bt_re.shape, F32)
    return pl.pallas_call(
        _ssm_prep_kernel,
        out_shape=(gn, gn, cgn, cgn),
        name="ssm_prep",
    )(a_re, a_im, log_dt, bt_re, bt_im)


def _ssm_kernel(x_ref, g_ref, d_ref, wbr_ref, wbi_ref, wcr_ref, wci_ref, ar_ref, ai_ref, o_ref,
                hn_ref, y_ref, xr_ref, xi_ref, cr_ref, ci_ref):
    i = pl.program_id(0)
    t = x_ref.shape[0]
    nblk, cb, ns = wbr_ref.shape
    n_batches, ppb = xr_ref.shape[0], xr_ref.shape[1]
    nsub = ns // LANES
    assert cb == LANES and 2 * nsub == SUBLANES and n_batches * ppb * 2 == nblk

    h = _rms_norm(x_ref[...], g_ref[...])
    for c in range(nblk):
        hn_ref[c] = h[:, c * LANES:(c + 1) * LANES]

    @pl.when(i == 0)
    def _():
        cr_ref[...] = jnp.zeros(cr_ref.shape, F32)
        ci_ref[...] = jnp.zeros(ci_ref.shape, F32)

    def slab_rows(half, k):
        return pl.ds(half * nsub + k, t, stride=SUBLANES)

    def blocks_of(batch):
        for g8 in range(2 * ppb):
            yield batch * 2 * ppb + g8, g8 // 2, g8 % 2

    def run(*stages):
        stages = list(stages)
        while stages:
            for stage in list(stages):
                if next(stage, "done") == "done":
                    stages.remove(stage)

    def project_in(batch):
        for gb, pair, half in blocks_of(batch):
            hb = hn_ref[gb].astype(BF16)
            for w_ref, dst_ref in ((wbr_ref, xr_ref), (wbi_ref, xi_ref)):
                bu = _dot(hb, w_ref[gb])
                for k in range(nsub):
                    dst_ref[batch, pair, slab_rows(half, k), :] = bu[:, k * LANES:(k + 1) * LANES]
            yield

    def scan(batch):
        coef = [(ar_ref[batch * ppb + p], ai_ref[batch * ppb + p]) for p in range(ppb)]
        steps_per_turn = t // (2 * ppb)

        def step(tt, carry):
            rows = pl.ds(tt * SUBLANES, SUBLANES)
            out = []
            for p in range(ppb):
                (a_r, a_i), x_r, x_i = coef[p], carry[2 * p], carry[2 * p + 1]
                n_r = a_r * x_r - a_i * x_i + xr_ref[batch, p, rows, :]
                n_i = a_r * x_i + a_i * x_r + xi_ref[batch, p, rows, :]
                xr_ref[batch, p, rows, :] = n_r
                xi_ref[batch, p, rows, :] = n_i
                out += [n_r, n_i]
            return tuple(out)

        last = tuple(ref[batch * ppb + p] for p in range(ppb) for ref in (cr_ref, ci_ref))
        for tt in range(t):
            last = step(tt, last)
            if (tt + 1) % steps_per_turn == 0 and tt + 1 < t:
                yield
        for p in range(ppb):
            cr_ref[batch * ppb + p] = last[2 * p]
            ci_ref[batch * ppb + p] = last[2 * p + 1]
        yield

    def project_out(batch):
        for gb, pair, half in blocks_of(batch):
            s_r, s_i = (jnp.concatenate([ref[batch, pair, slab_rows(half, k), :] for k in range(nsub)],
                                        axis=1).astype(BF16) for ref in (xr_ref, xi_ref))
            y_ref[gb] = _dot(s_r, wcr_ref[gb]) + _dot(s_i, wci_ref[gb])
            yield

    def finish(batch):
        for gb, _, _ in blocks_of(batch):
            lanes = slice(gb * LANES, (gb + 1) * LANES)
            o_ref[:, lanes] = jax.nn.gelu(y_ref[gb] + d_ref[:, lanes] * hn_ref[gb]).astype(BF16)
            yield

    run(project_in(0))
    for batch in range(n_batches):
        beside = [project_in(batch + 1)] if batch + 1 < n_batches else []
        beside += [project_out(batch - 1)] if batch > 0 else []
        beside += [finish(batch - 2)] if batch > 1 else []
        run(scan(batch), *beside)
    run(project_out(n_batches - 1), *([finish(n_batches - 2)] if n_batches > 1 else []))
    run(finish(n_batches - 1))


def _ssm_core(x, gain, d_skip, wb_r, wb_i, wc_r, wc_i, ab_r, ab_i):
    s, d = x.shape
    t = min(SSM_ROW_TILE, s)
    nblk, cb, ns = wb_r.shape
    ppb = SSM_PAIRS_PER_BATCH
    n_batches = nblk // (2 * ppb)

    def whole(a):
        return pl.BlockSpec(a.shape, lambda i: (0,) * a.ndim)

    return pl.pallas_call(
        _ssm_kernel,
        grid=(s // t,),
        in_specs=[
            pl.BlockSpec((t, d), lambda i: (i, 0)),
            whole(gain), whole(d_skip),
            whole(wb_r), whole(wb_i), whole(wc_r), whole(wc_i),
            whole(ab_r), whole(ab_i),
        ],
        out_specs=pl.BlockSpec((t, d), lambda i: (i, 0)),
        out_shape=jax.ShapeDtypeStruct((s, d), BF16),
        scratch_shapes=[
            pltpu.VMEM((nblk, t, LANES), F32),
            pltpu.VMEM((nblk, t, LANES), F32),
            pltpu.VMEM((n_batches, ppb, t * SUBLANES, LANES), F32),
            pltpu.VMEM((n_batches, ppb, t * SUBLANES, LANES), F32),
            pltpu.VMEM((nblk // 2, SUBLANES, LANES), F32),
            pltpu.VMEM((nblk // 2, SUBLANES, LANES), F32),
        ],
        compiler_params=_params("arbitrary"),
        name="ssm_core",
    )(x, gain, d_skip, wb_r, wb_i, wc_r, wc_i, ab_r, ab_i)


def _block_diag(w, groups_per_block):
    g, r, c = w.shape
    nblk = g // groups_per_block
    tiled = jnp.tile(w.reshape(nblk, groups_per_block * r, c), (1, 1, groups_per_block))
    row_group = lax.broadcasted_iota(jnp.int32, tiled.shape, 1) // r
    col_group = lax.broadcasted_iota(jnp.int32, tiled.shape, 2) // c
    return jnp.where(row_group == col_group, tiled, jnp.zeros_like(tiled))


def _s5_mixer(x, gain, a_re, a_im, log_dt, b_re, b_im, c_re, c_im, d_skip, w_glu, layer):
    s, d = x.shape
    g, n = a_re.shape
    gpb = SSM_GROUPS_PER_BLOCK
    bt_re, bt_im = b_re.transpose(2, 0, 1), b_im.transpose(2, 0, 1)
    ab_r, ab_i, bb_r, bb_i = _ssm_prep(a_re, a_im, log_dt.reshape(g, 1), bt_re, bt_im)
    wb_r = _block_diag(bb_r.transpose(1, 0, 2), gpb).astype(BF16)
    wb_i = _block_diag(bb_i.transpose(1, 0, 2), gpb).astype(BF16)
    wc_r = _block_diag(c_re.transpose(0, 2, 1), gpb).astype(BF16)
    wc_i = _block_diag(-c_im.transpose(0, 2, 1), gpb).astype(BF16)
    slabs = lambda a: a.reshape(g // (2 * gpb), 2 * gpb * n // LANES, LANES)
    z = _ssm_core(x, gain, d_skip.reshape(1, d), wb_r, wb_i, wc_r, wc_i, slabs(ab_r), slabs(ab_i))
    return _glu_res(z, w_glu, x, layer)


def kernel(x, norm_mix, norm_mlp, mlp_w1, mlp_w2, conv_w_in, conv_w, conv_w_out, pool_w_in, pool_w_group, pool_scale, att_w_qkv, att_q_norm, att_k_norm, att_rel_bias, att_w_out, ssm_a_re, ssm_a_im, ssm_log_dt, ssm_b_re, ssm_b_im, ssm_c_re, ssm_c_im, ssm_d, ssm_w_glu):
    b, s, d = x.shape
    depth = norm_mix.shape[0]
    n_mixers = 4
    outs = []
    for bi in range(b):
        xs = x[bi]
        for i in range(depth):
            kind, j = i % n_mixers, i // n_mixers
            gain = norm_mix[i].reshape(1, d)
            if kind == 0:
                gated = _conv_front(xs, gain, conv_w_in, conv_w[j], j)
                xs = _proj_res(gated, conv_w_out, xs, j)
            elif kind == 1:
                xs = _pool_mixer(xs, gain, pool_w_in, pool_w_group, pool_scale[j].reshape(1, d), j)
            elif kind == 2:
                n_heads = d // ATT_HEAD_DIM
                head_gains = jnp.concatenate([jnp.tile(att_q_norm[j], n_heads),
                                              jnp.tile(att_k_norm[j], n_heads)]).reshape(1, 2 * d)
                qkv = _qkv_proj(xs, gain, att_w_qkv, head_gains, j)
                att = _attention(qkv, _attention_profile(att_rel_bias[j]), n_heads)
                xs = _proj_res(att, att_w_out, xs, j)
            else:
                xs = _s5_mixer(xs, gain, ssm_a_re[j], ssm_a_im[j], ssm_log_dt[j], ssm_b_re[j],
                               ssm_b_im[j], ssm_c_re[j], ssm_c_im[j], ssm_d[j], ssm_w_glu, j)
            xs = _mlp(xs, norm_mlp[i].reshape(1, d), mlp_w1, mlp_w2, i)
        outs.append(xs)
    return outs[0][None] if b == 1 else jnp.stack(outs)
```

```python
import functools
import math

import jax
import jax.numpy as jnp
from jax import lax
from jax.experimental import pallas as pl
from jax.experimental.pallas import tpu as pltpu

F32 = jnp.float32
BF16 = jnp.bfloat16

RMS_EPS = 1e-6
CHUNK = 64
ATT_HEAD_DIM = 128
ATT_LEFT_CHUNKS = 8
REL_CLIP = 256
MASK_VALUE = -1e30
POOL_WINDOWS = (2, 4, 8, 16)
CONV_WIDTH = 3

SUBLANES = 8
LANES = 128
VMEM_LIMIT_BYTES = 56 * 1024 * 1024

ROW_TILE = 1024
CONV_COL_TILE = 512
QKV_COL_TILE = 1024
PROJ_ROW_TILE = 512
PROJ_COL_TILE = 2048
GLU_COL_TILE = 1024
FF_TILE = 512
MLP_OUT_TILE = 512
MLP_K_TILE = 2048
ATT_QBLOCK = 256
ATT_HEADS_PER_STEP = 16
SSM_ROW_TILE = 256
SSM_PAIRS_PER_BATCH = 4
SSM_GROUPS_PER_BLOCK = 8
HALO = 16
POOL_HALO = 32
POOL_ROW_TILE = 512


def _params(*semantics):
    return pltpu.CompilerParams(dimension_semantics=semantics, vmem_limit_bytes=VMEM_LIMIT_BYTES)


def _rms_norm(x, gain):
    return x * lax.rsqrt(jnp.mean(x * x, axis=-1, keepdims=True) + RMS_EPS) * gain


def _dot(a, b):
    return jnp.dot(a, b, preferred_element_type=F32)


def _wdot(a, w_ref):
    return _dot(a, w_ref[...].astype(BF16))


def _mlp_kernel(x_hbm, g_ref, w1_hbm, w2_hbm, xr_ref, o_ref, x_buf, h_ref, a_ref, w1_buf, w2_buf,
                semx, sem1, sem2, *, layer, n_rows):
    i, t = pl.program_id(0), pl.program_id(1)
    n_out = pl.num_programs(1)
    tf, tk, tn = w1_buf.shape[2], w2_buf.shape[1], w2_buf.shape[2]
    n_up, n_k = a_ref.shape[1] // tf, a_ref.shape[1] // tk
    assert n_up % 2 == 0 and n_k % 2 == 0

    def w1_copy(f, slot):
        return pltpu.make_async_copy(w1_hbm.at[layer, :, pl.ds(f * tf, tf)], w1_buf.at[slot],
                                     sem1.at[slot])

    def w2_copy(n, kh):
        return pltpu.make_async_copy(w2_hbm.at[layer, pl.ds(kh * tk, tk), pl.ds(n * tn, tn)],
                                     w2_buf.at[kh % 2], sem2.at[kh % 2])

    def x_copy(row):
        tm = x_buf.shape[0]
        return pltpu.make_async_copy(x_hbm.at[pl.ds(row * tm, tm), :], x_buf, semx.at[0])

    @pl.when(t == 0)
    def _():
        @pl.when(i == 0)
        def _():
            x_copy(0).start()
            w1_copy(0, 0).start()

        w2_copy(0, 0).start()
        w2_copy(0, 1).start()

        def up_tile(f, slot, h):
            a = jnp.maximum(_dot(h, w1_buf[slot].astype(BF16)), 0.0)
            cols = pl.ds(f * tf, tf) if isinstance(f, int) else pl.ds(pl.multiple_of(f * tf, tf), tf)
            a_ref[:, cols] = (a * a).astype(BF16)

        x_copy(i).wait()
        w1_copy(0, 0).wait()
        w1_copy(1, 1).start()
        h = _rms_norm(x_buf[...], g_ref[...]).astype(BF16)
        h_ref[...] = h
        up_tile(0, 0, h)

        @pl.when(i + 1 < n_rows)
        def _():
            x_copy(i + 1).start()

        def up_pair(p, _):
            for f, slot in ((2 * p + 1, 1), (2 * p + 2, 0)):
                w1_copy(f, slot).wait()
                w1_copy(f + 1, 1 - slot).start()
                up_tile(f, slot, h_ref[...])
            return 0

        lax.fori_loop(0, (n_up - 2) // 2, up_pair, 0)
        w1_copy(n_up - 1, 1).wait()
        up_tile(n_up - 1, 1, h_ref[...])

    def down_tile(n):
        for kh in range(n_k):
            w2_copy(n, kh).wait()
            part = _dot(a_ref[:, kh * tk:(kh + 1) * tk], w2_buf[kh % 2].astype(BF16))
            if kh == 0:
                o_ref[...] = xr_ref[...] + part
            else:
                o_ref[...] += part
            if kh + 2 < n_k:
                w2_copy(n, kh + 2).start()
            else:
                @pl.when(n + 1 < n_out)
                def _(kh=kh):
                    w2_copy(n + 1, kh + 2 - n_k).start()

        @pl.when((n + 1 == n_out) & (i + 1 < n_rows))
        def _():
            w1_copy(0, 0).start()

    down_tile(t)


def _mlp(x, gain, w1, w2, layer):
    s, d = x.shape
    dff = w1.shape[2]
    tm, tf, tn, tk = min(ROW_TILE, s), FF_TILE, MLP_OUT_TILE, MLP_K_TILE
    n_rows, n_out = s // tm, d // tn
    return pl.pallas_call(
        functools.partial(_mlp_kernel, layer=layer, n_rows=n_rows),
        grid=(n_rows, n_out),
        in_specs=[
            pl.BlockSpec(memory_space=pl.ANY),
            pl.BlockSpec((1, d), lambda i, t: (0, 0)),
            pl.BlockSpec(memory_space=pl.ANY),
            pl.BlockSpec(memory_space=pl.ANY),
            pl.BlockSpec((tm, tn), lambda i, t: (i, t)),
        ],
        out_specs=pl.BlockSpec((tm, tn), lambda i, t: (i, t)),
        out_shape=jax.ShapeDtypeStruct((s, d), F32),
        scratch_shapes=[
            pltpu.VMEM((tm, d), F32),
            pltpu.VMEM((tm, d), BF16),
            pltpu.VMEM((tm, dff), BF16),
            pltpu.VMEM((2, d, tf), F32),
            pltpu.VMEM((2, tk, tn), F32),
            pltpu.SemaphoreType.DMA((1,)),
            pltpu.SemaphoreType.DMA((2,)),
            pltpu.SemaphoreType.DMA((2,)),
        ],
        compiler_params=_params("arbitrary", "arbitrary"),
        name="mlp",
    )(x, gain, w1, w2, x)


def _proj_res_kernel(a_ref, w_ref, x_ref, o_ref, wb_ref):
    @pl.when(pl.program_id(1) == 0)
    def _():
        wb_ref[...] = w_ref[...].astype(BF16)

    o_ref[...] = x_ref[...] + _dot(a_ref[...], wb_ref[...])


def _proj_res(a, w, x, layer):
    s, k = a.shape
    n = w.shape[2]
    tm, tn = min(PROJ_ROW_TILE, s), min(PROJ_COL_TILE, n)
    return pl.pallas_call(
        _proj_res_kernel,
        grid=(n // tn, s // tm),
        in_specs=[
            pl.BlockSpec((tm, k), lambda j, i: (i, 0)),
            pl.BlockSpec((None, k, tn), lambda j, i: (layer, 0, j), pipeline_mode=pl.Buffered(1)),
            pl.BlockSpec((tm, tn), lambda j, i: (i, j)),
        ],
        out_specs=pl.BlockSpec((tm, tn), lambda j, i: (i, j)),
        out_shape=jax.ShapeDtypeStruct((s, n), F32),
        scratch_shapes=[pltpu.VMEM((k, tn), BF16)],
        compiler_params=_params("arbitrary", "arbitrary"),
        name="proj_res",
    )(a, w, x)


def _glu_res_kernel(a_ref, wv_ref, wg_ref, x_ref, o_ref, wvb_ref, wgb_ref):
    @pl.when(pl.program_id(1) == 0)
    def _():
        wvb_ref[...] = wv_ref[...].astype(BF16)
        wgb_ref[...] = wg_ref[...].astype(BF16)

    a = a_ref[...]
    val = _dot(a, wvb_ref[...])
    gate = _dot(a, wgb_ref[...])
    o_ref[...] = x_ref[...] + val * jax.nn.sigmoid(gate)


def _glu_res(a, w_glu, x, layer):
    s, k = a.shape
    n = w_glu.shape[2] // 2
    tm, tn = min(PROJ_ROW_TILE, s), min(GLU_COL_TILE, n)
    nj = n // tn
    return pl.pallas_call(
        _glu_res_kernel,
        grid=(nj, s // tm),
        in_specs=[
            pl.BlockSpec((tm, k), lambda j, i: (i, 0)),
            pl.BlockSpec((None, k, tn), lambda j, i: (layer, 0, j), pipeline_mode=pl.Buffered(1)),
            pl.BlockSpec((None, k, tn), lambda j, i: (layer, 0, j + nj), pipeline_mode=pl.Buffered(1)),
            pl.BlockSpec((tm, tn), lambda j, i: (i, j)),
        ],
        out_specs=pl.BlockSpec((tm, tn), lambda j, i: (i, j)),
        out_shape=jax.ShapeDtypeStruct((s, n), F32),
        scratch_shapes=[pltpu.VMEM((k, tn), BF16), pltpu.VMEM((k, tn), BF16)],
        compiler_params=_params("arbitrary", "arbitrary"),
        name="glu_res",
    )(a, w_glu, w_glu, x)


def _conv_front_kernel(x_ref, g_ref, wb_ref, wc_ref, wv_ref, cw_ref, o_ref, h_ref, ext_ref, carry_ref):
    i, j = pl.program_id(0), pl.program_id(1)
    tm = o_ref.shape[0]

    @pl.when(i == 0)
    def _():
        carry_ref[j] = jnp.zeros(carry_ref.shape[1:], F32)

    def column_tile(h):
        u = _wdot(h, wc_ref) * _wdot(h, wv_ref)
        ext_ref[0:HALO, :] = carry_ref[j]
        ext_ref[HALO:, :] = u
        carry_ref[j] = u[tm - HALO:, :]
        conv = cw_ref[CONV_WIDTH - 1:CONV_WIDTH, :] * u
        for lag in range(1, CONV_WIDTH):
            tap = CONV_WIDTH - 1 - lag
            conv += cw_ref[tap:tap + 1, :] * ext_ref[pl.ds(HALO - lag, tm), :]
        o_ref[...] = (_wdot(h, wb_ref) * conv).astype(BF16)

    @pl.when(j == 0)
    def _():
        h = _rms_norm(x_ref[...], g_ref[...]).astype(BF16)
        h_ref[...] = h
        column_tile(h)

    @pl.when(j > 0)
    def _():
        column_tile(h_ref[...])


def _conv_front(x, gain, w_in, conv_w, layer):
    s, d = x.shape
    tm, tn = min(ROW_TILE, s), CONV_COL_TILE
    nj = d // tn
    return pl.pallas_call(
        _conv_front_kernel,
        grid=(s // tm, nj),
        in_specs=[
            pl.BlockSpec((tm, d), lambda i, j: (i, 0)),
            pl.BlockSpec((1, d), lambda i, j: (0, 0)),
            pl.BlockSpec((None, d, tn), lambda i, j: (layer, 0, j)),
            pl.BlockSpec((None, d, tn), lambda i, j: (layer, 0, j + nj)),
            pl.BlockSpec((None, d, tn), lambda i, j: (layer, 0, j + 2 * nj)),
            pl.BlockSpec((CONV_WIDTH, tn), lambda i, j: (0, j)),
        ],
        out_specs=pl.BlockSpec((tm, tn), lambda i, j: (i, j)),
        out_shape=jax.ShapeDtypeStruct((s, d), BF16),
        scratch_shapes=[
            pltpu.VMEM((tm, d), BF16),
            pltpu.VMEM((tm + HALO, tn), F32),
            pltpu.VMEM((nj, HALO, tn), F32),
        ],
        compiler_params=_params("arbitrary", "arbitrary"),
        name="conv_front",
    )(x, gain, w_in, w_in, w_in, conv_w)


def _round_kernel(w_ref, o_ref):
    o_ref[...] = w_ref[...].astype(BF16)


def _round_bf16(w, layer, block_rows):
    _, r, c = w.shape
    return pl.pallas_call(
        _round_kernel,
        grid=(r // block_rows,),
        in_specs=[pl.BlockSpec((None, block_rows, c), lambda i: (layer, i, 0))],
        out_specs=pl.BlockSpec((block_rows, c), lambda i: (i, 0)),
        out_shape=jax.ShapeDtypeStruct((r, c), BF16),
        compiler_params=_params("parallel"),
        name="round_bf16",
    )(w)


def _pool_kernel(x_ref, g_ref, win_ref, wg_ref, sc_ref, o_ref, ext_ref, la_ref, lb_ref, carry_ref):
    i = pl.program_id(0)
    tm = o_ref.shape[0]
    halo = POOL_HALO
    n_ext = tm + halo
    pg = wg_ref.shape[1]

    @pl.when(i == 0)
    def _():
        carry_ref[...] = jnp.zeros(carry_ref.shape, F32)

    x = x_ref[...]
    h = _rms_norm(x, g_ref[...]).astype(BF16)
    pos = (i * tm + 1 + lax.broadcasted_iota(jnp.int32, (tm, 1), 0)).astype(F32)

    for gi, w in enumerate(POOL_WINDOWS):
        cols = slice(gi * pg, (gi + 1) * pg)
        u = _dot(h, win_ref[:, cols])
        ext_ref[gi, 0:halo, :] = carry_ref[gi]
        ext_ref[gi, halo:, :] = u
        carry_ref[gi] = u[tm - halo:, :]
        levels = int(math.log2(w))
        assert 2 ** levels == w and SUBLANES * levels <= halo
        src = ext_ref
        for lv in range(1, levels):
            dst = la_ref if lv % 2 else lb_ref
            lo, shift = SUBLANES * lv, 2 ** (lv - 1)
            dst[gi, lo:, :] = src[gi, lo:, :] + src[gi, pl.ds(lo - shift, n_ext - lo), :]
            src = dst
        acc = src[gi, halo:, :] + src[gi, pl.ds(halo - w // 2, tm), :]
        inv_count = 1.0 / jnp.minimum(pos, float(w))
        pooled = (acc * inv_count - u).astype(BF16)
        o_ref[:, cols] = x[:, cols] + _dot(pooled, wg_ref[gi]) * sc_ref[:, cols]


def _pool_mixer(x, gain, w_in, w_group, scale, layer):
    s, d = x.shape
    ng, pg = w_group.shape[1], w_group.shape[2]
    tm = min(POOL_ROW_TILE, s)
    win_b = _round_bf16(w_in, layer, d // ng)
    wg_b = _round_bf16(w_group.reshape(w_group.shape[0], ng * pg, pg), layer, pg).reshape(ng, pg, pg)
    return pl.pallas_call(
        _pool_kernel,
        grid=(s // tm,),
        in_specs=[
            pl.BlockSpec((tm, d), lambda i: (i, 0)),
            pl.BlockSpec((1, d), lambda i: (0, 0)),
            pl.BlockSpec((d, d), lambda i: (0, 0)),
            pl.BlockSpec((ng, pg, pg), lambda i: (0, 0, 0)),
            pl.BlockSpec((1, d), lambda i: (0, 0)),
        ],
        out_specs=pl.BlockSpec((tm, d), lambda i: (i, 0)),
        out_shape=jax.ShapeDtypeStruct((s, d), F32),
        scratch_shapes=[
            pltpu.VMEM((ng, tm + POOL_HALO, pg), F32),
            pltpu.VMEM((ng, tm + POOL_HALO, pg), F32),
            pltpu.VMEM((ng, tm + POOL_HALO, pg), F32),
            pltpu.VMEM((ng, POOL_HALO, pg), F32),
        ],
        compiler_params=_params("arbitrary"),
        name="pool_mixer",
    )(x, gain, win_b, wg_b, scale)


def _qkv_kernel(x_ref, g_ref, w_ref, hg_ref, o_ref, h_ref, *, n_norm_tiles):
    j = pl.program_id(1)

    def normed_heads(h):
        y = _wdot(h, w_ref)
        for hd in range(y.shape[1] // ATT_HEAD_DIM):
            sl = slice(hd * ATT_HEAD_DIM, (hd + 1) * ATT_HEAD_DIM)
            o_ref[:, sl] = _rms_norm(y[:, sl], hg_ref[:, sl]).astype(BF16)

    @pl.when(j == 0)
    def _():
        h = _rms_norm(x_ref[...], g_ref[...]).astype(BF16)
        h_ref[...] = h
        normed_heads(h)

    @pl.when((j > 0) & (j < n_norm_tiles))
    def _():
        normed_heads(h_ref[...])

    @pl.when(j >= n_norm_tiles)
    def _():
        o_ref[...] = _wdot(h_ref[...], w_ref).astype(BF16)


def _qkv_proj(x, gain, w_qkv, head_gains, layer):
    s, d = x.shape
    n = w_qkv.shape[2]
    tm, tn = min(ROW_TILE, s), QKV_COL_TILE
    n_norm_tiles = head_gains.shape[1] // tn
    return pl.pallas_call(
        functools.partial(_qkv_kernel, n_norm_tiles=n_norm_tiles),
        grid=(s // tm, n // tn),
        in_specs=[
            pl.BlockSpec((tm, d), lambda i, j: (i, 0)),
            pl.BlockSpec((1, d), lambda i, j: (0, 0)),
            pl.BlockSpec((None, d, tn), lambda i, j: (layer, 0, j)),
            pl.BlockSpec((1, tn), lambda i, j: (0, jnp.minimum(j, n_norm_tiles - 1))),
        ],
        out_specs=pl.BlockSpec((tm, tn), lambda i, j: (i, j)),
        out_shape=jax.ShapeDtypeStruct((s, n), BF16),
        scratch_shapes=[pltpu.VMEM((tm, d), BF16)],
        compiler_params=_params("parallel", "arbitrary"),
        name="qkv_proj",
    )(x, gain, w_qkv, head_gains)


def _attn_kernel(q_ref, k0_ref, k1_ref, k2_ref, v0_ref, v1_ref, v2_ref, prof_ref, o_ref, bias_ref):
    b = pl.program_id(1)
    qb = q_ref.shape[0]
    n_local = q_ref.shape[1] // ATT_HEAD_DIM
    log2e = math.log2(math.e)
    scale = ATT_HEAD_DIM ** -0.5 * log2e

    @pl.when(b == 0)
    def _():
        q_idx = lax.broadcasted_iota(jnp.int32, (qb, 3 * qb), 0) + 2 * qb
        k_idx = lax.broadcasted_iota(jnp.int32, (qb, 3 * qb), 1)
        chunk_start = (q_idx // CHUNK) * CHUNK
        in_band = (k_idx >= chunk_start - ATT_LEFT_CHUNKS * CHUNK) & (k_idx < chunk_start + CHUNK)
        for hh in range(n_local):
            profile = jnp.broadcast_to(prof_ref[hh], (qb, prof_ref.shape[2]))
            table = pltpu.roll(profile, 0, 1, stride=1, stride_axis=0)[:, :3 * qb]
            bias_ref[hh] = jnp.where(in_band, table * log2e, MASK_VALUE)

    def lane_slabs(a):
        return [a[:, k * LANES:(k + 1) * LANES] for k in range(a.shape[1] // LANES)]

    def heads(first_blocks):
        for hh in range(n_local):
            cols = slice(hh * ATT_HEAD_DIM, (hh + 1) * ATT_HEAD_DIM)
            q = q_ref[:, cols]
            scores = []
            for c, k_ref in enumerate((k0_ref, k1_ref, k2_ref)):
                s_c = lax.dot_general(q, k_ref[:, cols], (((1,), (1,)), ((), ())),
                                      preferred_element_type=F32)
                s_c = s_c * scale + bias_ref[hh, :, c * qb:(c + 1) * qb]
                if first_blocks and c < 2:
                    s_c = jnp.where(b - 2 + c >= 0, s_c, MASK_VALUE)
                scores.append(s_c)
            m = jnp.max(functools.reduce(jnp.maximum,
                                         [sl for s_c in scores for sl in lane_slabs(s_c)]),
                        axis=-1, keepdims=True)
            acc = jnp.zeros((qb, 2 * ATT_HEAD_DIM), F32)
            ones = jnp.ones((qb, ATT_HEAD_DIM), BF16)
            for s_c, v_ref in zip(scores, (v0_ref, v1_ref, v2_ref)):
                p = jnp.exp2((s_c - m).astype(BF16))
                acc += _dot(p, jnp.concatenate([v_ref[:, cols], ones], axis=1))
            o_ref[:, cols] = (acc[:, :ATT_HEAD_DIM] / acc[:, ATT_HEAD_DIM:]).astype(BF16)

    @pl.when(b < 2)
    def _():
        heads(True)

    @pl.when(b >= 2)
    def _():
        heads(False)


def _attention(qkv, profile, n_heads):
    s = qkv.shape[0]
    qb = ATT_QBLOCK
    nb = s // qb
    hps = ATT_HEADS_PER_STEP
    width = hps * ATT_HEAD_DIM
    ngroups = n_heads // hps

    def kv_spec(offset, back):
        return pl.BlockSpec((qb, width), lambda h, b: (jnp.maximum(b - back, 0), offset + h))

    return pl.pallas_call(
        _attn_kernel,
        grid=(ngroups, nb),
        in_specs=[
            pl.BlockSpec((qb, width), lambda h, b: (b, h)),
            kv_spec(ngroups, 2), kv_spec(ngroups, 1), kv_spec(ngroups, 0),
            kv_spec(2 * ngroups, 2), kv_spec(2 * ngroups, 1), kv_spec(2 * ngroups, 0),
            pl.BlockSpec((hps, 1, 4 * qb), lambda h, b: (h, 0, 0)),
        ],
        out_specs=pl.BlockSpec((qb, width), lambda h, b: (b, h)),
        out_shape=jax.ShapeDtypeStruct((s, n_heads * ATT_HEAD_DIM), BF16),
        scratch_shapes=[pltpu.VMEM((hps, qb, 3 * qb), F32)],
        compiler_params=_params("arbitrary", "arbitrary"),
        name="chunk_attention",
    )(qkv, qkv, qkv, qkv, qkv, qkv, qkv, profile)


def _attention_profile(rel_bias):
    qb = ATT_QBLOCK
    n_heads = rel_bias.shape[0]
    assert REL_CLIP <= 2 * qb and qb <= REL_CLIP + 1 and 2 * REL_CLIP + 1 == rel_bias.shape[1]
    n_flat = 2 * qb - REL_CLIP + 1
    top = rel_bias[:, 2 * REL_CLIP:]
    profile = jnp.concatenate([
        jnp.broadcast_to(top, (n_heads, n_flat)),
        jnp.flip(rel_bias[:, REL_CLIP - qb + 1:2 * REL_CLIP], axis=1),
        jnp.broadcast_to(top, (n_heads, qb)),
    ], axis=1)
    return profile.reshape(n_heads, 1, 4 * qb).astype(F32)


def _ssm_prep_kernel(are_ref, aim_ref, ldt_ref, btr_ref, bti_ref, abr_ref, abi_ref, bbr_ref, bbi_ref):
    lam_r, lam_i = are_ref[...], aim_ref[...]
    dt = jnp.exp(ldt_ref[...])
    mag = jnp.exp(lam_r * dt)
    ab_r, ab_i = mag * jnp.cos(lam_i * dt), mag * jnp.sin(lam_i * dt)
    abr_ref[...] = ab_r
    abi_ref[...] = ab_i
    num_r, num_i = ab_r - 1.0, ab_i
    den = lam_r * lam_r + lam_i * lam_i
    co_r = (num_r * lam_r + num_i * lam_i) / den
    co_i = (num_i * lam_r - num_r * lam_i) / den
    for c in range(btr_ref.shape[0]):
        b_r, b_i = btr_ref[c], bti_ref[c]
        bbr_ref[c] = co_r * b_r - co_i * b_i
        bbi_ref[c] = co_r * b_i + co_i * b_r


def _ssm_prep(a_re, a_im, log_dt, bt_re, bt_im):
    g, n = a_re.shape
    gn = jax.ShapeDtypeStruct((g, n), F32)
    cgn = jax.ShapeDtypeStruct(bt_re.shape, F32)
    return pl.pallas_call(
        _ssm_prep_kernel,
        out_shape=(gn, gn, cgn, cgn),
        name="ssm_prep",
    )(a_re, a_im, log_dt, bt_re, bt_im)


def _ssm_kernel(x_ref, g_ref, d_ref, wbr_ref, wbi_ref, wcr_ref, wci_ref, ar_ref, ai_ref, o_ref,
                hn_ref, y_ref, xr_ref, xi_ref, cr_ref, ci_ref):
    i = pl.program_id(0)
    t = x_ref.shape[0]
    nblk, cb, ns = wbr_ref.shape
    n_batches, ppb = xr_ref.shape[0], xr_ref.shape[1]
    nsub = ns // LANES
    assert cb == LANES and 2 * nsub == SUBLANES and n_batches * ppb * 2 == nblk

    h = _rms_norm(x_ref[...], g_ref[...])
    for c in range(nblk):
        hn_ref[c] = h[:, c * LANES:(c + 1) * LANES]

    @pl.when(i == 0)
    def _():
        cr_ref[...] = jnp.zeros(cr_ref.shape, F32)
        ci_ref[...] = jnp.zeros(ci_ref.shape, F32)

    def slab_rows(half, k):
        return pl.ds(half * nsub + k, t, stride=SUBLANES)

    def blocks_of(batch):
        for g8 in range(2 * ppb):
            yield batch * 2 * ppb + g8, g8 // 2, g8 % 2

    def run(*stages):
        stages = list(stages)
        while stages:
            for stage in list(stages):
                if next(stage, "done") == "done":
                    stages.remove(stage)

    def project_in(batch):
        for gb, pair, half in blocks_of(batch):
            hb = hn_ref[gb].astype(BF16)
            for w_ref, dst_ref in ((wbr_ref, xr_ref), (wbi_ref, xi_ref)):
                bu = _dot(hb, w_ref[gb])
                for k in range(nsub):
                    dst_ref[batch, pair, slab_rows(half, k), :] = bu[:, k * LANES:(k + 1) * LANES]
            yield

    def scan(batch):
        coef = [(ar_ref[batch * ppb + p], ai_ref[batch * ppb + p]) for p in range(ppb)]
        steps_per_turn = t // (2 * ppb)

        def step(tt, carry):
            rows = pl.ds(tt * SUBLANES, SUBLANES)
            out = []
            for p in range(ppb):
                (a_r, a_i), x_r, x_i = coef[p], carry[2 * p], carry[2 * p + 1]
                n_r = a_r * x_r - a_i * x_i + xr_ref[batch, p, rows, :]
                n_i = a_r * x_i + a_i * x_r + xi_ref[batch, p, rows, :]
                xr_ref[batch, p, rows, :] = n_r
                xi_ref[batch, p, rows, :] = n_i
                out += [n_r, n_i]
            return tuple(out)

        last = tuple(ref[batch * ppb + p] for p in range(ppb) for ref in (cr_ref, ci_ref))
        for tt in range(t):
            last = step(tt, last)
            if (tt + 1) % steps_per_turn == 0 and tt + 1 < t:
                yield
        for p in range(ppb):
            cr_ref[batch * ppb + p] = last[2 * p]
            ci_ref[batch * ppb + p] = last[2 * p + 1]
        yield

    def project_out(batch):
        for gb, pair, half in blocks_of(batch):
            s_r, s_i = (jnp.concatenate([ref[batch, pair, slab_rows(half, k), :] for k in range(nsub)],
                                        axis=1).astype(BF16) for ref in (xr_ref, xi_ref))
            y_ref[gb] = _dot(s_r, wcr_ref[gb]) + _dot(s_i, wci_ref[gb])
            yield

    def finish(batch):
        for gb, _, _ in blocks_of(batch):
            lanes = slice(gb * LANES, (gb + 1) * LANES)
            o_ref[:, lanes] = jax.nn.gelu(y_ref[gb] + d_ref[:, lanes] * hn_ref[gb]).astype(BF16)
            yield

    run(project_in(0))
    for batch in range(n_batches):
        beside = [project_in(batch + 1)] if batch + 1 < n_batches else []
        beside += [project_out(batch - 1)] if batch > 0 else []
        beside += [finish(batch - 2)] if batch > 1 else []
        run(scan(batch), *beside)
    run(project_out(n_batches - 1), *([finish(n_batches - 2)] if n_batches > 1 else []))
    run(finish(n_batches - 1))


def _ssm_core(x, gain, d_skip, wb_r, wb_i, wc_r, wc_i, ab_r, ab_i):
    s, d = x.shape
    t = min(SSM_ROW_TILE, s)
    nblk, cb, ns = wb_r.shape
    ppb = SSM_PAIRS_PER_BATCH
    n_batches = nblk // (2 * ppb)

    def whole(a):
        return pl.BlockSpec(a.shape, lambda i: (0,) * a.ndim)

    return pl.pallas_call(
        _ssm_kernel,
        grid=(s // t,),
        in_specs=[
            pl.BlockSpec((t, d), lambda i: (i, 0)),
            whole(gain), whole(d_skip),
            whole(wb_r), whole(wb_i), whole(wc_r), whole(wc_i),
            whole(ab_r), whole(ab_i),
        ],
        out_specs=pl.BlockSpec((t, d), lambda i: (i, 0)),
        out_shape=jax.ShapeDtypeStruct((s, d), BF16),
        scratch_shapes=[
            pltpu.VMEM((nblk, t, LANES), F32),
            pltpu.VMEM((nblk, t, LANES), F32),
            pltpu.VMEM((n_batches, ppb, t * SUBLANES, LANES), F32),
            pltpu.VMEM((n_batches, ppb, t * SUBLANES, LANES), F32),
            pltpu.VMEM((nblk // 2, SUBLANES, LANES), F32),
            pltpu.VMEM((nblk // 2, SUBLANES, LANES), F32),
        ],
        compiler_params=_params("arbitrary"),
        name="ssm_core",
    )(x, gain, d_skip, wb_r, wb_i, wc_r, wc_i, ab_r, ab_i)


def _block_diag(w, groups_per_block):
    g, r, c = w.shape
    nblk = g // groups_per_block
    tiled = jnp.tile(w.reshape(nblk, groups_per_block * r, c), (1, 1, groups_per_block))
    row_group = lax.broadcasted_iota(jnp.int32, tiled.shape, 1) // r
    col_group = lax.broadcasted_iota(jnp.int32, tiled.shape, 2) // c
    return jnp.where(row_group == col_group, tiled, jnp.zeros_like(tiled))


def _s5_mixer(x, gain, a_re, a_im, log_dt, b_re, b_im, c_re, c_im, d_skip, w_glu, layer):
    s, d = x.shape
    g, n = a_re.shape
    gpb = SSM_GROUPS_PER_BLOCK
    bt_re, bt_im = b_re.transpose(2, 0, 1), b_im.transpose(2, 0, 1)
    ab_r, ab_i, bb_r, bb_i = _ssm_prep(a_re, a_im, log_dt.reshape(g, 1), bt_re, bt_im)
    wb_r = _block_diag(bb_r.transpose(1, 0, 2), gpb).astype(BF16)
    wb_i = _block_diag(bb_i.transpose(1, 0, 2), gpb).astype(BF16)
    wc_r = _block_diag(c_re.transpose(0, 2, 1), gpb).astype(BF16)
    wc_i = _block_diag(-c_im.transpose(0, 2, 1), gpb).astype(BF16)
    slabs = lambda a: a.reshape(g // (2 * gpb), 2 * gpb * n // LANES, LANES)
    z = _ssm_core(x, gain, d_skip.reshape(1, d), wb_r, wb_i, wc_r, wc_i, slabs(ab_r), slabs(ab_i))
    return _glu_res(z, w_glu, x, layer)


def kernel(x, norm_mix, norm_mlp, mlp_w1, mlp_w2, conv_w_in, conv_w, conv_w_out, pool_w_in, pool_w_group, pool_scale, att_w_qkv, att_q_norm, att_k_norm, att_rel_bias, att_w_out, ssm_a_re, ssm_a_im, ssm_log_dt, ssm_b_re, ssm_b_im, ssm_c_re, ssm_c_im, ssm_d, ssm_w_glu):
    b, s, d = x.shape
    depth = norm_mix.shape[0]
    n_mixers = 4
    outs = []
    for bi in range(b):
        xs = x[bi]
        for i in range(depth):
            kind, j = i % n_mixers, i // n_mixers
            gain = norm_mix[i].reshape(1, d)
            if kind == 0:
                gated = _conv_front(xs, gain, conv_w_in, conv_w[j], j)
                xs = _proj_res(gated, conv_w_out, xs, j)
            elif kind == 1:
                xs = _pool_mixer(xs, gain, pool_w_in, pool_w_group, pool_scale[j].reshape(1, d), j)
            elif kind == 2:
                n_heads = d // ATT_HEAD_DIM
                head_gains = jnp.concatenate([jnp.tile(att_q_norm[j], n_heads),
                                              jnp.tile(att_k_norm[j], n_heads)]).reshape(1, 2 * d)
                qkv = _qkv_proj(xs, gain, att_w_qkv, head_gains, j)
                att = _attention(qkv, _attention_profile(att_rel_bias[j]), n_heads)
                xs = _proj_res(att, att_w_out, xs, j)
            else:
                xs = _s5_mixer(xs, gain, ssm_a_re[j], ssm_a_im[j], ssm_log_dt[j], ssm_b_re[j],
                               ssm_b_im[j], ssm_c_re[j], ssm_c_im[j], ssm_d[j], ssm_w_glu, j)
            xs = _mlp(xs, norm_mlp[i].reshape(1, d), mlp_w1, mlp_w2, i)
        outs.append(xs)
    return outs[0][None] if b == 1 else jnp.stack(outs)
```

```python
import functools
import math

import jax
import jax.numpy as jnp
from jax import lax
from jax.experimental import pallas as pl
from jax.experimental.pallas import tpu as pltpu

F32 = jnp.float32
BF16 = jnp.bfloat16

RMS_EPS = 1e-6
CHUNK = 64
ATT_HEAD_DIM = 128
ATT_LEFT_CHUNKS = 8
REL_CLIP = 256
MASK_VALUE = -1e30
POOL_WINDOWS = (2, 4, 8, 16)
CONV_WIDTH = 3

SUBLANES = 8
LANES = 128
VMEM_LIMIT_BYTES = 56 * 1024 * 1024

ROW_TILE = 1024
CONV_COL_TILE = 512
QKV_COL_TILE = 1024
QKV_WEIGHT_SLOTS = 3
PROJ_ROW_TILE = 512
PROJ_COL_TILE = 2048
GLU_COL_TILE = 1024
FF_TILE = 512
MLP_OUT_TILE = 512
MLP_K_TILE = 2048
ATT_QBLOCK = 256
ATT_HEADS_PER_STEP = 16
SSM_ROW_TILE = 256
SSM_PAIRS_PER_BATCH = 4
SSM_GROUPS_PER_BLOCK = 8
HALO = 16
POOL_HALO = 32
POOL_ROW_TILE = 512


def _params(*semantics):
    return pltpu.CompilerParams(dimension_semantics=semantics, vmem_limit_bytes=VMEM_LIMIT_BYTES)


def _rms_norm(x, gain):
    return x * lax.rsqrt(jnp.mean(x * x, axis=-1, keepdims=True) + RMS_EPS) * gain


def _dot(a, b):
    return jnp.dot(a, b, preferred_element_type=F32)


def _wdot(a, w_ref):
    return _dot(a, w_ref[...].astype(BF16))


def _mlp_kernel(x_hbm, g_ref, w1_hbm, w2_hbm, xr_ref, o_ref, x_buf, h_ref, a_ref, w1_buf, w2_buf,
                semx, sem1, sem2, *, layer, n_rows):
    i, t = pl.program_id(0), pl.program_id(1)
    n_out = pl.num_programs(1) - 1
    tf, tk, tn = w1_buf.shape[2], w2_buf.shape[1], w2_buf.shape[2]
    n_up, n_k = a_ref.shape[1] // tf, a_ref.shape[1] // tk
    assert n_up % 2 == 0 and n_k % 2 == 0

    def w1_copy(f, slot):
        return pltpu.make_async_copy(w1_hbm.at[layer, :, pl.ds(f * tf, tf)], w1_buf.at[slot],
                                     sem1.at[slot])

    def w2_copy(n, kh):
        return pltpu.make_async_copy(w2_hbm.at[layer, pl.ds(kh * tk, tk), pl.ds(n * tn, tn)],
                                     w2_buf.at[kh % 2], sem2.at[kh % 2])

    def x_copy(row):
        tm = x_buf.shape[0]
        return pltpu.make_async_copy(x_hbm.at[pl.ds(row * tm, tm), :], x_buf, semx.at[0])

    @pl.when(t == 0)
    def _():
        @pl.when(i == 0)
        def _():
            x_copy(0).start()
            w1_copy(0, 0).start()

        w2_copy(0, 0).start()
        w2_copy(0, 1).start()

        def up_tile(f, slot, h):
            a = jnp.maximum(_dot(h, w1_buf[slot].astype(BF16)), 0.0)
            cols = pl.ds(f * tf, tf) if isinstance(f, int) else pl.ds(pl.multiple_of(f * tf, tf), tf)
            a_ref[:, cols] = (a * a).astype(BF16)

        x_copy(i).wait()
        w1_copy(0, 0).wait()
        w1_copy(1, 1).start()
        h = _rms_norm(x_buf[...], g_ref[...]).astype(BF16)
        h_ref[...] = h
        up_tile(0, 0, h)

        @pl.when(i + 1 < n_rows)
        def _():
            x_copy(i + 1).start()

        def up_pair(p, _):
            for f, slot in ((2 * p + 1, 1), (2 * p + 2, 0)):
                w1_copy(f, slot).wait()
                w1_copy(f + 1, 1 - slot).start()
                up_tile(f, slot, h_ref[...])
            return 0

        lax.fori_loop(0, (n_up - 2) // 2, up_pair, 0)
        w1_copy(n_up - 1, 1).wait()
        up_tile(n_up - 1, 1, h_ref[...])

    @pl.when(t > 0)
    def _():
        n = t - 1
        for kh in range(n_k):
            w2_copy(n, kh).wait()
            part = _dot(a_ref[:, kh * tk:(kh + 1) * tk], w2_buf[kh % 2].astype(BF16))
            if kh == 0:
                o_ref[...] = xr_ref[...] + part
            else:
                o_ref[...] += part
            if kh + 2 < n_k:
                w2_copy(n, kh + 2).start()
            else:
                @pl.when(n + 1 < n_out)
                def _(kh=kh):
                    w2_copy(n + 1, kh + 2 - n_k).start()

        @pl.when((n + 1 == n_out) & (i + 1 < n_rows))
        def _():
            w1_copy(0, 0).start()


def _mlp(x, gain, w1, w2, layer):
    s, d = x.shape
    dff = w1.shape[2]
    tm, tf, tn, tk = min(ROW_TILE, s), FF_TILE, MLP_OUT_TILE, MLP_K_TILE
    n_rows, n_out = s // tm, d // tn

    def out_tile(t):
        return jnp.maximum(t - 1, 0)

    return pl.pallas_call(
        functools.partial(_mlp_kernel, layer=layer, n_rows=n_rows),
        grid=(n_rows, 1 + n_out),
        in_specs=[
            pl.BlockSpec(memory_space=pl.ANY),
            pl.BlockSpec((1, d), lambda i, t: (0, 0)),
            pl.BlockSpec(memory_space=pl.ANY),
            pl.BlockSpec(memory_space=pl.ANY),
            pl.BlockSpec((tm, tn), lambda i, t: (i, out_tile(t))),
        ],
        out_specs=pl.BlockSpec((tm, tn), lambda i, t: (i, out_tile(t))),
        out_shape=jax.ShapeDtypeStruct((s, d), F32),
        scratch_shapes=[
            pltpu.VMEM((tm, d), F32),
            pltpu.VMEM((tm, d), BF16),
            pltpu.VMEM((tm, dff), BF16),
            pltpu.VMEM((2, d, tf), F32),
            pltpu.VMEM((2, tk, tn), F32),
            pltpu.SemaphoreType.DMA((1,)),
            pltpu.SemaphoreType.DMA((2,)),
            pltpu.SemaphoreType.DMA((2,)),
        ],
        compiler_params=_params("arbitrary", "arbitrary"),
        name="mlp",
    )(x, gain, w1, w2, x)


def _proj_res_kernel(a_ref, w_ref, x_ref, o_ref, wb_ref):
    @pl.when(pl.program_id(1) == 0)
    def _():
        wb_ref[...] = w_ref[...].astype(BF16)

    o_ref[...] = x_ref[...] + _dot(a_ref[...], wb_ref[...])


def _proj_res(a, w, x, layer):
    s, k = a.shape
    n = w.shape[2]
    tm, tn = min(PROJ_ROW_TILE, s), min(PROJ_COL_TILE, n)
    return pl.pallas_call(
        _proj_res_kernel,
        grid=(n // tn, s // tm),
        in_specs=[
            pl.BlockSpec((tm, k), lambda j, i: (i, 0)),
            pl.BlockSpec((None, k, tn), lambda j, i: (layer, 0, j), pipeline_mode=pl.Buffered(1)),
            pl.BlockSpec((tm, tn), lambda j, i: (i, j)),
        ],
        out_specs=pl.BlockSpec((tm, tn), lambda j, i: (i, j)),
        out_shape=jax.ShapeDtypeStruct((s, n), F32),
        scratch_shapes=[pltpu.VMEM((k, tn), BF16)],
        compiler_params=_params("arbitrary", "arbitrary"),
        name="proj_res",
    )(a, w, x)


def _glu_res_kernel(a_ref, wv_ref, wg_ref, x_ref, o_ref, wvb_ref, wgb_ref):
    @pl.when(pl.program_id(1) == 0)
    def _():
        wvb_ref[...] = wv_ref[...].astype(BF16)
        wgb_ref[...] = wg_ref[...].astype(BF16)

    a = a_ref[...]
    val = _dot(a, wvb_ref[...])
    gate = _dot(a, wgb_ref[...])
    o_ref[...] = x_ref[...] + val * jax.nn.sigmoid(gate)


def _glu_res(a, w_glu, x, layer):
    s, k = a.shape
    n = w_glu.shape[2] // 2
    tm, tn = min(PROJ_ROW_TILE, s), min(GLU_COL_TILE, n)
    nj = n // tn
    return pl.pallas_call(
        _glu_res_kernel,
        grid=(nj, s // tm),
        in_specs=[
            pl.BlockSpec((tm, k), lambda j, i: (i, 0)),
            pl.BlockSpec((None, k, tn), lambda j, i: (layer, 0, j), pipeline_mode=pl.Buffered(1)),
            pl.BlockSpec((None, k, tn), lambda j, i: (layer, 0, j + nj), pipeline_mode=pl.Buffered(1)),
            pl.BlockSpec((tm, tn), lambda j, i: (i, j)),
        ],
        out_specs=pl.BlockSpec((tm, tn), lambda j, i: (i, j)),
        out_shape=jax.ShapeDtypeStruct((s, n), F32),
        scratch_shapes=[pltpu.VMEM((k, tn), BF16), pltpu.VMEM((k, tn), BF16)],
        compiler_params=_params("arbitrary", "arbitrary"),
        name="glu_res",
    )(a, w_glu, w_glu, x)


def _conv_front_kernel(x_ref, g_ref, wb_ref, wc_ref, wv_ref, cw_ref, o_ref, h_ref, ext_ref, carry_ref):
    i, j = pl.program_id(0), pl.program_id(1)
    tm = o_ref.shape[0]

    @pl.when(i == 0)
    def _():
        carry_ref[j] = jnp.zeros(carry_ref.shape[1:], F32)

    def column_tile(h):
        u = _wdot(h, wc_ref) * _wdot(h, wv_ref)
        ext_ref[0:HALO, :] = carry_ref[j]
        ext_ref[HALO:, :] = u
        carry_ref[j] = u[tm - HALO:, :]
        conv = cw_ref[CONV_WIDTH - 1:CONV_WIDTH, :] * u
        for lag in range(1, CONV_WIDTH):
            tap = CONV_WIDTH - 1 - lag
            conv += cw_ref[tap:tap + 1, :] * ext_ref[pl.ds(HALO - lag, tm), :]
        o_ref[...] = (_wdot(h, wb_ref) * conv).astype(BF16)

    @pl.when(j == 0)
    def _():
        h = _rms_norm(x_ref[...], g_ref[...]).astype(BF16)
        h_ref[...] = h
        column_tile(h)

    @pl.when(j > 0)
    def _():
        column_tile(h_ref[...])


def _conv_front(x, gain, w_in, conv_w, layer):
    s, d = x.shape
    tm, tn = min(ROW_TILE, s), CONV_COL_TILE
    nj = d // tn
    return pl.pallas_call(
        _conv_front_kernel,
        grid=(s // tm, nj),
        in_specs=[
            pl.BlockSpec((tm, d), lambda i, j: (i, 0)),
            pl.BlockSpec((1, d), lambda i, j: (0, 0)),
            pl.BlockSpec((None, d, tn), lambda i, j: (layer, 0, j)),
            pl.BlockSpec((None, d, tn), lambda i, j: (layer, 0, j + nj)),
            pl.BlockSpec((None, d, tn), lambda i, j: (layer, 0, j + 2 * nj)),
            pl.BlockSpec((CONV_WIDTH, tn), lambda i, j: (0, j)),
        ],
        out_specs=pl.BlockSpec((tm, tn), lambda i, j: (i, j)),
        out_shape=jax.ShapeDtypeStruct((s, d), BF16),
        scratch_shapes=[
            pltpu.VMEM((tm, d), BF16),
            pltpu.VMEM((tm + HALO, tn), F32),
            pltpu.VMEM((nj, HALO, tn), F32),
        ],
        compiler_params=_params("arbitrary", "arbitrary"),
        name="conv_front",
    )(x, gain, w_in, w_in, w_in, conv_w)


def _round_kernel(w_ref, o_ref):
    o_ref[...] = w_ref[...].astype(BF16)


def _round_bf16(w, layer, block_rows):
    _, r, c = w.shape
    return pl.pallas_call(
        _round_kernel,
        grid=(r // block_rows,),
        in_specs=[pl.BlockSpec((None, block_rows, c), lambda i: (layer, i, 0))],
        out_specs=pl.BlockSpec((block_rows, c), lambda i: (i, 0)),
        out_shape=jax.ShapeDtypeStruct((r, c), BF16),
        compiler_params=_params("parallel"),
        name="round_bf16",
    )(w)


def _pool_kernel(x_ref, g_ref, win_ref, wg_ref, sc_ref, o_ref, ext_ref, la_ref, lb_ref, carry_ref):
    i = pl.program_id(0)
    tm = o_ref.shape[0]
    halo = POOL_HALO
    n_ext = tm + halo
    pg = wg_ref.shape[1]

    @pl.when(i == 0)
    def _():
        carry_ref[...] = jnp.zeros(carry_ref.shape, F32)

    x = x_ref[...]
    h = _rms_norm(x, g_ref[...]).astype(BF16)
    pos = (i * tm + 1 + lax.broadcasted_iota(jnp.int32, (tm, 1), 0)).astype(F32)

    for gi, w in enumerate(POOL_WINDOWS):
        cols = slice(gi * pg, (gi + 1) * pg)
        u = _dot(h, win_ref[:, cols])
        ext_ref[gi, 0:halo, :] = carry_ref[gi]
        ext_ref[gi, halo:, :] = u
        carry_ref[gi] = u[tm - halo:, :]
        levels = int(math.log2(w))
        assert 2 ** levels == w and SUBLANES * levels <= halo
        src = ext_ref
        for lv in range(1, levels):
            dst = la_ref if lv % 2 else lb_ref
            lo, shift = SUBLANES * lv, 2 ** (lv - 1)
            dst[gi, lo:, :] = src[gi, lo:, :] + src[gi, pl.ds(lo - shift, n_ext - lo), :]
            src = dst
        acc = src[gi, halo:, :] + src[gi, pl.ds(halo - w // 2, tm), :]
        inv_count = 1.0 / jnp.minimum(pos, float(w))
        pooled = (acc * inv_count - u).astype(BF16)
        o_ref[:, cols] = x[:, cols] + _dot(pooled, wg_ref[gi]) * sc_ref[:, cols]


def _pool_mixer(x, gain, w_in, w_group, scale, layer):
    s, d = x.shape
    ng, pg = w_group.shape[1], w_group.shape[2]
    tm = min(POOL_ROW_TILE, s)
    win_b = _round_bf16(w_in, layer, d // ng)
    wg_b = _round_bf16(w_group.reshape(w_group.shape[0], ng * pg, pg), layer, pg).reshape(ng, pg, pg)
    return pl.pallas_call(
        _pool_kernel,
        grid=(s // tm,),
        in_specs=[
            pl.BlockSpec((tm, d), lambda i: (i, 0)),
            pl.BlockSpec((1, d), lambda i: (0, 0)),
            pl.BlockSpec((d, d), lambda i: (0, 0)),
            pl.BlockSpec((ng, pg, pg), lambda i: (0, 0, 0)),
            pl.BlockSpec((1, d), lambda i: (0, 0)),
        ],
        out_specs=pl.BlockSpec((tm, d), lambda i: (i, 0)),
        out_shape=jax.ShapeDtypeStruct((s, d), F32),
        scratch_shapes=[
            pltpu.VMEM((ng, tm + POOL_HALO, pg), F32),
            pltpu.VMEM((ng, tm + POOL_HALO, pg), F32),
            pltpu.VMEM((ng, tm + POOL_HALO, pg), F32),
            pltpu.VMEM((ng, POOL_HALO, pg), F32),
        ],
        compiler_params=_params("arbitrary"),
        name="pool_mixer",
    )(x, gain, win_b, wg_b, scale)


def _qkv_kernel(x_hbm, g_ref, w_hbm, hg_ref, o_ref, x_buf, h_ref, w_buf, semx, semw,
                *, layer, n_norm_tiles, n_rows):
    i, j = pl.program_id(0), pl.program_id(1)
    n_cols = pl.num_programs(1)
    n_slots, _, tn = w_buf.shape
    step = i * n_cols + j
    n_steps = n_rows * n_cols

    def w_copy(s):
        col = pl.multiple_of((s % n_cols) * tn, tn)
        return pltpu.make_async_copy(w_hbm.at[layer, :, pl.ds(col, tn)], w_buf.at[s % n_slots],
                                     semw.at[s % n_slots])

    def x_copy(row):
        tm = x_buf.shape[0]
        return pltpu.make_async_copy(x_hbm.at[pl.ds(row * tm, tm), :], x_buf, semx.at[0])

    @pl.when(step == 0)
    def _():
        x_copy(0).start()
        w_copy(0).start()
        w_copy(1).start()

    @pl.when(step + 2 < n_steps)
    def _():
        w_copy(step + 2).start()

    w_copy(step).wait()

    def wdot(h):
        return _dot(h, w_buf[step % n_slots].astype(BF16))

    def normed_heads(h):
        y = wdot(h)
        for hd in range(y.shape[1] // ATT_HEAD_DIM):
            sl = slice(hd * ATT_HEAD_DIM, (hd + 1) * ATT_HEAD_DIM)
            o_ref[:, sl] = _rms_norm(y[:, sl], hg_ref[:, sl]).astype(BF16)

    @pl.when(j == 0)
    def _():
        x_copy(i).wait()
        h = _rms_norm(x_buf[...], g_ref[...]).astype(BF16)
        h_ref[...] = h
        normed_heads(h)

        @pl.when(i + 1 < n_rows)
        def _():
            x_copy(i + 1).start()

    @pl.when((j > 0) & (j < n_norm_tiles))
    def _():
        normed_heads(h_ref[...])

    @pl.when(j >= n_norm_tiles)
    def _():
        o_ref[...] = wdot(h_ref[...]).astype(BF16)


def _qkv_proj(x, gain, w_qkv, head_gains, layer):
    s, d = x.shape
    n = w_qkv.shape[2]
    tm, tn = min(ROW_TILE, s), QKV_COL_TILE
    n_norm_tiles = head_gains.shape[1] // tn
    n_rows = s // tm
    return pl.pallas_call(
        functools.partial(_qkv_kernel, layer=layer, n_norm_tiles=n_norm_tiles, n_rows=n_rows),
        grid=(n_rows, n // tn),
        in_specs=[
            pl.BlockSpec(memory_space=pl.ANY),
            pl.BlockSpec((1, d), lambda i, j: (0, 0)),
            pl.BlockSpec(memory_space=pl.ANY),
            pl.BlockSpec((1, tn), lambda i, j: (0, jnp.minimum(j, n_norm_tiles - 1))),
        ],
        out_specs=pl.BlockSpec((tm, tn), lambda i, j: (i, j)),
        out_shape=jax.ShapeDtypeStruct((s, n), BF16),
        scratch_shapes=[
            pltpu.VMEM((tm, d), F32),
            pltpu.VMEM((tm, d), BF16),
            pltpu.VMEM((QKV_WEIGHT_SLOTS, d, tn), F32),
            pltpu.SemaphoreType.DMA((1,)),
            pltpu.SemaphoreType.DMA((QKV_WEIGHT_SLOTS,)),
        ],
        compiler_params=_params("arbitrary", "arbitrary"),
        name="qkv_proj",
    )(x, gain, w_qkv, head_gains)


def _attn_kernel(q_ref, k0_ref, k1_ref, k2_ref, v0_ref, v1_ref, v2_ref, prof_ref, o_ref, bias_ref):
    b = pl.program_id(1)
    qb = q_ref.shape[0]
    n_local = q_ref.shape[1] // ATT_HEAD_DIM
    log2e = math.log2(math.e)
    scale = ATT_HEAD_DIM ** -0.5 * log2e

    @pl.when(b == 0)
    def _():
        q_idx = lax.broadcasted_iota(jnp.int32, (qb, 3 * qb), 0) + 2 * qb
        k_idx = lax.broadcasted_iota(jnp.int32, (qb, 3 * qb), 1)
        chunk_start = (q_idx // CHUNK) * CHUNK
        in_band = (k_idx >= chunk_start - ATT_LEFT_CHUNKS * CHUNK) & (k_idx < chunk_start + CHUNK)
        for hh in range(n_local):
            profile = jnp.broadcast_to(prof_ref[hh], (qb, prof_ref.shape[2]))
            table = pltpu.roll(profile, 0, 1, stride=1, stride_axis=0)[:, :3 * qb]
            bias_ref[hh] = jnp.where(in_band, table * log2e, MASK_VALUE)

    def lane_slabs(a):
        return [a[:, k * LANES:(k + 1) * LANES] for k in range(a.shape[1] // LANES)]

    def heads(first_blocks):
        for hh in range(n_local):
            cols = slice(hh * ATT_HEAD_DIM, (hh + 1) * ATT_HEAD_DIM)
            q = q_ref[:, cols]
            scores = []
            for c, k_ref in enumerate((k0_ref, k1_ref, k2_ref)):
                s_c = lax.dot_general(q, k_ref[:, cols], (((1,), (1,)), ((), ())),
                                      preferred_element_type=F32)
                s_c = s_c * scale + bias_ref[hh, :, c * qb:(c + 1) * qb]
                if first_blocks and c < 2:
                    s_c = jnp.where(b - 2 + c >= 0, s_c, MASK_VALUE)
                scores.append(s_c)
            m = jnp.max(functools.reduce(jnp.maximum,
                                         [sl for s_c in scores for sl in lane_slabs(s_c)]),
                        axis=-1, keepdims=True)
            acc = jnp.zeros((qb, 2 * ATT_HEAD_DIM), F32)
            ones = jnp.ones((qb, ATT_HEAD_DIM), BF16)
            for s_c, v_ref in zip(scores, (v0_ref, v1_ref, v2_ref)):
                p = jnp.exp2((s_c - m).astype(BF16))
                acc += _dot(p, jnp.concatenate([v_ref[:, cols], ones], axis=1))
            o_ref[:, cols] = (acc[:, :ATT_HEAD_DIM] / acc[:, ATT_HEAD_DIM:]).astype(BF16)

    @pl.when(b < 2)
    def _():
        heads(True)

    @pl.when(b >= 2)
    def _():
        heads(False)


def _attention(qkv, profile, n_heads):
    s = qkv.shape[0]
    qb = ATT_QBLOCK
    nb = s // qb
    hps = ATT_HEADS_PER_STEP
    width = hps * ATT_HEAD_DIM
    ngroups = n_heads // hps

    def kv_spec(offset, back):
        return pl.BlockSpec((qb, width), lambda h, b: (jnp.maximum(b - back, 0), offset + h))

    return pl.pallas_call(
        _attn_kernel,
        grid=(ngroups, nb),
        in_specs=[
            pl.BlockSpec((qb, width), lambda h, b: (b, h)),
            kv_spec(ngroups, 2), kv_spec(ngroups, 1), kv_spec(ngroups, 0),
            kv_spec(2 * ngroups, 2), kv_spec(2 * ngroups, 1), kv_spec(2 * ngroups, 0),
            pl.BlockSpec((hps, 1, 4 * qb), lambda h, b: (h, 0, 0)),
        ],
        out_specs=pl.BlockSpec((qb, width), lambda h, b: (b, h)),
        out_shape=jax.ShapeDtypeStruct((s, n_heads * ATT_HEAD_DIM), BF16),
        scratch_shapes=[pltpu.VMEM((hps, qb, 3 * qb), F32)],
        compiler_params=_params("arbitrary", "arbitrary"),
        name="chunk_attention",
    )(qkv, qkv, qkv, qkv, qkv, qkv, qkv, profile)


def _attention_profile(rel_bias):
    qb = ATT_QBLOCK
    n_heads = rel_bias.shape[0]
    assert REL_CLIP <= 2 * qb and qb <= REL_CLIP + 1 and 2 * REL_CLIP + 1 == rel_bias.shape[1]
    n_flat = 2 * qb - REL_CLIP + 1
    top = rel_bias[:, 2 * REL_CLIP:]
    profile = jnp.concatenate([
        jnp.broadcast_to(top, (n_heads, n_flat)),
        jnp.flip(rel_bias[:, REL_CLIP - qb + 1:2 * REL_CLIP], axis=1),
        jnp.broadcast_to(top, (n_heads, qb)),
    ], axis=1)
    return profile.reshape(n_heads, 1, 4 * qb).astype(F32)


def _ssm_prep_kernel(are_ref, aim_ref, ldt_ref, btr_ref, bti_ref, abr_ref, abi_ref, bbr_ref, bbi_ref):
    lam_r, lam_i = are_ref[...], aim_ref[...]
    dt = jnp.exp(ldt_ref[...])
    mag = jnp.exp(lam_r * dt)
    ab_r, ab_i = mag * jnp.cos(lam_i * dt), mag * jnp.sin(lam_i * dt)
    abr_ref[...] = ab_r
    abi_ref[...] = ab_i
    num_r, num_i = ab_r - 1.0, ab_i
    den = lam_r * lam_r + lam_i * lam_i
    co_r = (num_r * lam_r + num_i * lam_i) / den
    co_i = (num_i * lam_r - num_r * lam_i) / den
    for c in range(btr_ref.shape[0]):
        b_r, b_i = btr_ref[c], bti_ref[c]
        bbr_ref[c] = co_r * b_r - co_i * b_i
        bbi_ref[c] = co_r * b_i + co_i * b_r


def _ssm_prep(a_re, a_im, log_dt, bt_re, bt_im):
    g, n = a_re.shape
    gn = jax.ShapeDtypeStruct((g, n), F32)
    cgn = jax.ShapeDtypeStruct(bt_re.shape, F32)
    return pl.pallas_call(
        _ssm_prep_kernel,
        out_shape=(gn, gn, cgn, cgn),
        name="ssm_prep",
    )(a_re, a_im, log_dt, bt_re, bt_im)


def _ssm_kernel(x_ref, g_ref, d_ref, wbr_ref, wbi_ref, wcr_ref, wci_ref, ar_ref, ai_ref, o_ref,
                hn_ref, y_ref, xr_ref, xi_ref, cr_ref, ci_ref):
    i = pl.program_id(0)
    t = x_ref.shape[0]
    nblk, cb, ns = wbr_ref.shape
    n_batches, ppb = xr_ref.shape[0], xr_ref.shape[1]
    nsub = ns // LANES
    assert cb == LANES and 2 * nsub == SUBLANES and n_batches * ppb * 2 == nblk

    h = _rms_norm(x_ref[...], g_ref[...])
    for c in range(nblk):
        hn_ref[c] = h[:, c * LANES:(c + 1) * LANES]

    @pl.when(i == 0)
    def _():
        cr_ref[...] = jnp.zeros(cr_ref.shape, F32)
        ci_ref[...] = jnp.zeros(ci_ref.shape, F32)

    def slab_rows(half, k):
        return pl.ds(half * nsub + k, t, stride=SUBLANES)

    def blocks_of(batch):
        for g8 in range(2 * ppb):
            yield batch * 2 * ppb + g8, g8 // 2, g8 % 2

    def run(*stages):
        stages = list(stages)
        while stages:
            for stage in list(stages):
                if next(stage, "done") == "done":
                    stages.remove(stage)

    def project_in(batch):
        for gb, pair, half in blocks_of(batch):
            hb = hn_ref[gb].astype(BF16)
            for w_ref, dst_ref in ((wbr_ref, xr_ref), (wbi_ref, xi_ref)):
                bu = _dot(hb, w_ref[gb])
                for k in range(nsub):
                    dst_ref[batch, pair, slab_rows(half, k), :] = bu[:, k * LANES:(k + 1) * LANES]
            yield

    def scan(batch):
        coef = [(ar_ref[batch * ppb + p], ai_ref[batch * ppb + p]) for p in range(ppb)]
        steps_per_turn = t // (2 * ppb)

        def step(tt, carry):
            rows = pl.ds(tt * SUBLANES, SUBLANES)
            out = []
            for p in range(ppb):
                (a_r, a_i), x_r, x_i = coef[p], carry[2 * p], carry[2 * p + 1]
                n_r = a_r * x_r - a_i * x_i + xr_ref[batch, p, rows, :]
                n_i = a_r * x_i + a_i * x_r + xi_ref[batch, p, rows, :]
                xr_ref[batch, p, rows, :] = n_r
                xi_ref[batch, p, rows, :] = n_i
                out += [n_r, n_i]
            return tuple(out)

        last = tuple(ref[batch * ppb + p] for p in range(ppb) for ref in (cr_ref, ci_ref))
        for tt in range(t):
            last = step(tt, last)
            if (tt + 1) % steps_per_turn == 0 and tt + 1 < t:
                yield
        for p in range(ppb):
            cr_ref[batch * ppb + p] = last[2 * p]
            ci_ref[batch * ppb + p] = last[2 * p + 1]
        yield

    def project_out(batch):
        for gb, pair, half in blocks_of(batch):
            s_r, s_i = (jnp.concatenate([ref[batch, pair, slab_rows(half, k), :] for k in range(nsub)],
                                        axis=1).astype(BF16) for ref in (xr_ref, xi_ref))
            y_ref[gb] = _dot(s_r, wcr_ref[gb]) + _dot(s_i, wci_ref[gb])
            yield

    def finish(batch):
        for gb, _, _ in blocks_of(batch):
            lanes = slice(gb * LANES, (gb + 1) * LANES)
            o_ref[:, lanes] = jax.nn.gelu(y_ref[gb] + d_ref[:, lanes] * hn_ref[gb]).astype(BF16)
            yield

    run(project_in(0))
    for batch in range(n_batches):
        beside = [project_in(batch + 1)] if batch + 1 < n_batches else []
        beside += [project_out(batch - 1)] if batch > 0 else []
        beside += [finish(batch - 2)] if batch > 1 else []
        run(scan(batch), *beside)
    run(project_out(n_batches - 1), *([finish(n_batches - 2)] if n_batches > 1 else []))
    run(finish(n_batches - 1))


def _ssm_core(x, gain, d_skip, wb_r, wb_i, wc_r, wc_i, ab_r, ab_i):
    s, d = x.shape
    t = min(SSM_ROW_TILE, s)
    nblk, cb, ns = wb_r.shape
    ppb = SSM_PAIRS_PER_BATCH
    n_batches = nblk // (2 * ppb)

    def whole(a):
        return pl.BlockSpec(a.shape, lambda i: (0,) * a.ndim)

    return pl.pallas_call(
        _ssm_kernel,
        grid=(s // t,),
        in_specs=[
            pl.BlockSpec((t, d), lambda i: (i, 0)),
            whole(gain), whole(d_skip),
            whole(wb_r), whole(wb_i), whole(wc_r), whole(wc_i),
            whole(ab_r), whole(ab_i),
        ],
        out_specs=pl.BlockSpec((t, d), lambda i: (i, 0)),
        out_shape=jax.ShapeDtypeStruct((s, d), BF16),
        scratch_shapes=[
            pltpu.VMEM((nblk, t, LANES), F32),
            pltpu.VMEM((nblk, t, LANES), F32),
            pltpu.VMEM((n_batches, ppb, t * SUBLANES, LANES), F32),
            pltpu.VMEM((n_batches, ppb, t * SUBLANES, LANES), F32),
            pltpu.VMEM((nblk // 2, SUBLANES, LANES), F32),
            pltpu.VMEM((nblk // 2, SUBLANES, LANES), F32),
        ],
        compiler_params=_params("arbitrary"),
        name="ssm_core",
    )(x, gain, d_skip, wb_r, wb_i, wc_r, wc_i, ab_r, ab_i)


def _block_diag(w, groups_per_block):
    g, r, c = w.shape
    nblk = g // groups_per_block
    tiled = jnp.tile(w.reshape(nblk, groups_per_block * r, c), (1, 1, groups_per_block))
    row_group = lax.broadcasted_iota(jnp.int32, tiled.shape, 1) // r
    col_group = lax.broadcasted_iota(jnp.int32, tiled.shape, 2) // c
    return jnp.where(row_group == col_group, tiled, jnp.zeros_like(tiled))


def _s5_mixer(x, gain, a_re, a_im, log_dt, b_re, b_im, c_re, c_im, d_skip, w_glu, layer):
    s, d = x.shape
    g, n = a_re.shape
    gpb = SSM_GROUPS_PER_BLOCK
    bt_re, bt_im = b_re.transpose(2, 0, 1), b_im.transpose(2, 0, 1)
    ab_r, ab_i, bb_r, bb_i = _ssm_prep(a_re, a_im, log_dt.reshape(g, 1), bt_re, bt_im)
    wb_r = _block_diag(bb_r.transpose(1, 0, 2), gpb).astype(BF16)
    wb_i = _block_diag(bb_i.transpose(1, 0, 2), gpb).astype(BF16)
    wc_r = _block_diag(c_re.transpose(0, 2, 1), gpb).astype(BF16)
    wc_i = _block_diag(-c_im.transpose(0, 2, 1), gpb).astype(BF16)
    slabs = lambda a: a.reshape(g // (2 * gpb), 2 * gpb * n // LANES, LANES)
    z = _ssm_core(x, gain, d_skip.reshape(1, d), wb_r, wb_i, wc_r, wc_i, slabs(ab_r), slabs(ab_i))
    return _glu_res(z, w_glu, x, layer)


def kernel(x, norm_mix, norm_mlp, mlp_w1, mlp_w2, conv_w_in, conv_w, conv_w_out, pool_w_in, pool_w_group, pool_scale, att_w_qkv, att_q_norm, att_k_norm, att_rel_bias, att_w_out, ssm_a_re, ssm_a_im, ssm_log_dt, ssm_b_re, ssm_b_im, ssm_c_re, ssm_c_im, ssm_d, ssm_w_glu):
    b, s, d = x.shape
    depth = norm_mix.shape[0]
    n_mixers = 4
    outs = []
    for bi in range(b):
        xs = x[bi]
        for i in range(depth):
            kind, j = i % n_mixers, i // n_mixers
            gain = norm_mix[i].reshape(1, d)
            if kind == 0:
                gated = _conv_front(xs, gain, conv_w_in, conv_w[j], j)
                xs = _proj_res(gated, conv_w_out, xs, j)
            elif kind == 1:
                xs = _pool_mixer(xs, gain, pool_w_in, pool_w_group, pool_scale[j].reshape(1, d), j)
            elif kind == 2:
                n_heads = d // ATT_HEAD_DIM
                head_gains = jnp.concatenate([jnp.tile(att_q_norm[j], n_heads),
                                              jnp.tile(att_k_norm[j], n_heads)]).reshape(1, 2 * d)
                qkv = _qkv_proj(xs, gain, att_w_qkv, head_gains, j)
                att = _attention(qkv, _attention_profile(att_rel_bias[j]), n_heads)
                xs = _proj_res(att, att_w_out, xs, j)
            else:
                xs = _s5_mixer(xs, gain, ssm_a_re[j], ssm_a_im[j], ssm_log_dt[j], ssm_b_re[j],
                               ssm_b_im[j], ssm_c_re[j], ssm_c_im[j], ssm_d[j], ssm_w_glu, j)
            xs = _mlp(xs, norm_mlp[i].reshape(1, d), mlp_w1, mlp_w2, i)
        outs.append(xs)
    return outs[0][None] if b == 1 else jnp.stack(outs)
```

```python
import functools
import math

import jax
import jax.numpy as jnp
from jax import lax
from jax.experimental import pallas as pl
from jax.experimental.pallas import tpu as pltpu

F32 = jnp.float32
BF16 = jnp.bfloat16

RMS_EPS = 1e-6
CHUNK = 64
ATT_HEAD_DIM = 128
ATT_LEFT_CHUNKS = 8
REL_CLIP = 256
MASK_VALUE = -1e30
POOL_WINDOWS = (2, 4, 8, 16)
CONV_WIDTH = 3

SUBLANES = 8
LANES = 128
VMEM_LIMIT_BYTES = 56 * 1024 * 1024

ROW_TILE = 1024
CONV_COL_TILE = 512
QKV_COL_TILE = 1024
QKV_WEIGHT_SLOTS = 3
PROJ_ROW_TILE = 512
PROJ_COL_TILE = 2048
GLU_COL_TILE = 1024
FF_TILE = 512
MLP_OUT_TILE = 512
MLP_K_TILE = 2048
ATT_QBLOCK = 256
ATT_HEADS_PER_STEP = 16
SSM_ROW_TILE = 256
SSM_PAIRS_PER_BATCH = 4
SSM_GROUPS_PER_BLOCK = 8
HALO = 16
POOL_HALO = 32
POOL_ROW_TILE = 512


def _params(*semantics):
    return pltpu.CompilerParams(dimension_semantics=semantics, vmem_limit_bytes=VMEM_LIMIT_BYTES)


def _rms_norm(x, gain):
    return x * lax.rsqrt(jnp.mean(x * x, axis=-1, keepdims=True) + RMS_EPS) * gain


def _dot(a, b):
    return jnp.dot(a, b, preferred_element_type=F32)


def _wdot(a, w_ref):
    return _dot(a, w_ref[...].astype(BF16))


def _mlp_kernel(x_hbm, g_ref, w1_hbm, w2_hbm, xr_ref, o_ref, x_buf, h_ref, a_ref, w1_buf, w2_buf,
                semx, sem1, sem2, *, layer, n_rows):
    i, t = pl.program_id(0), pl.program_id(1)
    n_out = pl.num_programs(1) - 1
    tf, tk, tn = w1_buf.shape[2], w2_buf.shape[1], w2_buf.shape[2]
    n_up, n_k = a_ref.shape[1] // tf, a_ref.shape[1] // tk
    assert n_up % 2 == 0 and n_k % 2 == 0

    def w1_copy(f, slot):
        return pltpu.make_async_copy(w1_hbm.at[layer, :, pl.ds(f * tf, tf)], w1_buf.at[slot],
                                     sem1.at[slot])

    def w2_copy(n, kh):
        return pltpu.make_async_copy(w2_hbm.at[layer, pl.ds(kh * tk, tk), pl.ds(n * tn, tn)],
                                     w2_buf.at[kh % 2], sem2.at[kh % 2])

    def x_copy(row):
        tm = x_buf.shape[0]
        return pltpu.make_async_copy(x_hbm.at[pl.ds(row * tm, tm), :], x_buf, semx.at[0])

    @pl.when(t == 0)
    def _():
        @pl.when(i == 0)
        def _():
            x_copy(0).start()
            w1_copy(0, 0).start()

        w2_copy(0, 0).start()
        w2_copy(0, 1).start()

        def up_tile(f, slot, h):
            a = jnp.maximum(_dot(h, w1_buf[slot].astype(BF16)), 0.0)
            cols = pl.ds(f * tf, tf) if isinstance(f, int) else pl.ds(pl.multiple_of(f * tf, tf), tf)
            a_ref[:, cols] = (a * a).astype(BF16)

        x_copy(i).wait()
        w1_copy(0, 0).wait()
        w1_copy(1, 1).start()
        h = _rms_norm(x_buf[...], g_ref[...]).astype(BF16)
        h_ref[...] = h
        up_tile(0, 0, h)

        @pl.when(i + 1 < n_rows)
        def _():
            x_copy(i + 1).start()

        def up_pair(p, _):
            for f, slot in ((2 * p + 1, 1), (2 * p + 2, 0)):
                w1_copy(f, slot).wait()
                w1_copy(f + 1, 1 - slot).start()
                up_tile(f, slot, h_ref[...])
            return 0

        lax.fori_loop(0, (n_up - 2) // 2, up_pair, 0)
        w1_copy(n_up - 1, 1).wait()
        up_tile(n_up - 1, 1, h_ref[...])

    @pl.when(t > 0)
    def _():
        n = t - 1
        for kh in range(n_k):
            w2_copy(n, kh).wait()
            part = _dot(a_ref[:, kh * tk:(kh + 1) * tk], w2_buf[kh % 2].astype(BF16))
            if kh == 0:
                o_ref[...] = xr_ref[...] + part
            else:
                o_ref[...] += part
            if kh + 2 < n_k:
                w2_copy(n, kh + 2).start()
            else:
                @pl.when(n + 1 < n_out)
                def _(kh=kh):
                    w2_copy(n + 1, kh + 2 - n_k).start()

        @pl.when((n + 1 == n_out) & (i + 1 < n_rows))
        def _():
            w1_copy(0, 0).start()


def _mlp(x, gain, w1, w2, layer):
    s, d = x.shape
    dff = w1.shape[2]
    tm, tf, tn, tk = min(ROW_TILE, s), FF_TILE, MLP_OUT_TILE, MLP_K_TILE
    n_rows, n_out = s // tm, d // tn

    def out_tile(t):
        return jnp.maximum(t - 1, 0)

    return pl.pallas_call(
        functools.partial(_mlp_kernel, layer=layer, n_rows=n_rows),
        grid=(n_rows, 1 + n_out),
        in_specs=[
            pl.BlockSpec(memory_space=pl.ANY),
            pl.BlockSpec((1, d), lambda i, t: (0, 0)),
            pl.BlockSpec(memory_space=pl.ANY),
            pl.BlockSpec(memory_space=pl.ANY),
            pl.BlockSpec((tm, tn), lambda i, t: (i, out_tile(t))),
        ],
        out_specs=pl.BlockSpec((tm, tn), lambda i, t: (i, out_tile(t))),
        out_shape=jax.ShapeDtypeStruct((s, d), F32),
        scratch_shapes=[
            pltpu.VMEM((tm, d), F32),
            pltpu.VMEM((tm, d), BF16),
            pltpu.VMEM((tm, dff), BF16),
            pltpu.VMEM((2, d, tf), F32),
            pltpu.VMEM((2, tk, tn), F32),
            pltpu.SemaphoreType.DMA((1,)),
            pltpu.SemaphoreType.DMA((2,)),
            pltpu.SemaphoreType.DMA((2,)),
        ],
        compiler_params=_params("arbitrary", "arbitrary"),
        name="mlp",
    )(x, gain, w1, w2, x)


def _proj_res_kernel(a_ref, w_ref, x_hbm, o_ref, wb_ref, x_ring, semx):
    j, i = pl.program_id(0), pl.program_id(1)
    n_i = pl.num_programs(1)
    n_slots, tm, tn = x_ring.shape
    step = j * n_i + i
    n_steps = pl.num_programs(0) * n_i

    def x_copy(s):
        rows = pl.multiple_of((s % n_i) * tm, tm)
        cols = pl.multiple_of((s // n_i) * tn, tn)
        return pltpu.make_async_copy(x_hbm.at[pl.ds(rows, tm), pl.ds(cols, tn)],
                                     x_ring.at[s % n_slots], semx.at[s % n_slots])

    @pl.when(step == 0)
    def _():
        x_copy(0).start()

        @pl.when(n_steps > 1)
        def _():
            x_copy(1).start()

    @pl.when(step + 2 < n_steps)
    def _():
        x_copy(step + 2).start()

    @pl.when(i == 0)
    def _():
        wb_ref[...] = w_ref[...].astype(BF16)

    x_copy(step).wait()
    o_ref[...] = x_ring[step % n_slots] + _dot(a_ref[...], wb_ref[...])


def _proj_res(a, w, x, layer):
    s, k = a.shape
    n = w.shape[2]
    tm, tn = min(PROJ_ROW_TILE, s), min(PROJ_COL_TILE, n)
    return pl.pallas_call(
        _proj_res_kernel,
        grid=(n // tn, s // tm),
        in_specs=[
            pl.BlockSpec((tm, k), lambda j, i: (i, 0)),
            pl.BlockSpec((None, k, tn), lambda j, i: (layer, 0, j), pipeline_mode=pl.Buffered(1)),
            pl.BlockSpec(memory_space=pl.ANY),
        ],
        out_specs=pl.BlockSpec((tm, tn), lambda j, i: (i, j)),
        out_shape=jax.ShapeDtypeStruct((s, n), F32),
        scratch_shapes=[
            pltpu.VMEM((k, tn), BF16),
            pltpu.VMEM((3, tm, tn), F32),
            pltpu.SemaphoreType.DMA((3,)),
        ],
        compiler_params=_params("arbitrary", "arbitrary"),
        name="proj_res",
    )(a, w, x)


def _glu_res_kernel(a_ref, wv_ref, wg_ref, x_ref, o_ref, wvb_ref, wgb_ref):
    @pl.when(pl.program_id(1) == 0)
    def _():
        wvb_ref[...] = wv_ref[...].astype(BF16)
        wgb_ref[...] = wg_ref[...].astype(BF16)

    a = a_ref[...]
    val = _dot(a, wvb_ref[...])
    gate = _dot(a, wgb_ref[...])
    o_ref[...] = x_ref[...] + val * jax.nn.sigmoid(gate)


def _glu_res(a, w_glu, x, layer):
    s, k = a.shape
    n = w_glu.shape[2] // 2
    tm, tn = min(PROJ_ROW_TILE, s), min(GLU_COL_TILE, n)
    nj = n // tn
    return pl.pallas_call(
        _glu_res_kernel,
        grid=(nj, s // tm),
        in_specs=[
            pl.BlockSpec((tm, k), lambda j, i: (i, 0)),
            pl.BlockSpec((None, k, tn), lambda j, i: (layer, 0, j), pipeline_mode=pl.Buffered(1)),
            pl.BlockSpec((None, k, tn), lambda j, i: (layer, 0, j + nj), pipeline_mode=pl.Buffered(1)),
            pl.BlockSpec((tm, tn), lambda j, i: (i, j)),
        ],
        out_specs=pl.BlockSpec((tm, tn), lambda j, i: (i, j)),
        out_shape=jax.ShapeDtypeStruct((s, n), F32),
        scratch_shapes=[pltpu.VMEM((k, tn), BF16), pltpu.VMEM((k, tn), BF16)],
        compiler_params=_params("arbitrary", "arbitrary"),
        name="glu_res",
    )(a, w_glu, w_glu, x)


def _conv_front_kernel(x_ref, g_ref, wb_ref, wc_ref, wv_ref, cw_ref, o_ref, h_ref, ext_ref, carry_ref):
    i, j = pl.program_id(0), pl.program_id(1)
    tm = o_ref.shape[0]

    @pl.when(i == 0)
    def _():
        carry_ref[j] = jnp.zeros(carry_ref.shape[1:], F32)

    def column_tile(h):
        u = _wdot(h, wc_ref) * _wdot(h, wv_ref)
        ext_ref[0:HALO, :] = carry_ref[j]
        ext_ref[HALO:, :] = u
        carry_ref[j] = u[tm - HALO:, :]
        conv = cw_ref[CONV_WIDTH - 1:CONV_WIDTH, :] * u
        for lag in range(1, CONV_WIDTH):
            tap = CONV_WIDTH - 1 - lag
            conv += cw_ref[tap:tap + 1, :] * ext_ref[pl.ds(HALO - lag, tm), :]
        o_ref[...] = (_wdot(h, wb_ref) * conv).astype(BF16)

    @pl.when(j == 0)
    def _():
        h = _rms_norm(x_ref[...], g_ref[...]).astype(BF16)
        h_ref[...] = h
        column_tile(h)

    @pl.when(j > 0)
    def _():
        column_tile(h_ref[...])


def _conv_front(x, gain, w_in, conv_w, layer):
    s, d = x.shape
    tm, tn = min(ROW_TILE, s), CONV_COL_TILE
    nj = d // tn
    return pl.pallas_call(
        _conv_front_kernel,
        grid=(s // tm, nj),
        in_specs=[
            pl.BlockSpec((tm, d), lambda i, j: (i, 0)),
            pl.BlockSpec((1, d), lambda i, j: (0, 0)),
            pl.BlockSpec((None, d, tn), lambda i, j: (layer, 0, j)),
            pl.BlockSpec((None, d, tn), lambda i, j: (layer, 0, j + nj)),
            pl.BlockSpec((None, d, tn), lambda i, j: (layer, 0, j + 2 * nj)),
            pl.BlockSpec((CONV_WIDTH, tn), lambda i, j: (0, j)),
        ],
        out_specs=pl.BlockSpec((tm, tn), lambda i, j: (i, j)),
        out_shape=jax.ShapeDtypeStruct((s, d), BF16),
        scratch_shapes=[
            pltpu.VMEM((tm, d), BF16),
            pltpu.VMEM((tm + HALO, tn), F32),
            pltpu.VMEM((nj, HALO, tn), F32),
        ],
        compiler_params=_params("arbitrary", "arbitrary"),
        name="conv_front",
    )(x, gain, w_in, w_in, w_in, conv_w)


def _round_kernel(w_ref, o_ref):
    o_ref[...] = w_ref[...].astype(BF16)


def _round_bf16(w, layer, block_rows):
    _, r, c = w.shape
    return pl.pallas_call(
        _round_kernel,
        grid=(r // block_rows,),
        in_specs=[pl.BlockSpec((None, block_rows, c), lambda i: (layer, i, 0))],
        out_specs=pl.BlockSpec((block_rows, c), lambda i: (i, 0)),
        out_shape=jax.ShapeDtypeStruct((r, c), BF16),
        compiler_params=_params("parallel"),
        name="round_bf16",
    )(w)


def _pool_kernel(x_ref, g_ref, win_ref, wg_ref, sc_ref, o_ref, ext_ref, la_ref, lb_ref, carry_ref):
    i = pl.program_id(0)
    tm = o_ref.shape[0]
    halo = POOL_HALO
    n_ext = tm + halo
    pg = wg_ref.shape[1]

    @pl.when(i == 0)
    def _():
        carry_ref[...] = jnp.zeros(carry_ref.shape, F32)

    x = x_ref[...]
    h = _rms_norm(x, g_ref[...]).astype(BF16)
    pos = (i * tm + 1 + lax.broadcasted_iota(jnp.int32, (tm, 1), 0)).astype(F32)

    for gi, w in enumerate(POOL_WINDOWS):
        cols = slice(gi * pg, (gi + 1) * pg)
        u = _dot(h, win_ref[:, cols])
        ext_ref[gi, 0:halo, :] = carry_ref[gi]
        ext_ref[gi, halo:, :] = u
        carry_ref[gi] = u[tm - halo:, :]
        levels = int(math.log2(w))
        assert 2 ** levels == w and SUBLANES * levels <= halo
        src = ext_ref
        for lv in range(1, levels):
            dst = la_ref if lv % 2 else lb_ref
            lo, shift = SUBLANES * lv, 2 ** (lv - 1)
            dst[gi, lo:, :] = src[gi, lo:, :] + src[gi, pl.ds(lo - shift, n_ext - lo), :]
            src = dst
        acc = src[gi, halo:, :] + src[gi, pl.ds(halo - w // 2, tm), :]
        inv_count = 1.0 / jnp.minimum(pos, float(w))
        pooled = (acc * inv_count - u).astype(BF16)
        o_ref[:, cols] = x[:, cols] + _dot(pooled, wg_ref[gi]) * sc_ref[:, cols]


def _pool_mixer(x, gain, w_in, w_group, scale, layer):
    s, d = x.shape
    ng, pg = w_group.shape[1], w_group.shape[2]
    tm = min(POOL_ROW_TILE, s)
    win_b = _round_bf16(w_in, layer, d // ng)
    wg_b = _round_bf16(w_group.reshape(w_group.shape[0], ng * pg, pg), layer, pg).reshape(ng, pg, pg)
    return pl.pallas_call(
        _pool_kernel,
        grid=(s // tm,),
        in_specs=[
            pl.BlockSpec((tm, d), lambda i: (i, 0)),
            pl.BlockSpec((1, d), lambda i: (0, 0)),
            pl.BlockSpec((d, d), lambda i: (0, 0)),
            pl.BlockSpec((ng, pg, pg), lambda i: (0, 0, 0)),
            pl.BlockSpec((1, d), lambda i: (0, 0)),
        ],
        out_specs=pl.BlockSpec((tm, d), lambda i: (i, 0)),
        out_shape=jax.ShapeDtypeStruct((s, d), F32),
        scratch_shapes=[
            pltpu.VMEM((ng, tm + POOL_HALO, pg), F32),
            pltpu.VMEM((ng, tm + POOL_HALO, pg), F32),
            pltpu.VMEM((ng, tm + POOL_HALO, pg), F32),
            pltpu.VMEM((ng, POOL_HALO, pg), F32),
        ],
        compiler_params=_params("arbitrary"),
        name="pool_mixer",
    )(x, gain, win_b, wg_b, scale)


def _qkv_kernel(x_hbm, g_ref, w_hbm, hg_ref, o_ref, x_buf, h_ref, w_buf, semx, semw,
                *, layer, n_norm_tiles, n_rows):
    i, j = pl.program_id(0), pl.program_id(1)
    n_cols = pl.num_programs(1)
    n_slots, _, tn = w_buf.shape
    step = i * n_cols + j
    n_steps = n_rows * n_cols

    def w_copy(s):
        col = pl.multiple_of((s % n_cols) * tn, tn)
        return pltpu.make_async_copy(w_hbm.at[layer, :, pl.ds(col, tn)], w_buf.at[s % n_slots],
                                     semw.at[s % n_slots])

    def x_copy(row):
        tm = x_buf.shape[0]
        return pltpu.make_async_copy(x_hbm.at[pl.ds(row * tm, tm), :], x_buf, semx.at[0])

    @pl.when(step == 0)
    def _():
        x_copy(0).start()
        w_copy(0).start()
        w_copy(1).start()

    @pl.when(step + 2 < n_steps)
    def _():
        w_copy(step + 2).start()

    w_copy(step).wait()

    def wdot(h):
        return _dot(h, w_buf[step % n_slots].astype(BF16))

    def normed_heads(h):
        y = wdot(h)
        for hd in range(y.shape[1] // ATT_HEAD_DIM):
            sl = slice(hd * ATT_HEAD_DIM, (hd + 1) * ATT_HEAD_DIM)
            o_ref[:, sl] = _rms_norm(y[:, sl], hg_ref[:, sl]).astype(BF16)

    @pl.when(j == 0)
    def _():
        x_copy(i).wait()
        h = _rms_norm(x_buf[...], g_ref[...]).astype(BF16)
        h_ref[...] = h
        normed_heads(h)

        @pl.when(i + 1 < n_rows)
        def _():
            x_copy(i + 1).start()

    @pl.when((j > 0) & (j < n_norm_tiles))
    def _():
        normed_heads(h_ref[...])

    @pl.when(j >= n_norm_tiles)
    def _():
        o_ref[...] = wdot(h_ref[...]).astype(BF16)


def _qkv_proj(x, gain, w_qkv, head_gains, layer):
    s, d = x.shape
    n = w_qkv.shape[2]
    tm, tn = min(ROW_TILE, s), QKV_COL_TILE
    n_norm_tiles = head_gains.shape[1] // tn
    n_rows = s // tm
    return pl.pallas_call(
        functools.partial(_qkv_kernel, layer=layer, n_norm_tiles=n_norm_tiles, n_rows=n_rows),
        grid=(n_rows, n // tn),
        in_specs=[
            pl.BlockSpec(memory_space=pl.ANY),
            pl.BlockSpec((1, d), lambda i, j: (0, 0)),
            pl.BlockSpec(memory_space=pl.ANY),
            pl.BlockSpec((1, tn), lambda i, j: (0, jnp.minimum(j, n_norm_tiles - 1))),
        ],
        out_specs=pl.BlockSpec((tm, tn), lambda i, j: (i, j)),
        out_shape=jax.ShapeDtypeStruct((s, n), BF16),
        scratch_shapes=[
            pltpu.VMEM((tm, d), F32),
            pltpu.VMEM((tm, d), BF16),
            pltpu.VMEM((QKV_WEIGHT_SLOTS, d, tn), F32),
            pltpu.SemaphoreType.DMA((1,)),
            pltpu.SemaphoreType.DMA((QKV_WEIGHT_SLOTS,)),
        ],
        compiler_params=_params("arbitrary", "arbitrary"),
        name="qkv_proj",
    )(x, gain, w_qkv, head_gains)


def _attn_kernel(q_ref, k0_ref, k1_ref, k2_ref, v0_ref, v1_ref, v2_ref, prof_ref, o_ref, bias_ref):
    b = pl.program_id(1)
    qb = q_ref.shape[0]
    n_local = q_ref.shape[1] // ATT_HEAD_DIM
    log2e = math.log2(math.e)
    scale = ATT_HEAD_DIM ** -0.5 * log2e

    @pl.when(b == 0)
    def _():
        q_idx = lax.broadcasted_iota(jnp.int32, (qb, 3 * qb), 0) + 2 * qb
        k_idx = lax.broadcasted_iota(jnp.int32, (qb, 3 * qb), 1)
        chunk_start = (q_idx // CHUNK) * CHUNK
        in_band = (k_idx >= chunk_start - ATT_LEFT_CHUNKS * CHUNK) & (k_idx < chunk_start + CHUNK)
        for hh in range(n_local):
            profile = jnp.broadcast_to(prof_ref[hh], (qb, prof_ref.shape[2]))
            table = pltpu.roll(profile, 0, 1, stride=1, stride_axis=0)[:, :3 * qb]
            bias_ref[hh] = jnp.where(in_band, table * log2e, MASK_VALUE)

    def lane_slabs(a):
        return [a[:, k * LANES:(k + 1) * LANES] for k in range(a.shape[1] // LANES)]

    def heads(first_blocks):
        for hh in range(n_local):
            cols = slice(hh * ATT_HEAD_DIM, (hh + 1) * ATT_HEAD_DIM)
            q = q_ref[:, cols]
            scores = []
            for c, k_ref in enumerate((k0_ref, k1_ref, k2_ref)):
                s_c = lax.dot_general(q, k_ref[:, cols], (((1,), (1,)), ((), ())),
                                      preferred_element_type=F32)
                s_c = s_c * scale + bias_ref[hh, :, c * qb:(c + 1) * qb]
                if first_blocks and c < 2:
                    s_c = jnp.where(b - 2 + c >= 0, s_c, MASK_VALUE)
                scores.append(s_c)
            m = jnp.max(functools.reduce(jnp.maximum,
                                         [sl for s_c in scores for sl in lane_slabs(s_c)]),
                        axis=-1, keepdims=True)
            acc = jnp.zeros((qb, 2 * ATT_HEAD_DIM), F32)
            ones = jnp.ones((qb, ATT_HEAD_DIM), BF16)
            for s_c, v_ref in zip(scores, (v0_ref, v1_ref, v2_ref)):
                p = jnp.exp2((s_c - m).astype(BF16))
                acc += _dot(p, jnp.concatenate([v_ref[:, cols], ones], axis=1))
            o_ref[:, cols] = (acc[:, :ATT_HEAD_DIM] / acc[:, ATT_HEAD_DIM:]).astype(BF16)

    @pl.when(b < 2)
    def _():
        heads(True)

    @pl.when(b >= 2)
    def _():
        heads(False)


def _attention(qkv, profile, n_heads):
    s = qkv.shape[0]
    qb = ATT_QBLOCK
    nb = s // qb
    hps = ATT_HEADS_PER_STEP
    width = hps * ATT_HEAD_DIM
    ngroups = n_heads // hps

    def kv_spec(offset, back):
        return pl.BlockSpec((qb, width), lambda h, b: (jnp.maximum(b - back, 0), offset + h))

    return pl.pallas_call(
        _attn_kernel,
        grid=(ngroups, nb),
        in_specs=[
            pl.BlockSpec((qb, width), lambda h, b: (b, h)),
            kv_spec(ngroups, 2), kv_spec(ngroups, 1), kv_spec(ngroups, 0),
            kv_spec(2 * ngroups, 2), kv_spec(2 * ngroups, 1), kv_spec(2 * ngroups, 0),
            pl.BlockSpec((hps, 1, 4 * qb), lambda h, b: (h, 0, 0)),
        ],
        out_specs=pl.BlockSpec((qb, width), lambda h, b: (b, h)),
        out_shape=jax.ShapeDtypeStruct((s, n_heads * ATT_HEAD_DIM), BF16),
        scratch_shapes=[pltpu.VMEM((hps, qb, 3 * qb), F32)],
        compiler_params=_params("arbitrary", "arbitrary"),
        name="chunk_attention",
    )(qkv, qkv, qkv, qkv, qkv, qkv, qkv, profile)


def _attention_profile(rel_bias):
    qb = ATT_QBLOCK
    n_heads = rel_bias.shape[0]
    assert REL_CLIP <= 2 * qb and qb <= REL_CLIP + 1 and 2 * REL_CLIP + 1 == rel_bias.shape[1]
    n_flat = 2 * qb - REL_CLIP + 1
    top = rel_bias[:, 2 * REL_CLIP:]
    profile = jnp.concatenate([
        jnp.broadcast_to(top, (n_heads, n_flat)),
        jnp.flip(rel_bias[:, REL_CLIP - qb + 1:2 * REL_CLIP], axis=1),
        jnp.broadcast_to(top, (n_heads, qb)),
    ], axis=1)
    return profile.reshape(n_heads, 1, 4 * qb).astype(F32)


def _ssm_prep_kernel(are_ref, aim_ref, ldt_ref, btr_ref, bti_ref, abr_ref, abi_ref, bbr_ref, bbi_ref):
    lam_r, lam_i = are_ref[...], aim_ref[...]
    dt = jnp.exp(ldt_ref[...])
    mag = jnp.exp(lam_r * dt)
    ab_r, ab_i = mag * jnp.cos(lam_i * dt), mag * jnp.sin(lam_i * dt)
    abr_ref[...] = ab_r
    abi_ref[...] = ab_i
    num_r, num_i = ab_r - 1.0, ab_i
    den = lam_r * lam_r + lam_i * lam_i
    co_r = (num_r * lam_r + num_i * lam_i) / den
    co_i = (num_i * lam_r - num_r * lam_i) / den
    for c in range(btr_ref.shape[0]):
        b_r, b_i = btr_ref[c], bti_ref[c]
        bbr_ref[c] = co_r * b_r - co_i * b_i
        bbi_ref[c] = co_r * b_i + co_i * b_r


def _ssm_prep(a_re, a_im, log_dt, bt_re, bt_im):
    g, n = a_re.shape
    gn = jax.ShapeDtypeStruct((g, n), F32)
    cgn = jax.ShapeDtypeStruct(bt_re.shape, F32)
    return pl.pallas_call(
        _ssm_prep_kernel,
        out_shape=(gn, gn, cgn, cgn),
        name="ssm_prep",
    )(a_re, a_im, log_dt, bt_re, bt_im)


def _ssm_kernel(x_ref, g_ref, d_ref, wbr_ref, wbi_ref, wcr_ref, wci_ref, ar_ref, ai_ref, o_ref,
                hn_ref, y_ref, xr_ref, xi_ref, cr_ref, ci_ref):
    i = pl.program_id(0)
    t = x_ref.shape[0]
    nblk, cb, ns = wbr_ref.shape
    n_batches, ppb = xr_ref.shape[0], xr_ref.shape[1]
    nsub = ns // LANES
    assert cb == LANES and 2 * nsub == SUBLANES and n_batches * ppb * 2 == nblk

    h = _rms_norm(x_ref[...], g_ref[...])
    for c in range(nblk):
        hn_ref[c] = h[:, c * LANES:(c + 1) * LANES]

    @pl.when(i == 0)
    def _():
        cr_ref[...] = jnp.zeros(cr_ref.shape, F32)
        ci_ref[...] = jnp.zeros(ci_ref.shape, F32)

    def slab_rows(half, k):
        return pl.ds(half * nsub + k, t, stride=SUBLANES)

    def blocks_of(batch):
        for g8 in range(2 * ppb):
            yield batch * 2 * ppb + g8, g8 // 2, g8 % 2

    def run(*stages):
        stages = list(stages)
        while stages:
            for stage in list(stages):
                if next(stage, "done") == "done":
                    stages.remove(stage)

    def project_in(batch):
        for gb, pair, half in blocks_of(batch):
            hb = hn_ref[gb].astype(BF16)
            for w_ref, dst_ref in ((wbr_ref, xr_ref), (wbi_ref, xi_ref)):
                bu = _dot(hb, w_ref[gb])
                for k in range(nsub):
                    dst_ref[batch, pair, slab_rows(half, k), :] = bu[:, k * LANES:(k + 1) * LANES]
            yield

    def scan(batch):
        coef = [(ar_ref[batch * ppb + p], ai_ref[batch * ppb + p]) for p in range(ppb)]
        steps_per_turn = t // (2 * ppb)

        def step(tt, carry):
            rows = pl.ds(tt * SUBLANES, SUBLANES)
            out = []
            for p in range(ppb):
                (a_r, a_i), x_r, x_i = coef[p], carry[2 * p], carry[2 * p + 1]
                n_r = a_r * x_r - a_i * x_i + xr_ref[batch, p, rows, :]
                n_i = a_r * x_i + a_i * x_r + xi_ref[batch, p, rows, :]
                xr_ref[batch, p, rows, :] = n_r
                xi_ref[batch, p, rows, :] = n_i
                out += [n_r, n_i]
            return tuple(out)

        last = tuple(ref[batch * ppb + p] for p in range(ppb) for ref in (cr_ref, ci_ref))
        for tt in range(t):
            last = step(tt, last)
            if (tt + 1) % steps_per_turn == 0 and tt + 1 < t:
                yield
        for p in range(ppb):
            cr_ref[batch * ppb + p] = last[2 * p]
            ci_ref[batch * ppb + p] = last[2 * p + 1]
        yield

    def project_out(batch):
        for gb, pair, half in blocks_of(batch):
            s_r, s_i = (jnp.concatenate([ref[batch, pair, slab_rows(half, k), :] for k in range(nsub)],
                                        axis=1).astype(BF16) for ref in (xr_ref, xi_ref))
            y_ref[gb] = _dot(s_r, wcr_ref[gb]) + _dot(s_i, wci_ref[gb])
            yield

    def finish(batch):
        for gb, _, _ in blocks_of(batch):
            lanes = slice(gb * LANES, (gb + 1) * LANES)
            o_ref[:, lanes] = jax.nn.gelu(y_ref[gb] + d_ref[:, lanes] * hn_ref[gb]).astype(BF16)
            yield

    run(project_in(0))
    for batch in range(n_batches):
        beside = [project_in(batch + 1)] if batch + 1 < n_batches else []
        beside += [project_out(batch - 1)] if batch > 0 else []
        beside += [finish(batch - 2)] if batch > 1 else []
        run(scan(batch), *beside)
    run(project_out(n_batches - 1), *([finish(n_batches - 2)] if n_batches > 1 else []))
    run(finish(n_batches - 1))


def _ssm_core(x, gain, d_skip, wb_r, wb_i, wc_r, wc_i, ab_r, ab_i):
    s, d = x.shape
    t = min(SSM_ROW_TILE, s)
    nblk, cb, ns = wb_r.shape
    ppb = SSM_PAIRS_PER_BATCH
    n_batches = nblk // (2 * ppb)

    def whole(a):
        return pl.BlockSpec(a.shape, lambda i: (0,) * a.ndim)

    return pl.pallas_call(
        _ssm_kernel,
        grid=(s // t,),
        in_specs=[
            pl.BlockSpec((t, d), lambda i: (i, 0)),
            whole(gain), whole(d_skip),
            whole(wb_r), whole(wb_i), whole(wc_r), whole(wc_i),
            whole(ab_r), whole(ab_i),
        ],
        out_specs=pl.BlockSpec((t, d), lambda i: (i, 0)),
        out_shape=jax.ShapeDtypeStruct((s, d), BF16),
        scratch_shapes=[
            pltpu.VMEM((nblk, t, LANES), F32),
            pltpu.VMEM((nblk, t, LANES), F32),
            pltpu.VMEM((n_batches, ppb, t * SUBLANES, LANES), F32),
            pltpu.VMEM((n_batches, ppb, t * SUBLANES, LANES), F32),
            pltpu.VMEM((nblk // 2, SUBLANES, LANES), F32),
            pltpu.VMEM((nblk // 2, SUBLANES, LANES), F32),
        ],
        compiler_params=_params("arbitrary"),
        name="ssm_core",
    )(x, gain, d_skip, wb_r, wb_i, wc_r, wc_i, ab_r, ab_i)


def _block_diag(w, groups_per_block):
    g, r, c = w.shape
    nblk = g // groups_per_block
    tiled = jnp.tile(w.reshape(nblk, groups_per_block * r, c), (1, 1, groups_per_block))
    row_group = lax.broadcasted_iota(jnp.int32, tiled.shape, 1) // r
    col_group = lax.broadcasted_iota(jnp.int32, tiled.shape, 2) // c
    return jnp.where(row_group == col_group, tiled, jnp.zeros_like(tiled))


def _s5_mixer(x, gain, a_re, a_im, log_dt, b_re, b_im, c_re, c_im, d_skip, w_glu, layer):
    s, d = x.shape
    g, n = a_re.shape
    gpb = SSM_GROUPS_PER_BLOCK
    bt_re, bt_im = b_re.transpose(2, 0, 1), b_im.transpose(2, 0, 1)
    ab_r, ab_i, bb_r, bb_i = _ssm_prep(a_re, a_im, log_dt.reshape(g, 1), bt_re, bt_im)
    wb_r = _block_diag(bb_r.transpose(1, 0, 2), gpb).astype(BF16)
    wb_i = _block_diag(bb_i.transpose(1, 0, 2), gpb).astype(BF16)
    wc_r = _block_diag(c_re.transpose(0, 2, 1), gpb).astype(BF16)
    wc_i = _block_diag(-c_im.transpose(0, 2, 1), gpb).astype(BF16)
    slabs = lambda a: a.reshape(g // (2 * gpb), 2 * gpb * n // LANES, LANES)
    z = _ssm_core(x, gain, d_skip.reshape(1, d), wb_r, wb_i, wc_r, wc_i, slabs(ab_r), slabs(ab_i))
    return _glu_res(z, w_glu, x, layer)


def kernel(x, norm_mix, norm_mlp, mlp_w1, mlp_w2, conv_w_in, conv_w, conv_w_out, pool_w_in, pool_w_group, pool_scale, att_w_qkv, att_q_norm, att_k_norm, att_rel_bias, att_w_out, ssm_a_re, ssm_a_im, ssm_log_dt, ssm_b_re, ssm_b_im, ssm_c_re, ssm_c_im, ssm_d, ssm_w_glu):
    b, s, d = x.shape
    depth = norm_mix.shape[0]
    n_mixers = 4
    outs = []
    for bi in range(b):
        xs = x[bi]
        for i in range(depth):
            kind, j = i % n_mixers, i // n_mixers
            gain = norm_mix[i].reshape(1, d)
            if kind == 0:
                gated = _conv_front(xs, gain, conv_w_in, conv_w[j], j)
                xs = _proj_res(gated, conv_w_out, xs, j)
            elif kind == 1:
                xs = _pool_mixer(xs, gain, pool_w_in, pool_w_group, pool_scale[j].reshape(1, d), j)
            elif kind == 2:
                n_heads = d // ATT_HEAD_DIM
                head_gains = jnp.concatenate([jnp.tile(att_q_norm[j], n_heads),
                                              jnp.tile(att_k_norm[j], n_heads)]).reshape(1, 2 * d)
                qkv = _qkv_proj(xs, gain, att_w_qkv, head_gains, j)
                att = _attention(qkv, _attention_profile(att_rel_bias[j]), n_heads)
                xs = _proj_res(att, att_w_out, xs, j)
            else:
                xs = _s5_mixer(xs, gain, ssm_a_re[j], ssm_a_im[j], ssm_log_dt[j], ssm_b_re[j],
                               ssm_b_im[j], ssm_c_re[j], ssm_c_im[j], ssm_d[j], ssm_w_glu, j)
            xs = _mlp(xs, norm_mlp[i].reshape(1, d), mlp_w1, mlp_w2, i)
        outs.append(xs)
    return outs[0][None] if b == 1 else jnp.stack(outs)
```
